```python
import math
import jax, jax.numpy as jnp
from jax import lax
import numpy as np

D_MODEL = 2048
BATCH = 8
SEQ = 2048
DEPTH = 2

CTX_LEN = 256
GRID_W = 64
EPS = 1e-6
NEG_INF = -1e30
N_MOD = 6

D_MIX = D_MODEL
DH = 64
DK_A = 128
D_A = D_MIX // 2
H_A = D_A // DK_A
CONV_W = 5
CHUNK = 64
D_B = D_MIX // 4
H_B = D_B // DH
KV_B = 2
WIN = 128
QBLK = 128
ROPE_BASE = 10000.0
D_C = D_MIX - D_A - D_B
H_C = D_C // DH
KH_MAX = 8
KW = 16
NA_QCOLS = 16
NA_KCOLS = 32
N_IN = 3 * D_A + D_A + 4 * H_A + D_B + 2 * KV_B * DH + 3 * D_C
N_GROUPS = 4
E_PER_GROUP = 8
N_EXPERTS = N_GROUPS * E_PER_GROUP
TOP_K = 2
D_EXPERT = 1408
MOE_BLK = 128

kernel_name = "hybrid_dit_gdn_swa_natten_hmoe"


def rmsnorm(x, gain):
    x32 = x.astype(jnp.float32)
    y = x32 * lax.rsqrt(jnp.mean(x32 * x32, -1, keepdims=True) + EPS)
    return (y * gain).astype(x.dtype)


def modulate(h, shift, scale):
    return h * (1 + scale) + shift


def l2norm(u):
    u32 = u.astype(jnp.float32)
    return u32 * lax.rsqrt(jnp.sum(u32 * u32, -1, keepdims=True) + EPS)


def split_projection(p):
    sizes = (3 * D_A, D_A, 4 * H_A, D_B, 2 * KV_B * DH, D_C, 2 * D_C)
    bounds = [sum(sizes[:i + 1]) for i in range(len(sizes) - 1)]
    return jnp.split(p, bounds, axis=-1)


def axial_rope_tables(n_tokens, head_dim):
    t = jnp.arange(n_tokens)
    row = (t // GRID_W).astype(jnp.float32)
    col = (t % GRID_W).astype(jnp.float32)
    half = head_dim // 2
    inv = ROPE_BASE ** (-jnp.arange(0, half, 2, dtype=jnp.float32) / half)
    ang_r = row[:, None] * inv[None]
    ang_c = col[:, None] * inv[None]
    ang = jnp.concatenate([ang_r, ang_r, ang_c, ang_c], -1)
    return jnp.cos(ang), jnp.sin(ang)


def apply_axial_rope(x, cos, sin):
    half = x.shape[-1] // 2

    def rot(u):
        u1, u2 = jnp.split(u, 2, -1)
        return jnp.concatenate([-u2, u1], -1)

    xr = jnp.concatenate([rot(x[..., :half]), rot(x[..., half:])], -1)
    return (x * cos[None, :, None, :] + xr * sin[None, :, None, :]).astype(x.dtype)


def softmax_with_sink(s, sink):
    sk = jnp.broadcast_to(sink, s.shape[:-1] + (1,))
    p = jax.nn.softmax(jnp.concatenate([s, sk], -1), -1)
    return p[..., :-1]


def centred_short_conv(u, w):
    pad = CONV_W // 2
    L = u.shape[1]
    up = jnp.pad(u, ((0, 0), (pad, pad), (0, 0)))
    return sum(up[:, i:i + L] * w[i] for i in range(CONV_W))


def gdn_prepare(qkv, ab, conv_w, a_log, dt_bias):
    B, L, _ = qkv.shape
    qkv = jax.nn.silu(centred_short_conv(qkv, conv_w))
    q, k, v = jnp.split(qkv, 3, -1)
    q = l2norm(q.reshape(B, L, H_A, DK_A))
    k = l2norm(k.reshape(B, L, H_A, DK_A))
    v = v.reshape(B, L, H_A, DK_A)
    ab = ab.astype(jnp.float32).reshape(B, L, 2, 2, H_A)
    g = -jnp.exp(a_log) * jax.nn.softplus(ab[:, :, :, 0] + dt_bias)
    beta = jax.nn.sigmoid(ab[:, :, :, 1])
    return q, k, v, g, beta


def gdn_chunked(q, k, v, g, beta, s0):
    B, L, H, dk = k.shape
    dv = v.shape[-1]
    n = L // CHUNK
    f32 = jnp.float32

    def chunks(u):
        return u.astype(f32).reshape(B, n, CHUNK, H, -1).transpose(0, 3, 1, 2, 4)

    qc, kc, vc = chunks(q * dk ** -0.5), chunks(k), chunks(v)
    gc = g.astype(f32).reshape(B, n, CHUNK, H).transpose(0, 3, 1, 2)
    bc = beta.astype(f32).reshape(B, n, CHUNK, H).transpose(0, 3, 1, 2)
    gcum = jnp.cumsum(gc, -1)
    tril = jnp.tril(jnp.ones((CHUNK, CHUNK), bool))
    strict = jnp.tril(jnp.ones((CHUNK, CHUNK), bool), -1)
    diff = gcum[..., :, None] - gcum[..., None, :]
    decay = jnp.where(tril, jnp.exp(jnp.where(tril, diff, 0.0)), 0.0)
    kb = kc * bc[..., None]
    vb = vc * bc[..., None]
    a = jnp.where(strict, jnp.einsum('bhnid,bhnjd->bhnij', kb, kc) * decay, 0.0)
    eye = jnp.eye(CHUNK, dtype=f32)
    tinv = lax.linalg.triangular_solve(eye + a, jnp.broadcast_to(eye, a.shape),
                                       left_side=True, lower=True, unit_diagonal=True)
    u = tinv @ vb
    w = tinv @ (kb * jnp.exp(gcum)[..., None])
    attn = jnp.einsum('bhnid,bhnjd->bhnij', qc, kc) * decay
    qg = qc * jnp.exp(gcum)[..., None]
    kdec = kc * jnp.exp(gcum[..., -1:] - gcum)[..., None]
    glast = jnp.exp(gcum[..., -1])
    xs = tuple(jnp.moveaxis(t, 2, 0) for t in (u, w, attn, qg, kdec, glast))

    def step(s, inp):
        u_i, w_i, attn_i, qg_i, kdec_i, gl_i = inp
        v_new = u_i - w_i @ s
        o_i = qg_i @ s + attn_i @ v_new
        s = s * gl_i[..., None, None] + jnp.einsum('bhcd,bhce->bhde', kdec_i, v_new)
        return s, o_i

    s_final, o = lax.scan(step, s0.astype(f32), xs)
    o = jnp.moveaxis(o, 0, 2).transpose(0, 2, 3, 1, 4).reshape(B, L, H, dv)
    return o, s_final


def gdn_output(o, z, norm_w):
    B, L = z.shape[:2]
    o = o * lax.rsqrt(jnp.mean(o * o, -1, keepdims=True) + EPS) * norm_w
    zg = jax.nn.silu(z.reshape(B, L, H_A, DK_A).astype(jnp.float32))
    return (o * zg).reshape(B, L, D_A).astype(z.dtype)


def gated_deltanet(qkv_c, ab_c, z_c, qkv_l, ab_l, z_l, conv_w, a_log, dt_bias, norm_w, ctx_out):
    qc, kc, vc, gc, bc = gdn_prepare(qkv_c, ab_c, conv_w, a_log, dt_bias)
    ql, kl, vl, gl, bl = gdn_prepare(qkv_l, ab_l, conv_w, a_log, dt_bias)
    B = ql.shape[0]
    s0 = jnp.zeros((B, H_A, DK_A, DK_A), jnp.float32)
    o_c_sum, o_l_sum = 0.0, 0.0
    for d in range(2):
        seq_c = (qc, kc, vc, gc[:, :, d], bc[:, :, d])
        seq_l = (ql, kl, vl, gl[:, :, d], bl[:, :, d])
        if d == 1:
            seq_c = tuple(jnp.flip(t, 1) for t in seq_c)
            seq_l = tuple(jnp.flip(t, 1) for t in seq_l)
        o_c, s_c = gdn_chunked(*seq_c, s0)
        o_l, _ = gdn_chunked(*seq_l, s_c)
        if d == 1:
            o_c, o_l = jnp.flip(o_c, 1), jnp.flip(o_l, 1)
        o_c_sum = o_c_sum + o_c
        o_l_sum = o_l_sum + o_l
    o_lat = gdn_output(o_l_sum, z_l, norm_w)
    o_ctx = gdn_output(o_c_sum, z_c, norm_w) if ctx_out else None
    return o_ctx, o_lat


def window_gqa(q_c, kv_c, q_l, kv_l, sink, cos, sin, ctx_out):
    B, L = q_l.shape[:2]
    Lc = q_c.shape[1]
    G = H_B // KV_B
    n = L // QBLK
    J = 3 * QBLK
    scale = DH ** -0.5
    f32 = jnp.float32
    k_l, v_l = [u.reshape(B, L, KV_B, DH) for u in jnp.split(kv_l, 2, -1)]
    k_c, v_c = [u.reshape(B, Lc, KV_B, DH) for u in jnp.split(kv_c, 2, -1)]
    q_l = apply_axial_rope(q_l.reshape(B, L, H_B, DH), cos, sin)
    k_l = apply_axial_rope(k_l, cos, sin)
    sink = sink.astype(f32).reshape(KV_B, G, 1, 1)

    def band(u):
        up = jnp.pad(u, ((0, 0), (QBLK, QBLK), (0, 0), (0, 0))).reshape(B, n + 2, QBLK, KV_B, DH)
        return jnp.concatenate([up[:, :-2], up[:, 1:-1], up[:, 2:]], axis=2)

    k_w, v_w = band(k_l), band(v_l)
    qb = q_l.reshape(B, n, QBLK, KV_B, G, DH)
    qi = jnp.arange(QBLK)[:, None]
    kj = jnp.arange(J)[None]
    kpos = (jnp.arange(n) * QBLK)[:, None, None] - QBLK + kj[None]
    in_win = (jnp.abs(kj - QBLK - qi)[None] <= WIN) & (kpos >= 0) & (kpos < L)
    s_w = jnp.einsum('bnqkgd,bnjkd->bnkgqj', qb, k_w).astype(f32) * scale
    s_w = jnp.where(in_win[None, :, None, None], s_w, NEG_INF)
    s_x = jnp.einsum('bnqkgd,bckd->bnkgqc', qb, k_c).astype(f32) * scale
    p = softmax_with_sink(jnp.concatenate([s_w, s_x], -1), sink).astype(v_l.dtype)
    o = (jnp.einsum('bnkgqj,bnjkd->bnqkgd', p[..., :J], v_w)
         + jnp.einsum('bnkgqc,bckd->bnqkgd', p[..., J:], v_c))
    o_lat = o.reshape(B, L, D_B)
    o_ctx = None
    if ctx_out:
        qc = q_c.reshape(B, Lc, KV_B, G, DH)
        s_c = jnp.einsum('bqkgd,bckd->bkgqc', qc, k_c).astype(f32) * scale
        p_c = softmax_with_sink(s_c, sink).astype(v_c.dtype)
        o_ctx = jnp.einsum('bkgqc,bckd->bqkgd', p_c, v_c).reshape(B, Lc, D_B)
    return o_ctx, o_lat


def neighbourhood_attn(q_c, kv_c, q_l, kv_l, rpb, ctx_out):
    B, L = q_l.shape[:2]
    Lc = q_c.shape[1]
    rows = L // GRID_W
    kh = min(KH_MAX, rows)
    ncb = GRID_W // NA_QCOLS
    m = kh * NA_KCOLS
    scale = DH ** -0.5
    f32 = jnp.float32
    k_l, v_l = [u.reshape(B, L, H_C, DH) for u in jnp.split(kv_l, 2, -1)]
    k_c, v_c = [u.reshape(B, Lc, H_C, DH) for u in jnp.split(kv_c, 2, -1)]
    r = jnp.arange(rows)
    row_idx = jnp.clip(r - kh // 2, 0, rows - kh)[:, None] + jnp.arange(kh)[None]
    cb = jnp.arange(ncb)
    col_idx = (jnp.clip(cb * NA_QCOLS - KW // 2, 0, GRID_W - NA_KCOLS)[:, None]
               + jnp.arange(NA_KCOLS)[None])

    def gather_band(u):
        u = u.reshape(B, rows, GRID_W, H_C, DH)[:, row_idx]
        u = u[:, :, :, col_idx]
        return u.transpose(0, 1, 3, 2, 4, 5, 6).reshape(B, rows, ncb, m, H_C, DH)

    k_n, v_n = gather_band(k_l), gather_band(v_l)
    qn = q_l.reshape(B, rows, ncb, NA_QCOLS, H_C, DH)
    q_col = cb[:, None] * NA_QCOLS + jnp.arange(NA_QCOLS)[None]
    q_cs = jnp.clip(q_col - KW // 2, 0, GRID_W - KW)
    key_col = jnp.tile(col_idx, (1, kh))
    key_row = jnp.repeat(row_idx, NA_KCOLS, axis=1)
    in_win = ((key_col[:, None, :] >= q_cs[:, :, None])
              & (key_col[:, None, :] < q_cs[:, :, None] + KW))
    dr = key_row - r[:, None] + (KH_MAX - 1)
    dc = jnp.clip(key_col[:, None, :] - q_col[:, :, None] + (KW - 1), 0, 2 * KW - 2)
    bias = rpb[:, dr[:, None, None, :], dc[None]].astype(f32)
    s_nb = jnp.einsum('brcqhd,brcmhd->bhrcqm', qn, k_n).astype(f32) * scale + bias
    s_nb = jnp.where(in_win, s_nb, NEG_INF)
    s_x = jnp.einsum('brcqhd,bkhd->bhrcqk', qn, k_c).astype(f32) * scale
    p = jax.nn.softmax(jnp.concatenate([s_nb, s_x], -1), -1).astype(v_l.dtype)
    o = (jnp.einsum('bhrcqm,brcmhd->brcqhd', p[..., :m], v_n)
         + jnp.einsum('bhrcqk,bkhd->brcqhd', p[..., m:], v_c))
    o_lat = o.reshape(B, L, D_C)
    o_ctx = None
    if ctx_out:
        qc = q_c.reshape(B, Lc, H_C, DH)
        s_c = jnp.einsum('bqhd,bkhd->bhqk', qc, k_c).astype(f32) * scale
        p_c = jax.nn.softmax(s_c, -1).astype(v_c.dtype)
        o_ctx = jnp.einsum('bhqk,bkhd->bqhd', p_c, v_c).reshape(B, Lc, D_C)
    return o_ctx, o_lat


def hierarchical_moe(h, w_rg, b_rg, w_re, b_re, w_gate, w_up, w_down):
    T, D = h.shape
    f32 = jnp.float32
    g_logits = (h @ w_rg).astype(f32) + b_rg
    g_prob = jax.nn.softmax(g_logits, -1)
    g_sel = jnp.argmax(g_logits, -1)
    e_logits = ((h @ w_re).astype(f32) + b_re).reshape(T, N_GROUPS, E_PER_GROUP)
    e_in = jnp.take_along_axis(e_logits, g_sel[:, None, None], 1)[:, 0]
    top_v, top_i = lax.top_k(e_in, TOP_K)
    wts = jax.nn.softmax(top_v, -1) * jnp.take_along_axis(g_prob, g_sel[:, None], 1)
    expert = g_sel[:, None] * E_PER_GROUP + top_i
    S = T * TOP_K
    e_flat = expert.reshape(S)
    order = jnp.argsort(e_flat)
    e_sorted = e_flat[order]
    tok_sorted = order // TOP_K
    w_sorted = wts.reshape(S)[order]
    counts = jnp.bincount(e_flat, length=N_EXPERTS)
    starts = jnp.cumsum(counts) - counts
    padded = (counts + MOE_BLK - 1) // MOE_BLK * MOE_BLK
    pends = jnp.cumsum(padded)
    pstarts = pends - padded
    dest = pstarts[e_sorted] + jnp.arange(S) - starts[e_sorted]
    n_blk = -(-S // MOE_BLK) + N_EXPERTS
    P = n_blk * MOE_BLK
    buf = jnp.zeros((P, D), h.dtype).at[dest].set(h[tok_sorted])
    blk_e = jnp.minimum(jnp.searchsorted(pends, jnp.arange(n_blk) * MOE_BLK, side='right'), N_EXPERTS - 1)

    def expert_block(args):
        xb, e = args
        return (jax.nn.silu(xb @ w_gate[e]) * (xb @ w_up[e])) @ w_down[e]

    out = lax.map(expert_block, (buf.reshape(n_blk, MOE_BLK, D), blk_e)).reshape(P, D)
    contrib = out[dest] * w_sorted[:, None].astype(out.dtype)
    return jnp.zeros((T, D), out.dtype).at[tok_sorted].add(contrib)


def setup_inputs(seed: int = 0) -> dict:
    key = jax.random.key(seed)
    ks = jax.random.split(key, 26)
    f32 = jnp.float32

    def nrm(k, shape, scale):
        return jax.random.normal(k, shape, f32) * scale

    dt = jnp.exp(jax.random.uniform(ks[10], (DEPTH, 2, H_A), f32, math.log(1e-3), math.log(1e-1)))
    return {
        'x': nrm(ks[0], (BATCH, SEQ, D_MODEL), 1.0),
        'c': nrm(ks[1], (BATCH, D_MODEL), 1.0),
        'ctx': nrm(ks[2], (BATCH, CTX_LEN, D_MODEL), 1.0),
        'c_ctx': nrm(ks[3], (D_MODEL,), 1.0),
        'w_mod': nrm(ks[4], (DEPTH, D_MODEL, N_MOD * D_MODEL), 0.5 * D_MODEL ** -0.5),
        'b_mod': nrm(ks[5], (DEPTH, N_MOD * D_MODEL), 0.02),
        'norm_mix': 1.0 + nrm(ks[6], (DEPTH, D_MODEL), 0.02),
        'norm_ffn': 1.0 + nrm(ks[7], (DEPTH, D_MODEL), 0.02),
        'w_in': nrm(ks[8], (DEPTH, D_MODEL, N_IN), D_MODEL ** -0.5),
        'conv_a': nrm(ks[9], (DEPTH, CONV_W, 3 * D_A), CONV_W ** -0.5),
        'a_log': jnp.log(jax.random.uniform(ks[11], (DEPTH, 2, H_A), f32, 1.0, 16.0)),
        'dt_bias': dt + jnp.log(-jnp.expm1(-dt)),
        'gdn_norm': 1.0 + nrm(ks[12], (DEPTH, DK_A), 0.02),
        'sink_b': nrm(ks[13], (DEPTH, H_B), 1.0),
        'rpb_c': nrm(ks[14], (DEPTH, H_C, 2 * KH_MAX - 1, 2 * KW - 1), 0.5),
        'w_out': nrm(ks[15], (DEPTH, D_MIX, D_MODEL), D_MIX ** -0.5),
        'w_router_group': nrm(ks[16], (DEPTH, D_MODEL, N_GROUPS), D_MODEL ** -0.5),
        'b_router_group': nrm(ks[17], (DEPTH, N_GROUPS), 0.01),
        'w_router_expert': nrm(ks[18], (DEPTH, D_MODEL, N_EXPERTS), D_MODEL ** -0.5),
        'b_router_expert': nrm(ks[19], (DEPTH, N_EXPERTS), 0.01),
        'w_gate': nrm(ks[20], (DEPTH, N_EXPERTS, D_MODEL, D_EXPERT), D_MODEL ** -0.5),
        'w_up': nrm(ks[21], (DEPTH, N_EXPERTS, D_MODEL, D_EXPERT), D_MODEL ** -0.5),
        'w_down': nrm(ks[22], (DEPTH, N_EXPERTS, D_EXPERT, D_MODEL), D_EXPERT ** -0.5),
        'norm_final': 1.0 + nrm(ks[23], (D_MODEL,), 0.02),
    }


def reference(x, c, ctx, c_ctx, w_mod, b_mod, norm_mix, norm_ffn, w_in, conv_a, a_log, dt_bias,
              gdn_norm, sink_b, rpb_c, w_out, w_router_group, b_router_group, w_router_expert,
              b_router_expert, w_gate, w_up, w_down, norm_final):
    B, L, D = x.shape
    Lc = ctx.shape[1]
    cos, sin = axial_rope_tables(L, DH)
    x_l, x_c = x, ctx
    for l in range(DEPTH):
        last = l == DEPTH - 1
        ctx_out = not last
        mod_l = (jax.nn.silu(c) @ w_mod[l] + b_mod[l]).reshape(B, N_MOD, D)
        mod_c = (jax.nn.silu(c_ctx) @ w_mod[l] + b_mod[l]).reshape(1, N_MOD, D)
        h_l = modulate(rmsnorm(x_l, norm_mix[l]), mod_l[:, 0:1], mod_l[:, 1:2])
        h_c = modulate(rmsnorm(x_c, norm_mix[l]), mod_c[:, 0:1], mod_c[:, 1:2])
        qkvA_l, zA_l, abA_l, qB_l, kvB_l, qC_l, kvC_l = split_projection(h_l @ w_in[l])
        qkvA_c, zA_c, abA_c, qB_c, kvB_c, qC_c, kvC_c = split_projection(h_c @ w_in[l])
        oA_c, oA_l = gated_deltanet(qkvA_c, abA_c, zA_c, qkvA_l, abA_l, zA_l,
                                    conv_a[l], a_log[l], dt_bias[l], gdn_norm[l], ctx_out)
        oB_c, oB_l = window_gqa(qB_c, kvB_c, qB_l, kvB_l, sink_b[l], cos, sin, ctx_out)
        oC_c, oC_l = neighbourhood_attn(qC_c, kvC_c, qC_l, kvC_l, rpb_c[l], ctx_out)
        x_l = x_l + mod_l[:, 2:3] * (jnp.concatenate([oA_l, oB_l, oC_l], -1) @ w_out[l])
        if ctx_out:
            x_c = x_c + mod_c[:, 2:3] * (jnp.concatenate([oA_c, oB_c, oC_c], -1) @ w_out[l])
        moe_w = (w_router_group[l], b_router_group[l], w_router_expert[l], b_router_expert[l],
                 w_gate[l], w_up[l], w_down[l])
        f_l = modulate(rmsnorm(x_l, norm_ffn[l]), mod_l[:, 3:4], mod_l[:, 4:5]).reshape(B * L, D)
        if ctx_out:
            f_c = modulate(rmsnorm(x_c, norm_ffn[l]), mod_c[:, 3:4], mod_c[:, 4:5]).reshape(B * Lc, D)
            y = hierarchical_moe(jnp.concatenate([f_c, f_l], 0), *moe_w)
            y_c, y_l = y[:B * Lc], y[B * Lc:]
            x_c = x_c + mod_c[:, 5:6] * y_c.reshape(B, Lc, D)
        else:
            y_l = hierarchical_moe(f_l, *moe_w)
        x_l = x_l + mod_l[:, 5:6] * y_l.reshape(B, L, D)
    return rmsnorm(x_l, norm_final)
```

```python
import functools
import math

import jax
import jax.numpy as jnp
from jax import lax
from jax.experimental import pallas as pl
from jax.experimental.pallas import tpu as pltpu

F32 = jnp.float32
BF16 = jnp.bfloat16
HIGHEST = lax.Precision.HIGHEST

EPS = 1e-6
NEG_INF = -1e30
N_MOD = 6
GRID_W = 64
DH = 64
DK_A = 128
CONV_W = 5
CHUNK = 64
KV_B = 2
WIN = 128
QBLK = 128
ROPE_BASE = 10000.0
KH_MAX = 8
KW = 16
NA_QCOLS = 16
NA_KCOLS = 32
N_GROUPS = 4
E_PER_GROUP = 8
N_EXPERTS = N_GROUPS * E_PER_GROUP
TOP_K = 2
LANE = 128
MOD_ROWS = 16
ROUTER_COLS = LANE
MOE_TM = 256


def _params(sem, vmem_mb):
    return pltpu.CompilerParams(dimension_semantics=sem, vmem_limit_bytes=vmem_mb << 20)


def _mod_kernel(c_ref, w_ref, b_ref, o_ref):
    c = c_ref[...]
    s = c * jax.nn.sigmoid(c)
    o_ref[0] = jnp.dot(s, w_ref[0], preferred_element_type=F32, precision=HIGHEST) + b_ref[0]


def _mod_call(cc, w_mod, b_mod):
    depth, d, n = w_mod.shape
    tn = min(n, 1024)
    return pl.pallas_call(
        _mod_kernel,
        grid=(depth, n // tn),
        in_specs=[pl.BlockSpec((MOD_ROWS, d), lambda l, j: (0, 0)),
                  pl.BlockSpec((1, d, tn), lambda l, j: (l, 0, j)),
                  pl.BlockSpec((1, 1, tn), lambda l, j: (l, 0, j))],
        out_specs=pl.BlockSpec((1, MOD_ROWS, tn), lambda l, j: (l, 0, j)),
        out_shape=jax.ShapeDtypeStruct((depth, MOD_ROWS, n), F32),
        compiler_params=_params(("parallel", "parallel"), 48),
        name="mod_proj",
    )(cc, w_mod, b_mod.reshape(depth, 1, n))


def _inproj_kernel(x_ref, shift_ref, scale_ref, g_ref, w_ref, wab_ref, o_ref, ab_ref, h_ref):
    @pl.when(pl.program_id(2) == 0)
    def _():
        x = x_ref[0]
        y = x * lax.rsqrt(jnp.mean(x * x, -1, keepdims=True) + EPS) * g_ref[...]
        h = (y * (1.0 + scale_ref[0]) + shift_ref[0]).astype(BF16)
        h_ref[...] = h
        ab_ref[0] = jnp.dot(h, wab_ref[...], preferred_element_type=F32)

    o_ref[0] = jnp.dot(h_ref[...], w_ref[...], preferred_element_type=F32).astype(o_ref.dtype)


def _inproj_call(x, shift, scale, gain, w_main, w_ab, tm, tn):
    bn, seq, d = x.shape
    n = w_main.shape[1]
    per_batch = shift.shape[0] > 1
    mod_map = (lambda b, i, j: (b, 0, 0)) if per_batch else (lambda b, i, j: (0, 0, 0))
    return pl.pallas_call(
        _inproj_kernel,
        grid=(bn, seq // tm, n // tn),
        in_specs=[pl.BlockSpec((1, tm, d), lambda b, i, j: (b, i, 0)),
                  pl.BlockSpec((1, 1, d), mod_map),
                  pl.BlockSpec((1, 1, d), mod_map),
                  pl.BlockSpec((1, d), lambda b, i, j: (0, 0)),
                  pl.BlockSpec((d, tn), lambda b, i, j: (0, j)),
                  pl.BlockSpec((d, LANE), lambda b, i, j: (0, 0))],
        out_specs=[pl.BlockSpec((1, tm, tn), lambda b, i, j: (b, i, j)),
                   pl.BlockSpec((1, tm, LANE), lambda b, i, j: (b, i, 0))],
        out_shape=[jax.ShapeDtypeStruct((bn, seq, n), BF16),
                   jax.ShapeDtypeStruct((bn, seq, LANE), F32)],
        scratch_shapes=[pltpu.VMEM((tm, d), BF16)],
        compiler_params=_params(("parallel", "parallel", "arbitrary"), 56),
        name="in_proj",
    )(x, shift, scale, gain, w_main, w_ab)


def _outproj_kernel(o_ref, x_ref, gate_ref, shift_ref, scale_ref, g_ref, w_ref, wr_ref, br_ref,
                    xn_ref, f_ref, lg_ref):
    acc = jnp.dot(o_ref[0], w_ref[...], preferred_element_type=F32)
    xn = x_ref[0] + gate_ref[0] * acc
    xn_ref[0] = xn
    y = xn * lax.rsqrt(jnp.mean(xn * xn, -1, keepdims=True) + EPS) * g_ref[...]
    f = y * (1.0 + scale_ref[0]) + shift_ref[0]
    f_ref[0] = f.astype(BF16)
    lg_ref[0] = jnp.dot(f, wr_ref[...], preferred_element_type=F32, precision=HIGHEST) + br_ref[...]


def _outproj_call(o, x, gate, shift, scale, gain, w_out, w_r, b_r, tm):
    bn, seq, d = x.shape
    per_batch = gate.shape[0] > 1
    mod_map = (lambda b, i: (b, 0, 0)) if per_batch else (lambda b, i: (0, 0, 0))
    row = pl.BlockSpec((1, tm, d), lambda b, i: (b, i, 0))
    return pl.pallas_call(
        _outproj_kernel,
        grid=(bn, seq // tm),
        in_specs=[row, row,
                  pl.BlockSpec((1, 1, d), mod_map),
                  pl.BlockSpec((1, 1, d), mod_map),
                  pl.BlockSpec((1, 1, d), mod_map),
                  pl.BlockSpec((1, d), lambda b, i: (0, 0)),
                  pl.BlockSpec((d, d), lambda b, i: (0, 0)),
                  pl.BlockSpec((d, ROUTER_COLS), lambda b, i: (0, 0)),
                  pl.BlockSpec((1, ROUTER_COLS), lambda b, i: (0, 0))],
        out_specs=[row, row, pl.BlockSpec((1, tm, ROUTER_COLS), lambda b, i: (b, i, 0))],
        out_shape=[jax.ShapeDtypeStruct((bn, seq, d), F32),
                   jax.ShapeDtypeStruct((bn, seq, d), BF16),
                   jax.ShapeDtypeStruct((bn, seq, ROUTER_COLS), F32)],
        compiler_params=_params(("parallel", "parallel"), 56),
        name="out_proj",
    )(o, x, gate, shift, scale, gain, w_out, w_r, b_r)


def _moe_kernel(te_ref, nu_ref, x_ref, wg_ref, wu_ref, wd_ref, o_ref):
    @pl.when(pl.program_id(0) < nu_ref[0])
    def _():
        x = x_ref[...]
        g = jnp.dot(x, wg_ref[0], preferred_element_type=F32)
        u = jnp.dot(x, wu_ref[0], preferred_element_type=F32)
        a = (g * jax.nn.sigmoid(g) * u).astype(BF16)
        o_ref[...] = jnp.dot(a, wd_ref[0], preferred_element_type=F32).astype(o_ref.dtype)


def _moe_call(tile_expert, n_used, buf, w_gate, w_up, w_down, tm):
    p, d = buf.shape
    f = w_gate.shape[-1]
    grid_spec = pltpu.PrefetchScalarGridSpec(
        num_scalar_prefetch=2,
        grid=(p // tm,),
        in_specs=[pl.BlockSpec((tm, d), lambda i, te, nu: (i, 0)),
                  pl.BlockSpec((1, d, f), lambda i, te, nu: (te[i], 0, 0)),
                  pl.BlockSpec((1, d, f), lambda i, te, nu: (te[i], 0, 0)),
                  pl.BlockSpec((1, f, d), lambda i, te, nu: (te[i], 0, 0))],
        out_specs=pl.BlockSpec((tm, d), lambda i, te, nu: (i, 0)),
    )
    return pl.pallas_call(
        _moe_kernel,
        grid_spec=grid_spec,
        out_shape=jax.ShapeDtypeStruct((p, d), BF16),
        compiler_params=_params(("arbitrary",), 60),
        name="moe_experts",
    )(tile_expert, n_used, buf, w_gate, w_up, w_down)


def _resid_kernel(x_ref, y_ref, gate_ref, g_ref, o_ref, *, final_norm):
    xn = x_ref[0] + gate_ref[0] * y_ref[0]
    if final_norm:
        xn = xn * lax.rsqrt(jnp.mean(xn * xn, -1, keepdims=True) + EPS) * g_ref[...]
    o_ref[0] = xn


def _resid_call(x, y, gate, gain, tm, final_norm):
    bn, seq, d = x.shape
    per_batch = gate.shape[0] > 1
    mod_map = (lambda b, i: (b, 0, 0)) if per_batch else (lambda b, i: (0, 0, 0))
    row = pl.BlockSpec((1, tm, d), lambda b, i: (b, i, 0))
    return pl.pallas_call(
        functools.partial(_resid_kernel, final_norm=final_norm),
        grid=(bn, seq // tm),
        in_specs=[row, row, pl.BlockSpec((1, 1, d), mod_map), pl.BlockSpec((1, d), lambda b, i: (0, 0))],
        out_specs=row,
        out_shape=jax.ShapeDtypeStruct((bn, seq, d), F32),
        compiler_params=_params(("parallel", "parallel"), 48),
        name="moe_residual",
    )(x, y, gate, gain)


def _l2norm(u):
    return u * lax.rsqrt(jnp.sum(u * u, -1, keepdims=True) + EPS)


def _short_conv(u, w):
    pad = CONV_W // 2
    L = u.shape[1]
    up = jnp.pad(u, ((0, 0), (pad, pad), (0, 0)))
    return sum(up[:, i:i + L] * w[i] for i in range(CONV_W))


def _gdn_prepare(qkv, ab, conv_w, a_log, dt_bias, h_a):
    B, L, _ = qkv.shape
    qkv = jax.nn.silu(_short_conv(qkv, conv_w))
    q, k, v = jnp.split(qkv, 3, -1)
    q = _l2norm(q.reshape(B, L, h_a, DK_A))
    k = _l2norm(k.reshape(B, L, h_a, DK_A))
    v = v.reshape(B, L, h_a, DK_A)
    ab = ab.reshape(B, L, 2, 2, h_a)
    g = -jnp.exp(a_log) * jax.nn.softplus(ab[:, :, :, 0] + dt_bias)
    beta = jax.nn.sigmoid(ab[:, :, :, 1])
    return q, k, v, g, beta


def _gdn_chunked(q, k, v, g, beta, s0):
    B, L, H, dk = k.shape
    dv = v.shape[-1]
    n = L // CHUNK

    def chunks(u):
        return u.reshape(B, n, CHUNK, H, -1).transpose(0, 3, 1, 2, 4)

    qc, kc, vc = chunks(q * dk ** -0.5), chunks(k), chunks(v)
    gc = g.reshape(B, n, CHUNK, H).transpose(0, 3, 1, 2)
    bc = beta.reshape(B, n, CHUNK, H).transpose(0, 3, 1, 2)
    gcum = jnp.cumsum(gc, -1)
    tril = jnp.tril(jnp.ones((CHUNK, CHUNK), bool))
    strict = jnp.tril(jnp.ones((CHUNK, CHUNK), bool), -1)
    diff = gcum[..., :, None] - gcum[..., None, :]
    decay = jnp.where(tril, jnp.exp(jnp.where(tril, diff, 0.0)), 0.0)
    kb = kc * bc[..., None]
    vb = vc * bc[..., None]
    a = jnp.where(strict, jnp.einsum('bhnid,bhnjd->bhnij', kb, kc) * decay, 0.0)
    eye = jnp.eye(CHUNK, dtype=F32)
    tinv = lax.linalg.triangular_solve(eye + a, jnp.broadcast_to(eye, a.shape),
                                       left_side=True, lower=True, unit_diagonal=True)
    u = tinv @ vb
    w = tinv @ (kb * jnp.exp(gcum)[..., None])
    attn = jnp.einsum('bhnid,bhnjd->bhnij', qc, kc) * decay
    qg = qc * jnp.exp(gcum)[..., None]
    kdec = kc * jnp.exp(gcum[..., -1:] - gcum)[..., None]
    glast = jnp.exp(gcum[..., -1])
    xs = tuple(jnp.moveaxis(t, 2, 0) for t in (u, w, attn, qg, kdec, glast))

    def step(s, inp):
        u_i, w_i, attn_i, qg_i, kdec_i, gl_i = inp
        v_new = u_i - w_i @ s
        o_i = qg_i @ s + attn_i @ v_new
        s = s * gl_i[..., None, None] + jnp.einsum('bhcd,bhce->bhde', kdec_i, v_new)
        return s, o_i

    s_final, o = lax.scan(step, s0, xs)
    o = jnp.moveaxis(o, 0, 2).transpose(0, 2, 3, 1, 4).reshape(B, L, H, dv)
    return o, s_final


def _gdn_output(o, z, norm_w, h_a):
    B, L = z.shape[:2]
    o = o * lax.rsqrt(jnp.mean(o * o, -1, keepdims=True) + EPS) * norm_w
    zg = jax.nn.silu(z.reshape(B, L, h_a, DK_A))
    return (o * zg).reshape(B, L, h_a * DK_A)


def _gated_deltanet(qkv_c, ab_c, z_c, qkv_l, ab_l, z_l, conv_w, a_log, dt_bias, norm_w, ctx_out, h_a):
    qc, kc, vc, gc, bc = _gdn_prepare(qkv_c, ab_c, conv_w, a_log, dt_bias, h_a)
    ql, kl, vl, gl, bl = _gdn_prepare(qkv_l, ab_l, conv_w, a_log, dt_bias, h_a)
    B = ql.shape[0]
    s0 = jnp.zeros((B, h_a, DK_A, DK_A), F32)
    o_c_sum, o_l_sum = 0.0, 0.0
    for d in range(2):
        seq_c = (qc, kc, vc, gc[:, :, d], bc[:, :, d])
        seq_l = (ql, kl, vl, gl[:, :, d], bl[:, :, d])
        if d == 1:
            seq_c = tuple(jnp.flip(t, 1) for t in seq_c)
            seq_l = tuple(jnp.flip(t, 1) for t in seq_l)
        o_c, s_c = _gdn_chunked(*seq_c, s0)
        o_l, _ = _gdn_chunked(*seq_l, s_c)
        if d == 1:
            o_c, o_l = jnp.flip(o_c, 1), jnp.flip(o_l, 1)
        o_c_sum = o_c_sum + o_c
        o_l_sum = o_l_sum + o_l
    o_lat = _gdn_output(o_l_sum, z_l, norm_w, h_a)
    o_ctx = _gdn_output(o_c_sum, z_c, norm_w, h_a) if ctx_out else None
    return o_ctx, o_lat


def _rope_tables(n_tokens, head_dim):
    t = jnp.arange(n_tokens)
    row = (t // GRID_W).astype(F32)
    col = (t % GRID_W).astype(F32)
    half = head_dim // 2
    inv = ROPE_BASE ** (-jnp.arange(0, half, 2, dtype=F32) / half)
    ang_r = row[:, None] * inv[None]
    ang_c = col[:, None] * inv[None]
    ang = jnp.concatenate([ang_r, ang_r, ang_c, ang_c], -1)
    return jnp.cos(ang), jnp.sin(ang)


def _apply_rope(x, cos, sin):
    half = x.shape[-1] // 2

    def rot(u):
        u1, u2 = jnp.split(u, 2, -1)
        return jnp.concatenate([-u2, u1], -1)

    xr = jnp.concatenate([rot(x[..., :half]), rot(x[..., half:])], -1)
    return x * cos[None, :, None, :] + xr * sin[None, :, None, :]


def _softmax_with_sink(s, sink):
    sk = jnp.broadcast_to(sink, s.shape[:-1] + (1,))
    p = jax.nn.softmax(jnp.concatenate([s, sk], -1), -1)
    return p[..., :-1]


def _window_gqa(q_c, kv_c, q_l, kv_l, sink, cos, sin, ctx_out):
    B, L = q_l.shape[:2]
    Lc = q_c.shape[1]
    h_b = q_l.shape[-1] // DH
    G = h_b // KV_B
    n = L // QBLK
    J = 3 * QBLK
    scale = DH ** -0.5
    k_l, v_l = [u.reshape(B, L, KV_B, DH) for u in jnp.split(kv_l, 2, -1)]
    k_c, v_c = [u.reshape(B, Lc, KV_B, DH) for u in jnp.split(kv_c, 2, -1)]
    q_l = _apply_rope(q_l.reshape(B, L, h_b, DH), cos, sin)
    k_l = _apply_rope(k_l, cos, sin)
    sink = sink.reshape(KV_B, G, 1, 1)

    def band(u):
        up = jnp.pad(u, ((0, 0), (QBLK, QBLK), (0, 0), (0, 0))).reshape(B, n + 2, QBLK, KV_B, DH)
        return jnp.concatenate([up[:, :-2], up[:, 1:-1], up[:, 2:]], axis=2)

    k_w, v_w = band(k_l), band(v_l)
    qb = q_l.reshape(B, n, QBLK, KV_B, G, DH)
    qi = jnp.arange(QBLK)[:, None]
    kj = jnp.arange(J)[None]
    kpos = (jnp.arange(n) * QBLK)[:, None, None] - QBLK + kj[None]
    in_win = (jnp.abs(kj - QBLK - qi)[None] <= WIN) & (kpos >= 0) & (kpos < L)
    s_w = jnp.einsum('bnqkgd,bnjkd->bnkgqj', qb, k_w) * scale
    s_w = jnp.where(in_win[None, :, None, None], s_w, NEG_INF)
    s_x = jnp.einsum('bnqkgd,bckd->bnkgqc', qb, k_c) * scale
    p = _softmax_with_sink(jnp.concatenate([s_w, s_x], -1), sink)
    o = (jnp.einsum('bnkgqj,bnjkd->bnqkgd', p[..., :J], v_w)
         + jnp.einsum('bnkgqc,bckd->bnqkgd', p[..., J:], v_c))
    o_lat = o.reshape(B, L, h_b * DH)
    o_ctx = None
    if ctx_out:
        qc = q_c.reshape(B, Lc, KV_B, G, DH)
        s_c = jnp.einsum('bqkgd,bckd->bkgqc', qc, k_c) * scale
        p_c = _softmax_with_sink(s_c, sink)
        o_ctx = jnp.einsum('bkgqc,bckd->bqkgd', p_c, v_c).reshape(B, Lc, h_b * DH)
    return o_ctx, o_lat


def _neighbourhood_attn(q_c, kv_c, q_l, kv_l, rpb, ctx_out):
    B, L = q_l.shape[:2]
    Lc = q_c.shape[1]
    h_c = q_l.shape[-1] // DH
    rows = L // GRID_W
    kh = min(KH_MAX, rows)
    ncb = GRID_W // NA_QCOLS
    m = kh * NA_KCOLS
    scale = DH ** -0.5
    k_l, v_l = [u.reshape(B, L, h_c, DH) for u in jnp.split(kv_l, 2, -1)]
    k_c, v_c = [u.reshape(B, Lc, h_c, DH) for u in jnp.split(kv_c, 2, -1)]
    r = jnp.arange(rows)
    row_idx = jnp.clip(r - kh // 2, 0, rows - kh)[:, None] + jnp.arange(kh)[None]
    cb = jnp.arange(ncb)
    col_idx = (jnp.clip(cb * NA_QCOLS - KW // 2, 0, GRID_W - NA_KCOLS)[:, None]
               + jnp.arange(NA_KCOLS)[None])

    def gather_band(u):
        u = u.reshape(B, rows, GRID_W, h_c, DH)[:, row_idx]
        u = u[:, :, :, col_idx]
        return u.transpose(0, 1, 3, 2, 4, 5, 6).reshape(B, rows, ncb, m, h_c, DH)

    k_n, v_n = gather_band(k_l), gather_band(v_l)
    qn = q_l.reshape(B, rows, ncb, NA_QCOLS, h_c, DH)
    q_col = cb[:, None] * NA_QCOLS + jnp.arange(NA_QCOLS)[None]
    q_cs = jnp.clip(q_col - KW // 2, 0, GRID_W - KW)
    key_col = jnp.tile(col_idx, (1, kh))
    key_row = jnp.repeat(row_idx, NA_KCOLS, axis=1)
    in_win = ((key_col[:, None, :] >= q_cs[:, :, None])
              & (key_col[:, None, :] < q_cs[:, :, None] + KW))
    dr = key_row - r[:, None] + (KH_MAX - 1)
    dc = jnp.clip(key_col[:, None, :] - q_col[:, :, None] + (KW - 1), 0, 2 * KW - 2)
    bias = rpb[:, dr[:, None, None, :], dc[None]]
    s_nb = jnp.einsum('brcqhd,brcmhd->bhrcqm', qn, k_n) * scale + bias
    s_nb = jnp.where(in_win, s_nb, NEG_INF)
    s_x = jnp.einsum('brcqhd,bkhd->bhrcqk', qn, k_c) * scale
    p = jax.nn.softmax(jnp.concatenate([s_nb, s_x], -1), -1)
    o = (jnp.einsum('bhrcqm,brcmhd->brcqhd', p[..., :m], v_n)
         + jnp.einsum('bhrcqk,bkhd->brcqhd', p[..., m:], v_c))
    o_lat = o.reshape(B, L, h_c * DH)
    o_ctx = None
    if ctx_out:
        qc = q_c.reshape(B, Lc, h_c, DH)
        s_c = jnp.einsum('bqhd,bkhd->bhqk', qc, k_c) * scale
        p_c = jax.nn.softmax(s_c, -1)
        o_ctx = jnp.einsum('bhqk,bkhd->bqhd', p_c, v_c).reshape(B, Lc, h_c * DH)
    return o_ctx, o_lat


def _route(logits):
    T = logits.shape[0]
    g_logits = logits[:, :N_GROUPS]
    g_prob = jax.nn.softmax(g_logits, -1)
    g_sel = jnp.argmax(g_logits, -1)
    e_logits = logits[:, N_GROUPS:N_GROUPS + N_EXPERTS].reshape(T, N_GROUPS, E_PER_GROUP)
    e_in = jnp.take_along_axis(e_logits, g_sel[:, None, None], 1)[:, 0]
    top_v, top_i = lax.top_k(e_in, TOP_K)
    wts = jax.nn.softmax(top_v, -1) * jnp.take_along_axis(g_prob, g_sel[:, None], 1)
    expert = g_sel[:, None] * E_PER_GROUP + top_i
    return expert.astype(jnp.int32), wts


def _dispatch(expert, tm):
    T = expert.shape[0]
    S = T * TOP_K
    e_flat = expert.reshape(S)
    order = jnp.argsort(e_flat)
    e_sorted = e_flat[order]
    tok_sorted = order // TOP_K
    counts = jnp.bincount(e_flat, length=N_EXPERTS)
    starts = jnp.cumsum(counts) - counts
    padded = (counts + tm - 1) // tm * tm
    pends = jnp.cumsum(padded)
    pstarts = pends - padded
    dest = pstarts[e_sorted] + jnp.arange(S) - starts[e_sorted]
    n_tiles = -(-S // tm) + N_EXPERTS
    tile_expert = jnp.minimum(jnp.searchsorted(pends, jnp.arange(n_tiles) * tm, side='right'),
                              N_EXPERTS - 1).astype(jnp.int32)
    n_used = (pends[-1] // tm).astype(jnp.int32).reshape(1)
    pos = jnp.zeros((S,), jnp.int32).at[order].set(dest.astype(jnp.int32)).reshape(T, TOP_K)
    src_tok = jnp.zeros((n_tiles * tm,), jnp.int32).at[dest].set(tok_sorted.astype(jnp.int32))
    return tile_expert, n_used, pos, src_tok


def _moe(f, logits, w_gate, w_up, w_down):
    T, d = f.shape
    expert, wts = _route(logits)
    tile_expert, n_used, pos, src_tok = _dispatch(expert, MOE_TM)
    buf = f[src_tok]
    out = _moe_call(tile_expert, n_used, buf, w_gate, w_up, w_down, MOE_TM)
    y = out[pos[:, 0]].astype(F32) * wts[:, 0:1] + out[pos[:, 1]].astype(F32) * wts[:, 1:2]
    return y


def kernel(x, c, ctx, c_ctx, w_mod, b_mod, norm_mix, norm_ffn, w_in, conv_a, a_log, dt_bias, gdn_norm, sink_b, rpb_c, w_out, w_router_group, b_router_group, w_router_expert, b_router_expert, w_gate, w_up, w_down, norm_final):
    B, L, D = x.shape
    Lc = ctx.shape[1]
    depth = w_mod.shape[0]
    d_a = D // 2
    d_b = D // 4
    d_c = D - d_a - d_b
    h_a = d_a // DK_A
    n_ab = 4 * h_a
    kvb = 2 * KV_B * DH
    o_ab = 4 * d_a
    n_main = 4 * d_a + d_b + kvb + 3 * d_c

    cc = jnp.zeros((MOD_ROWS, D), F32).at[:B].set(c).at[B].set(c_ctx)
    mod = _mod_call(cc, w_mod, b_mod).reshape(depth, MOD_ROWS, N_MOD, D)
    cos, sin = _rope_tables(L, DH)

    tn_in = n_main
    for cand in (1280, 1024, 768, 640, 512, 384, 256, 128):
        if n_main % cand == 0:
            tn_in = cand
            break

    x_l, x_c = x, ctx
    for l in range(depth):
        last = l == depth - 1
        ctx_out = not last
        mod_l = mod[l, :B]
        mod_c = mod[l, B:B + 1]
        ml = [mod_l[:, i:i + 1] for i in range(N_MOD)]
        mc = [mod_c[:, i:i + 1] for i in range(N_MOD)]
        w_main = jnp.concatenate([w_in[l][:, :o_ab], w_in[l][:, o_ab + n_ab:]], -1).astype(BF16)
        w_ab = jnp.pad(w_in[l][:, o_ab:o_ab + n_ab], ((0, 0), (0, LANE - n_ab))).astype(BF16)
        gain_mix = norm_mix[l].reshape(1, D)
        p_l, ab_l = _inproj_call(x_l, ml[0], ml[1], gain_mix, w_main, w_ab, min(L, 1024), tn_in)
        p_c, ab_c = _inproj_call(x_c, mc[0], mc[1], gain_mix, w_main, w_ab, min(Lc, 1024), tn_in)

        def split(p):
            p = p.astype(F32)
            sizes = (3 * d_a, d_a, d_b, kvb, d_c, 2 * d_c)
            bounds = [sum(sizes[:i + 1]) for i in range(len(sizes) - 1)]
            return jnp.split(p, bounds, axis=-1)

        qkvA_l, zA_l, qB_l, kvB_l, qC_l, kvC_l = split(p_l)
        qkvA_c, zA_c, qB_c, kvB_c, qC_c, kvC_c = split(p_c)
        oA_c, oA_l = _gated_deltanet(qkvA_c, ab_c[..., :n_ab], zA_c, qkvA_l, ab_l[..., :n_ab], zA_l,
                                     conv_a[l], a_log[l], dt_bias[l], gdn_norm[l], ctx_out, h_a)
        oB_c, oB_l = _window_gqa(qB_c, kvB_c, qB_l, kvB_l, sink_b[l], cos, sin, ctx_out)
        oC_c, oC_l = _neighbourhood_attn(qC_c, kvC_c, qC_l, kvC_l, rpb_c[l], ctx_out)
        o_l = jnp.concatenate([oA_l, oB_l, oC_l], -1).astype(BF16)

        w_o = w_out[l].astype(BF16)
        w_r = jnp.pad(jnp.concatenate([w_router_group[l], w_router_expert[l]], -1),
                      ((0, 0), (0, ROUTER_COLS - N_GROUPS - N_EXPERTS)))
        b_r = jnp.pad(jnp.concatenate([b_router_group[l], b_router_expert[l]], -1),
                      (0, ROUTER_COLS - N_GROUPS - N_EXPERTS)).reshape(1, ROUTER_COLS)
        gain_ffn = norm_ffn[l].reshape(1, D)
        x_l, f_l, lg_l = _outproj_call(o_l, x_l, ml[2], ml[3], ml[4], gain_ffn, w_o, w_r, b_r, min(L, 256))
        wg, wu, wd = w_gate[l].astype(BF16), w_up[l].astype(BF16), w_down[l].astype(BF16)
        if ctx_out:
            o_c = jnp.concatenate([oA_c, oB_c, oC_c], -1).astype(BF16)
            x_c, f_c, lg_c = _outproj_call(o_c, x_c, mc[2], mc[3], mc[4], gain_ffn, w_o, w_r, b_r, min(Lc, 256))
            f_all = jnp.concatenate([f_c.reshape(B * Lc, D), f_l.reshape(B * L, D)], 0)
            lg_all = jnp.concatenate([lg_c.reshape(B * Lc, ROUTER_COLS), lg_l.reshape(B * L, ROUTER_COLS)], 0)
            y = _moe(f_all, lg_all, wg, wu, wd)
            y_c, y_l = y[:B * Lc], y[B * Lc:]
            x_c = _resid_call(x_c, y_c.reshape(B, Lc, D), mc[5], gain_ffn, min(Lc, 256), False)
        else:
            y_l = _moe(f_l.reshape(B * L, D), lg_l.reshape(B * L, ROUTER_COLS), wg, wu, wd)
        x_l = _resid_call(x_l, y_l.reshape(B, L, D), ml[5], norm_final.reshape(1, D), min(L, 256), last)
    return x_l
```

```python
import functools
import math

import jax
import jax.numpy as jnp
from jax import lax
from jax.experimental import pallas as pl
from jax.experimental.pallas import tpu as pltpu

F32 = jnp.float32
BF16 = jnp.bfloat16
HIGHEST = lax.Precision.HIGHEST

EPS = 1e-6
NEG_INF = -1e30
N_MOD = 6
GRID_W = 64
DH = 64
DK_A = 128
CONV_W = 5
CHUNK = 64
KV_B = 2
WIN = 128
QBLK = 128
ROPE_BASE = 10000.0
KH_MAX = 8
KW = 16
NA_QCOLS = 16
NA_KCOLS = 32
N_GROUPS = 4
E_PER_GROUP = 8
N_EXPERTS = N_GROUPS * E_PER_GROUP
TOP_K = 2
LANE = 128
MOD_ROWS = 16
ROUTER_COLS = LANE
MOE_TM = 256


def _params(sem, vmem_mb):
    return pltpu.CompilerParams(dimension_semantics=sem, vmem_limit_bytes=vmem_mb << 20)


def _mod_kernel(c_ref, w_ref, b_ref, o_ref):
    c = c_ref[...]
    s = c * jax.nn.sigmoid(c)
    o_ref[0] = jnp.dot(s, w_ref[0], preferred_element_type=F32, precision=HIGHEST) + b_ref[0]


def _mod_call(cc, w_mod, b_mod):
    depth, d, n = w_mod.shape
    tn = min(n, 1024)
    return pl.pallas_call(
        _mod_kernel,
        grid=(depth, n // tn),
        in_specs=[pl.BlockSpec((MOD_ROWS, d), lambda l, j: (0, 0)),
                  pl.BlockSpec((1, d, tn), lambda l, j: (l, 0, j)),
                  pl.BlockSpec((1, 1, tn), lambda l, j: (l, 0, j))],
        out_specs=pl.BlockSpec((1, MOD_ROWS, tn), lambda l, j: (l, 0, j)),
        out_shape=jax.ShapeDtypeStruct((depth, MOD_ROWS, n), F32),
        compiler_params=_params(("parallel", "parallel"), 48),
        name="mod_proj",
    )(cc, w_mod, b_mod.reshape(depth, 1, n))


def _inproj_kernel(x_ref, shift_ref, scale_ref, g_ref, w_ref, wab_ref, o_ref, ab_ref, h_ref):
    @pl.when(pl.program_id(2) == 0)
    def _():
        x = x_ref[0]
        y = x * lax.rsqrt(jnp.mean(x * x, -1, keepdims=True) + EPS) * g_ref[...]
        h = (y * (1.0 + scale_ref[0]) + shift_ref[0]).astype(BF16)
        h_ref[...] = h
        ab_ref[0] = jnp.dot(h, wab_ref[...], preferred_element_type=F32)

    o_ref[0] = jnp.dot(h_ref[...], w_ref[...], preferred_element_type=F32).astype(o_ref.dtype)


def _inproj_call(x, shift, scale, gain, w_main, w_ab, tm, tn):
    bn, seq, d = x.shape
    n = w_main.shape[1]
    per_batch = shift.shape[0] > 1
    mod_map = (lambda b, i, j: (b, 0, 0)) if per_batch else (lambda b, i, j: (0, 0, 0))
    return pl.pallas_call(
        _inproj_kernel,
        grid=(bn, seq // tm, n // tn),
        in_specs=[pl.BlockSpec((1, tm, d), lambda b, i, j: (b, i, 0)),
                  pl.BlockSpec((1, 1, d), mod_map),
                  pl.BlockSpec((1, 1, d), mod_map),
                  pl.BlockSpec((1, d), lambda b, i, j: (0, 0)),
                  pl.BlockSpec((d, tn), lambda b, i, j: (0, j)),
                  pl.BlockSpec((d, LANE), lambda b, i, j: (0, 0))],
        out_specs=[pl.BlockSpec((1, tm, tn), lambda b, i, j: (b, i, j)),
                   pl.BlockSpec((1, tm, LANE), lambda b, i, j: (b, i, 0))],
        out_shape=[jax.ShapeDtypeStruct((bn, seq, n), BF16),
                   jax.ShapeDtypeStruct((bn, seq, LANE), F32)],
        scratch_shapes=[pltpu.VMEM((tm, d), BF16)],
        compiler_params=_params(("parallel", "parallel", "arbitrary"), 56),
        name="in_proj",
    )(x, shift, scale, gain, w_main, w_ab)


def _outproj_kernel(o_ref, x_ref, gate_ref, shift_ref, scale_ref, g_ref, w_ref, wr_ref, br_ref,
                    xn_ref, f_ref, lg_ref):
    acc = jnp.dot(o_ref[0], w_ref[...], preferred_element_type=F32)
    xn = x_ref[0] + gate_ref[0] * acc
    xn_ref[0] = xn
    y = xn * lax.rsqrt(jnp.mean(xn * xn, -1, keepdims=True) + EPS) * g_ref[...]
    f = y * (1.0 + scale_ref[0]) + shift_ref[0]
    f_ref[0] = f.astype(BF16)
    lg_ref[0] = jnp.dot(f, wr_ref[...], preferred_element_type=F32, precision=HIGHEST) + br_ref[...]


def _outproj_call(o, x, gate, shift, scale, gain, w_out, w_r, b_r, tm):
    bn, seq, d = x.shape
    per_batch = gate.shape[0] > 1
    mod_map = (lambda b, i: (b, 0, 0)) if per_batch else (lambda b, i: (0, 0, 0))
    row = pl.BlockSpec((1, tm, d), lambda b, i: (b, i, 0))
    return pl.pallas_call(
        _outproj_kernel,
        grid=(bn, seq // tm),
        in_specs=[row, row,
                  pl.BlockSpec((1, 1, d), mod_map),
                  pl.BlockSpec((1, 1, d), mod_map),
                  pl.BlockSpec((1, 1, d), mod_map),
                  pl.BlockSpec((1, d), lambda b, i: (0, 0)),
                  pl.BlockSpec((d, d), lambda b, i: (0, 0)),
                  pl.BlockSpec((d, ROUTER_COLS), lambda b, i: (0, 0)),
                  pl.BlockSpec((1, ROUTER_COLS), lambda b, i: (0, 0))],
        out_specs=[row, row, pl.BlockSpec((1, tm, ROUTER_COLS), lambda b, i: (b, i, 0))],
        out_shape=[jax.ShapeDtypeStruct((bn, seq, d), F32),
                   jax.ShapeDtypeStruct((bn, seq, d), BF16),
                   jax.ShapeDtypeStruct((bn, seq, ROUTER_COLS), F32)],
        compiler_params=_params(("parallel", "parallel"), 56),
        name="out_proj",
    )(o, x, gate, shift, scale, gain, w_out, w_r, b_r)


def _moe_kernel(te_ref, nu_ref, x_ref, wg_ref, wu_ref, wd_ref, o_ref):
    @pl.when(pl.program_id(0) < nu_ref[0])
    def _():
        x = x_ref[...]
        g = jnp.dot(x, wg_ref[0], preferred_element_type=F32)
        u = jnp.dot(x, wu_ref[0], preferred_element_type=F32)
        a = (g * jax.nn.sigmoid(g) * u).astype(BF16)
        o_ref[...] = jnp.dot(a, wd_ref[0], preferred_element_type=F32).astype(o_ref.dtype)


def _moe_call(tile_expert, n_used, buf, w_gate, w_up, w_down, tm):
    p, d = buf.shape
    f = w_gate.shape[-1]
    grid_spec = pltpu.PrefetchScalarGridSpec(
        num_scalar_prefetch=2,
        grid=(p // tm,),
        in_specs=[pl.BlockSpec((tm, d), lambda i, te, nu: (i, 0)),
                  pl.BlockSpec((1, d, f), lambda i, te, nu: (te[i], 0, 0)),
                  pl.BlockSpec((1, d, f), lambda i, te, nu: (te[i], 0, 0)),
                  pl.BlockSpec((1, f, d), lambda i, te, nu: (te[i], 0, 0))],
        out_specs=pl.BlockSpec((tm, d), lambda i, te, nu: (i, 0)),
    )
    return pl.pallas_call(
        _moe_kernel,
        grid_spec=grid_spec,
        out_shape=jax.ShapeDtypeStruct((p, d), BF16),
        compiler_params=_params(("arbitrary",), 60),
        name="moe_experts",
    )(tile_expert, n_used, buf, w_gate, w_up, w_down)


def _resid_kernel(x_ref, y_ref, gate_ref, g_ref, o_ref, *, final_norm):
    xn = x_ref[0] + gate_ref[0] * y_ref[0]
    if final_norm:
        xn = xn * lax.rsqrt(jnp.mean(xn * xn, -1, keepdims=True) + EPS) * g_ref[...]
    o_ref[0] = xn


def _resid_call(x, y, gate, gain, tm, final_norm):
    bn, seq, d = x.shape
    per_batch = gate.shape[0] > 1
    mod_map = (lambda b, i: (b, 0, 0)) if per_batch else (lambda b, i: (0, 0, 0))
    row = pl.BlockSpec((1, tm, d), lambda b, i: (b, i, 0))
    return pl.pallas_call(
        functools.partial(_resid_kernel, final_norm=final_norm),
        grid=(bn, seq // tm),
        in_specs=[row, row, pl.BlockSpec((1, 1, d), mod_map), pl.BlockSpec((1, d), lambda b, i: (0, 0))],
        out_specs=row,
        out_shape=jax.ShapeDtypeStruct((bn, seq, d), F32),
        compiler_params=_params(("parallel", "parallel"), 48),
        name="moe_residual",
    )(x, y, gate, gain)


def _softplus(x):
    return jnp.maximum(x, 0.0) + jnp.log1p(jnp.exp(-jnp.abs(x)))


def _gdn_prep(x_ref, w_ref, dst_ref, off, mode):
    x = x_ref[0].astype(F32)
    n = x.shape[0]
    rows = lax.broadcasted_iota(jnp.int32, (n, 1), 0)
    pad = CONV_W // 2
    acc = x * w_ref[pad:pad + 1, :]
    for o in range(-pad, pad + 1):
        if o == 0:
            continue
        xs = pltpu.roll(x, (-o) % n, 0)
        valid = (rows + o >= 0) if o < 0 else (rows + o < n)
        acc = acc + jnp.where(valid, xs, 0.0) * w_ref[o + pad:o + pad + 1, :]
    y = acc * jax.nn.sigmoid(acc)
    if mode != "v":
        y = y * lax.rsqrt(jnp.sum(y * y, -1, keepdims=True) + EPS)
    if mode == "q":
        y = y * DK_A ** -0.5
    dst_ref[off:off + n, :] = y


def _gdn_kernel(alog_ref, dtb_ref,
                qc_ref, kc_ref, vc_ref, zc_ref, ql_ref, kl_ref, vl_ref, zl_ref,
                wq_ref, wk_ref, wv_ref, grow_ref, bcol_ref, nw_ref,
                oc_ref, ol_ref,
                q_s, k_s, v_s, o_s, u_s, l1_s, l2_s, gl_s, gc_s, *, h_a, n_ctx):
    C = CHUNK
    lc = qc_ref.shape[1]
    n_tot = gc_s.shape[1]
    h = pl.program_id(1)

    _gdn_prep(qc_ref, wq_ref, q_s, 0, "q")
    _gdn_prep(kc_ref, wk_ref, k_s, 0, "k")
    _gdn_prep(vc_ref, wv_ref, v_s, 0, "v")
    _gdn_prep(ql_ref, wq_ref, q_s, lc, "q")
    _gdn_prep(kl_ref, wk_ref, k_s, lc, "k")
    _gdn_prep(vl_ref, wv_ref, v_s, lc, "v")

    ii = lax.broadcasted_iota(jnp.int32, (C, C), 0)
    jj = lax.broadcasted_iota(jnp.int32, (C, C), 1)
    eye_f = (ii == jj).astype(F32)
    for d in range(2):
        a = jnp.exp(jnp.full((1, C), alog_ref[d * h_a + h], F32))
        g = -a * _softplus(grow_ref[0, 0, d] + dtb_ref[d * h_a + h])
        tri = (ii <= jj) if d == 0 else (ii >= jj)
        gc_s[d] = jnp.dot(g, tri.astype(F32), preferred_element_type=F32, precision=HIGHEST)

    def chunk_terms(c, _):
        r0 = pl.multiple_of(c * C, C)
        q = q_s[pl.ds(r0, C), :]
        k = k_s[pl.ds(r0, C), :]
        v = v_s[pl.ds(r0, C), :]
        k16 = k.astype(BF16)
        nt = (((1,), (1,)), ((), ()))
        kk = lax.dot_general(k16, k16, nt, preferred_element_type=F32)
        qk = lax.dot_general(q.astype(BF16), k16, nt, preferred_element_type=F32)
        for d in range(2):
            gcr = gc_s[d, pl.ds(c, 1), :]
            rowb = jnp.broadcast_to(gcr, (C, C))
            gcc = jnp.sum(jnp.where(ii == jj, rowb, 0.0), axis=1, keepdims=True)
            tri = (ii >= jj) if d == 0 else (ii <= jj)
            dec = jnp.where(tri, jnp.exp(jnp.where(tri, gcc - rowb, 0.0)), 0.0)
            beta = jax.nn.sigmoid(bcol_ref[0, 0, pl.ds(r0, C), d:d + 1])
            m0 = jnp.where(ii == jj, 0.0, -(kk * beta * dec))
            m16 = m0.astype(BF16)
            r = jnp.dot(m16, m16, preferred_element_type=F32)
            qm = eye_f + m0
            for _ in range(4):
                qr = jnp.dot(jnp.concatenate([qm, r], 0).astype(BF16), r.astype(BF16),
                             preferred_element_type=F32)
                qm = qm + qr[:C]
                r = qr[C:]
            tinv = qm + jnp.dot(qm.astype(BF16), r.astype(BF16), preferred_element_type=F32)
            eg = jnp.exp(gcc)
            rhs = jnp.concatenate([v * beta, k * (beta * eg)], axis=1).astype(BF16)
            uw = jnp.dot(tinv.astype(BF16), rhs, preferred_element_type=F32)
            u_s[d, c] = uw[:, :DK_A]
            l1_s[d, c] = jnp.concatenate([uw[:, DK_A:], q * eg], 0).astype(BF16)
            tot = gcr[:, C - 1:C] if d == 0 else gcr[:, 0:1]
            kdec = k * jnp.exp(tot - gcc)
            l2_s[d, c] = jnp.concatenate([qk * dec, kdec.T], 0).astype(BF16)
            gl_s[d, c] = jnp.broadcast_to(jnp.exp(tot), (1, DK_A))
        return 0

    lax.fori_loop(0, n_tot, chunk_terms, 0)

    o_s[...] = jnp.zeros_like(o_s)

    def scan_step(i, carry):
        c_b = jnp.where(i < n_ctx, n_ctx - 1 - i, n_ctx + n_tot - 1 - i)
        new = []
        for d, c, s in ((0, i, carry[0]), (1, c_b, carry[1])):
            r1 = jnp.dot(l1_s[d, c], s.astype(BF16), preferred_element_type=F32)
            v_new = u_s[d, c] - r1[:C]
            r2 = jnp.dot(l2_s[d, c], v_new.astype(BF16), preferred_element_type=F32)
            rows = pl.ds(pl.multiple_of(c * C, C), C)
            o_s[rows, :] = o_s[rows, :] + r1[C:] + r2[:C]
            new.append(s * gl_s[d, c] + r2[C:])
        return tuple(new)

    zero = jnp.zeros((DK_A, DK_A), F32)
    lax.fori_loop(0, n_tot, scan_step, (zero, zero))

    def finish(z_ref, o_ref, off):
        n = z_ref.shape[1]
        o = o_s[off:off + n, :]
        o = o * lax.rsqrt(jnp.mean(o * o, -1, keepdims=True) + EPS) * nw_ref[...]
        z = z_ref[0].astype(F32)
        o_ref[0] = (o * (z * jax.nn.sigmoid(z))).astype(o_ref.dtype)

    finish(zc_ref, oc_ref, 0)
    finish(zl_ref, ol_ref, lc)


def _gdn_call(p_c, p_l, ab_c, ab_l, conv_w, a_log, dt_bias, norm_w, h_a):
    B, L, _ = p_l.shape
    lc = p_c.shape[1]
    t = lc + L
    n_tot, n_ctx = t // CHUNK, lc // CHUNK
    d_a = h_a * DK_A
    ab = jnp.concatenate([ab_c, ab_l], 1)[..., :4 * h_a].reshape(B, t, 2, 2, h_a)
    g_row = ab[:, :, :, 0].transpose(0, 3, 2, 1).reshape(B, h_a, 2, n_tot, CHUNK)
    b_col = ab[:, :, :, 1].transpose(0, 3, 1, 2)

    def col(k, n):
        return pl.BlockSpec((1, n, DK_A), lambda b, h, *_: (b, 0, k * h_a + h))

    def tap(k):
        return pl.BlockSpec((CONV_W, DK_A), lambda b, h, *_: (0, k * h_a + h))

    grid_spec = pltpu.PrefetchScalarGridSpec(
        num_scalar_prefetch=2,
        grid=(B, h_a),
        in_specs=[col(0, lc), col(1, lc), col(2, lc), col(3, lc),
                  col(0, L), col(1, L), col(2, L), col(3, L),
                  tap(0), tap(1), tap(2),
                  pl.BlockSpec((1, 1, 2, n_tot, CHUNK), lambda b, h, *_: (b, h, 0, 0, 0)),
                  pl.BlockSpec((1, 1, t, 2), lambda b, h, *_: (b, h, 0, 0)),
                  pl.BlockSpec((1, DK_A), lambda b, h, *_: (0, 0))],
        out_specs=[pl.BlockSpec((1, lc, DK_A), lambda b, h, *_: (b, 0, h)),
                   pl.BlockSpec((1, L, DK_A), lambda b, h, *_: (b, 0, h))],
        scratch_shapes=[pltpu.VMEM((t, DK_A), F32), pltpu.VMEM((t, DK_A), F32), pltpu.VMEM((t, DK_A), F32),
                        pltpu.VMEM((t, DK_A), F32),
                        pltpu.VMEM((2, n_tot, CHUNK, DK_A), F32),
                        pltpu.VMEM((2, n_tot, 2 * CHUNK, DK_A), BF16),
                        pltpu.VMEM((2, n_tot, CHUNK + DK_A, CHUNK), BF16),
                        pltpu.VMEM((2, n_tot, 1, DK_A), F32),
                        pltpu.VMEM((2, n_tot, CHUNK), F32)],
    )
    return pl.pallas_call(
        functools.partial(_gdn_kernel, h_a=h_a, n_ctx=n_ctx),
        grid_spec=grid_spec,
        out_shape=[jax.ShapeDtypeStruct((B, lc, d_a), BF16), jax.ShapeDtypeStruct((B, L, d_a), BF16)],
        compiler_params=_params(("parallel", "parallel"), 48),
        name="gdn",
    )(a_log.reshape(-1), dt_bias.reshape(-1), p_c, p_c, p_c, p_c, p_l, p_l, p_l, p_l,
      conv_w, conv_w, conv_w, g_row, b_col, norm_w.reshape(1, DK_A))


_NT = (((1,), (1,)), ((), ()))


def _softmax_rows(s, sink=None):
    m = jnp.max(s, -1, keepdims=True)
    if sink is not None:
        m = jnp.maximum(m, sink)
    e = jnp.exp(s - m)
    den = jnp.sum(e, -1, keepdims=True)
    if sink is not None:
        den = den + jnp.exp(sink - m)
    return (e * (1.0 / den)).astype(BF16)


def _gqa_kernel(sink_ref, q_ref, kvl_ref, kvc_ref, cos_ref, sa_ref, sb_ref, o_ref, *, n_heads, latent):
    G = n_heads // KV_B
    P = 2 * DH
    n = pl.program_id(1)
    seq = kvl_ref.shape[1]
    lc = kvc_ref.shape[1]

    def rope(x, r0):
        reps = x.shape[1] // P
        rows = pl.ds(pl.multiple_of(r0, QBLK), QBLK)
        c, a, b = [jnp.concatenate([t[rows, :]] * reps, 1) if reps > 1 else t[rows, :]
                   for t in (cos_ref, sa_ref, sb_ref)]
        w = x.shape[1]
        return x * c + pltpu.roll(x, w - DH // 4, 1) * a + pltpu.roll(x, DH // 4, 1) * b

    q = q_ref[0].astype(F32) * DH ** -0.5
    kc = kvc_ref[0][:, :P].astype(F32)
    vc = kvc_ref[0][:, P:].astype(F32)
    if latent:
        q = rope(q, n * QBLK)
        ks, vs = [], []
        for o in (-1, 0, 1):
            s0 = jnp.clip((n + o) * QBLK, 0, seq - QBLK)
            kv = kvl_ref[0, pl.ds(pl.multiple_of(s0, QBLK), QBLK), :].astype(F32)
            ks.append(rope(kv[:, :P], s0))
            vs.append(kv[:, P:])
        k = jnp.concatenate(ks + [kc], 0)
        v = jnp.concatenate(vs + [vc], 0)
        nk = 3 * QBLK + lc
        qi = lax.broadcasted_iota(jnp.int32, (QBLK, nk), 0)
        kj = lax.broadcasted_iota(jnp.int32, (QBLK, nk), 1)
        kpos = n * QBLK - QBLK + kj
        rel = kj - QBLK - qi
        valid = ((rel >= -WIN) & (rel <= WIN) & (kpos >= 0) & (kpos < seq)) | (kj >= 3 * QBLK)
    else:
        k, v = kc, vc
        valid = None
    q = q.astype(BF16)
    k_sw = pltpu.roll(k, DH, 1).astype(BF16)
    v_sw = pltpu.roll(v, DH, 1).astype(BF16)
    k = k.astype(BF16)
    v = v.astype(BF16)
    lane = lax.broadcasted_iota(jnp.int32, (QBLK, P), 1)
    zero = jnp.zeros((QBLK, P), BF16)
    for p in range(n_heads // 2):
        kvh = (2 * p) // G
        qp = q[:, p * P:(p + 1) * P]
        outs = []
        for half in range(2):
            qh = jnp.where(lane < DH, qp, zero) if half == 0 else jnp.where(lane >= DH, qp, zero)
            straight = (kvh == half)
            kk = k if straight else k_sw
            vv = v if straight else v_sw
            s = lax.dot_general(qh, kk, _NT, preferred_element_type=F32)
            if valid is not None:
                s = jnp.where(valid, s, NEG_INF)
            pr = _softmax_rows(s, sink_ref[2 * p + half])
            outs.append(jnp.dot(pr, vv, preferred_element_type=F32))
        o_ref[0, :, p * P:(p + 1) * P] = jnp.where(lane < DH, outs[0], outs[1]).astype(o_ref.dtype)


def _gqa_call(p_q, p_l, p_c, sink, tables, d_model, d_b, latent):
    B, nq, _ = p_q.shape
    L, lc = p_l.shape[1], p_c.shape[1]
    kvw = 2 * KV_B * DH
    n_heads = d_b // DH
    assert (n_heads // KV_B) % 2 == 0
    grid_spec = pltpu.PrefetchScalarGridSpec(
        num_scalar_prefetch=1,
        grid=(B, nq // QBLK),
        in_specs=[pl.BlockSpec((1, QBLK, d_b), lambda b, n, *_: (b, n, 2 * d_model // d_b)),
                  pl.BlockSpec((1, L, kvw), lambda b, n, *_: (b, 0, 3 * d_model // kvw)),
                  pl.BlockSpec((1, lc, kvw), lambda b, n, *_: (b, 0, 3 * d_model // kvw)),
                  pl.BlockSpec((L, 2 * DH), lambda b, n, *_: (0, 0)),
                  pl.BlockSpec((L, 2 * DH), lambda b, n, *_: (0, 0)),
                  pl.BlockSpec((L, 2 * DH), lambda b, n, *_: (0, 0))],
        out_specs=pl.BlockSpec((1, QBLK, d_b), lambda b, n, *_: (b, n, 0)),
    )
    return pl.pallas_call(
        functools.partial(_gqa_kernel, n_heads=n_heads, latent=latent),
        grid_spec=grid_spec,
        out_shape=jax.ShapeDtypeStruct((B, nq, d_b), BF16),
        compiler_params=_params(("parallel", "parallel"), 48),
        name="window_gqa" if latent else "ctx_gqa",
    )(sink, p_q, p_l, p_c, *tables)


def _rope_tables(n_tokens):
    t = jnp.arange(n_tokens)
    row = (t // GRID_W).astype(F32)
    col = (t % GRID_W).astype(F32)
    half = DH // 2
    inv = ROPE_BASE ** (-jnp.arange(0, half, 2, dtype=F32) / half)
    ang_r = row[:, None] * inv[None]
    ang_c = col[:, None] * inv[None]
    ang = jnp.concatenate([ang_r, ang_r, ang_c, ang_c], -1)
    cos, sin = jnp.cos(ang), jnp.sin(ang)
    first = (jnp.arange(DH) % half) < half // 2
    sa = jnp.where(first, -sin, 0.0)
    sb = jnp.where(first, 0.0, sin)
    return tuple(jnp.concatenate([u, u], -1) for u in (cos, sa, sb))


def _na_kernel(q_ref, k_ref, v_ref, kc_ref, vc_ref, tbl_ref, o_ref, *, n_heads, rows, latent):
    P = 2 * DH
    r = pl.program_id(1)
    nq = q_ref.shape[1]
    nb = KH_MAX * GRID_W
    q = (q_ref[0].astype(F32) * DH ** -0.5).astype(BF16)
    lane = lax.broadcasted_iota(jnp.int32, (nq, P), 1)
    zero = jnp.zeros((nq, P), BF16)
    if latent:
        r0 = jnp.clip(r - KH_MAX // 2, 0, rows - KH_MAX)
        band = pl.ds(pl.multiple_of(r0 * GRID_W, GRID_W), nb)
        d0 = r0 - r + (KH_MAX - 1)
    for p in range(n_heads // 2):
        cols = slice(p * P, (p + 1) * P)
        qp = q[:, cols]
        kc = kc_ref[0, :, cols]
        vc = vc_ref[0, :, cols]
        if latent:
            kk = jnp.concatenate([k_ref[0, band, cols], kc], 0)
            vv = jnp.concatenate([v_ref[0, band, cols], vc], 0)
        else:
            kk, vv = kc, vc
        outs = []
        for half in range(2):
            qh = jnp.where(lane < DH, qp, zero) if half == 0 else jnp.where(lane >= DH, qp, zero)
            s = lax.dot_general(qh, kk, _NT, preferred_element_type=F32)
            if latent:
                bias = jnp.concatenate([tbl_ref[2 * p + half, d0 + 2 * t] for t in range(KH_MAX // 2)], 1)
                s = jnp.concatenate([s[:, :nb] + bias, s[:, nb:]], 1)
            outs.append(jnp.dot(_softmax_rows(s), vv, preferred_element_type=F32))
        o_ref[0, :, cols] = jnp.where(lane < DH, outs[0], outs[1]).astype(o_ref.dtype)


def _na_bias_table(rpb):
    qc = jnp.arange(GRID_W)[:, None]
    kc = jnp.arange(GRID_W)[None, :]
    dc = jnp.clip(kc - qc + (KW - 1), 0, 2 * KW - 2)
    q_cs = jnp.clip(qc - KW // 2, 0, GRID_W - KW)
    in_win = (kc >= q_cs) & (kc < q_cs + KW)
    t = jnp.where(in_win[None, None], rpb[:, :, dc], NEG_INF)
    return jnp.concatenate([t[:, :-1], t[:, 1:]], -1)


def _na_call(p_q, p_l, p_c, tbl, d_model, d_c, latent):
    B, nq, _ = p_q.shape
    L, lc = p_l.shape[1], p_c.shape[1]
    n_heads = d_c // DH
    rows = L // GRID_W
    assert rows >= KH_MAX and n_heads % 2 == 0
    cb = 2 * d_model // d_c
    qb = GRID_W
    return pl.pallas_call(
        functools.partial(_na_kernel, n_heads=n_heads, rows=rows, latent=latent),
        grid=(B, nq // qb),
        in_specs=[pl.BlockSpec((1, qb, d_c), lambda b, r: (b, r, cb + 1)),
                  pl.BlockSpec((1, L, d_c), lambda b, r: (b, 0, cb + 2)),
                  pl.BlockSpec((1, L, d_c), lambda b, r: (b, 0, cb + 3)),
                  pl.BlockSpec((1, lc, d_c), lambda b, r: (b, 0, cb + 2)),
                  pl.BlockSpec((1, lc, d_c), lambda b, r: (b, 0, cb + 3)),
                  pl.BlockSpec(tbl.shape, lambda b, r: (0, 0, 0, 0))],
        out_specs=pl.BlockSpec((1, qb, d_c), lambda b, r: (b, r, 0)),
        out_shape=jax.ShapeDtypeStruct((B, nq, d_c), BF16),
        compiler_params=_params(("parallel", "parallel"), 48),
        name="nbr_attn" if latent else "ctx_attn",
    )(p_q, p_l, p_l, p_c, p_c, tbl)


def _l2norm(u):
    return u * lax.rsqrt(jnp.sum(u * u, -1, keepdims=True) + EPS)


def _short_conv(u, w):
    pad = CONV_W // 2
    L = u.shape[1]
    up = jnp.pad(u, ((0, 0), (pad, pad), (0, 0)))
    return sum(up[:, i:i + L] * w[i] for i in range(CONV_W))


def _gdn_prepare(qkv, ab, conv_w, a_log, dt_bias, h_a):
    B, L, _ = qkv.shape
    qkv = jax.nn.silu(_short_conv(qkv, conv_w))
    q, k, v = jnp.split(qkv, 3, -1)
    q = _l2norm(q.reshape(B, L, h_a, DK_A))
    k = _l2norm(k.reshape(B, L, h_a, DK_A))
    v = v.reshape(B, L, h_a, DK_A)
    ab = ab.reshape(B, L, 2, 2, h_a)
    g = -jnp.exp(a_log) * jax.nn.softplus(ab[:, :, :, 0] + dt_bias)
    beta = jax.nn.sigmoid(ab[:, :, :, 1])
    return q, k, v, g, beta


def _gdn_chunked(q, k, v, g, beta, s0):
    B, L, H, dk = k.shape
    dv = v.shape[-1]
    n = L // CHUNK

    def chunks(u):
        return u.reshape(B, n, CHUNK, H, -1).transpose(0, 3, 1, 2, 4)

    qc, kc, vc = chunks(q * dk ** -0.5), chunks(k), chunks(v)
    gc = g.reshape(B, n, CHUNK, H).transpose(0, 3, 1, 2)
    bc = beta.reshape(B, n, CHUNK, H).transpose(0, 3, 1, 2)
    gcum = jnp.cumsum(gc, -1)
    tril = jnp.tril(jnp.ones((CHUNK, CHUNK), bool))
    strict = jnp.tril(jnp.ones((CHUNK, CHUNK), bool), -1)
    diff = gcum[..., :, None] - gcum[..., None, :]
    decay = jnp.where(tril, jnp.exp(jnp.where(tril, diff, 0.0)), 0.0)
    kb = kc * bc[..., None]
    vb = vc * bc[..., None]
    a = jnp.where(strict, jnp.einsum('bhnid,bhnjd->bhnij', kb, kc) * decay, 0.0)
    eye = jnp.eye(CHUNK, dtype=F32)
    tinv = lax.linalg.triangular_solve(eye + a, jnp.broadcast_to(eye, a.shape),
                                       left_side=True, lower=True, unit_diagonal=True)
    u = tinv @ vb
    w = tinv @ (kb * jnp.exp(gcum)[..., None])
    attn = jnp.einsum('bhnid,bhnjd->bhnij', qc, kc) * decay
    qg = qc * jnp.exp(gcum)[..., None]
    kdec = kc * jnp.exp(gcum[..., -1:] - gcum)[..., None]
    glast = jnp.exp(gcum[..., -1])
    xs = tuple(jnp.moveaxis(t, 2, 0) for t in (u, w, attn, qg, kdec, glast))

    def step(s, inp):
        u_i, w_i, attn_i, qg_i, kdec_i, gl_i = inp
        v_new = u_i - w_i @ s
        o_i = qg_i @ s + attn_i @ v_new
        s = s * gl_i[..., None, None] + jnp.einsum('bhcd,bhce->bhde', kdec_i, v_new)
        return s, o_i

    s_final, o = lax.scan(step, s0, xs)
    o = jnp.moveaxis(o, 0, 2).transpose(0, 2, 3, 1, 4).reshape(B, L, H, dv)
    return o, s_final


def _gdn_output(o, z, norm_w, h_a):
    B, L = z.shape[:2]
    o = o * lax.rsqrt(jnp.mean(o * o, -1, keepdims=True) + EPS) * norm_w
    zg = jax.nn.silu(z.reshape(B, L, h_a, DK_A))
    return (o * zg).reshape(B, L, h_a * DK_A)


def _gated_deltanet(qkv_c, ab_c, z_c, qkv_l, ab_l, z_l, conv_w, a_log, dt_bias, norm_w, ctx_out, h_a):
    qc, kc, vc, gc, bc = _gdn_prepare(qkv_c, ab_c, conv_w, a_log, dt_bias, h_a)
    ql, kl, vl, gl, bl = _gdn_prepare(qkv_l, ab_l, conv_w, a_log, dt_bias, h_a)
    B = ql.shape[0]
    s0 = jnp.zeros((B, h_a, DK_A, DK_A), F32)
    o_c_sum, o_l_sum = 0.0, 0.0
    for d in range(2):
        seq_c = (qc, kc, vc, gc[:, :, d], bc[:, :, d])
        seq_l = (ql, kl, vl, gl[:, :, d], bl[:, :, d])
        if d == 1:
            seq_c = tuple(jnp.flip(t, 1) for t in seq_c)
            seq_l = tuple(jnp.flip(t, 1) for t in seq_l)
        o_c, s_c = _gdn_chunked(*seq_c, s0)
        o_l, _ = _gdn_chunked(*seq_l, s_c)
        if d == 1:
            o_c, o_l = jnp.flip(o_c, 1), jnp.flip(o_l, 1)
        o_c_sum = o_c_sum + o_c
        o_l_sum = o_l_sum + o_l
    o_lat = _gdn_output(o_l_sum, z_l, norm_w, h_a)
    o_ctx = _gdn_output(o_c_sum, z_c, norm_w, h_a) if ctx_out else None
    return o_ctx, o_lat


def _rope_tables_ref(n_tokens, head_dim):
    t = jnp.arange(n_tokens)
    row = (t // GRID_W).astype(F32)
    col = (t % GRID_W).astype(F32)
    half = head_dim // 2
    inv = ROPE_BASE ** (-jnp.arange(0, half, 2, dtype=F32) / half)
    ang_r = row[:, None] * inv[None]
    ang_c = col[:, None] * inv[None]
    ang = jnp.concatenate([ang_r, ang_r, ang_c, ang_c], -1)
    return jnp.cos(ang), jnp.sin(ang)


def _apply_rope(x, cos, sin):
    half = x.shape[-1] // 2

    def rot(u):
        u1, u2 = jnp.split(u, 2, -1)
        return jnp.concatenate([-u2, u1], -1)

    xr = jnp.concatenate([rot(x[..., :half]), rot(x[..., half:])], -1)
    return x * cos[None, :, None, :] + xr * sin[None, :, None, :]


def _softmax_with_sink(s, sink):
    sk = jnp.broadcast_to(sink, s.shape[:-1] + (1,))
    p = jax.nn.softmax(jnp.concatenate([s, sk], -1), -1)
    return p[..., :-1]


def _window_gqa(q_c, kv_c, q_l, kv_l, sink, cos, sin, ctx_out):
    B, L = q_l.shape[:2]
    Lc = q_c.shape[1]
    h_b = q_l.shape[-1] // DH
    G = h_b // KV_B
    n = L // QBLK
    J = 3 * QBLK
    scale = DH ** -0.5
    k_l, v_l = [u.reshape(B, L, KV_B, DH) for u in jnp.split(kv_l, 2, -1)]
    k_c, v_c = [u.reshape(B, Lc, KV_B, DH) for u in jnp.split(kv_c, 2, -1)]
    q_l = _apply_rope(q_l.reshape(B, L, h_b, DH), cos, sin)
    k_l = _apply_rope(k_l, cos, sin)
    sink = sink.reshape(KV_B, G, 1, 1)

    def band(u):
        up = jnp.pad(u, ((0, 0), (QBLK, QBLK), (0, 0), (0, 0))).reshape(B, n + 2, QBLK, KV_B, DH)
        return jnp.concatenate([up[:, :-2], up[:, 1:-1], up[:, 2:]], axis=2)

    k_w, v_w = band(k_l), band(v_l)
    qb = q_l.reshape(B, n, QBLK, KV_B, G, DH)
    qi = jnp.arange(QBLK)[:, None]
    kj = jnp.arange(J)[None]
    kpos = (jnp.arange(n) * QBLK)[:, None, None] - QBLK + kj[None]
    in_win = (jnp.abs(kj - QBLK - qi)[None] <= WIN) & (kpos >= 0) & (kpos < L)
    s_w = jnp.einsum('bnqkgd,bnjkd->bnkgqj', qb, k_w) * scale
    s_w = jnp.where(in_win[None, :, None, None], s_w, NEG_INF)
    s_x = jnp.einsum('bnqkgd,bckd->bnkgqc', qb, k_c) * scale
    p = _softmax_with_sink(jnp.concatenate([s_w, s_x], -1), sink)
    o = (jnp.einsum('bnkgqj,bnjkd->bnqkgd', p[..., :J], v_w)
         + jnp.einsum('bnkgqc,bckd->bnqkgd', p[..., J:], v_c))
    o_lat = o.reshape(B, L, h_b * DH)
    o_ctx = None
    if ctx_out:
        qc = q_c.reshape(B, Lc, KV_B, G, DH)
        s_c = jnp.einsum('bqkgd,bckd->bkgqc', qc, k_c) * scale
        p_c = _softmax_with_sink(s_c, sink)
        o_ctx = jnp.einsum('bkgqc,bckd->bqkgd', p_c, v_c).reshape(B, Lc, h_b * DH)
    return o_ctx, o_lat


def _neighbourhood_attn(q_c, kv_c, q_l, kv_l, rpb, ctx_out):
    B, L = q_l.shape[:2]
    Lc = q_c.shape[1]
    h_c = q_l.shape[-1] // DH
    rows = L // GRID_W
    kh = min(KH_MAX, rows)
    ncb = GRID_W // NA_QCOLS
    m = kh * NA_KCOLS
    scale = DH ** -0.5
    k_l, v_l = [u.reshape(B, L, h_c, DH) for u in jnp.split(kv_l, 2, -1)]
    k_c, v_c = [u.reshape(B, Lc, h_c, DH) for u in jnp.split(kv_c, 2, -1)]
    r = jnp.arange(rows)
    row_idx = jnp.clip(r - kh // 2, 0, rows - kh)[:, None] + jnp.arange(kh)[None]
    cb = jnp.arange(ncb)
    col_idx = (jnp.clip(cb * NA_QCOLS - KW // 2, 0, GRID_W - NA_KCOLS)[:, None]
               + jnp.arange(NA_KCOLS)[None])

    def gather_band(u):
        u = u.reshape(B, rows, GRID_W, h_c, DH)[:, row_idx]
        u = u[:, :, :, col_idx]
        return u.transpose(0, 1, 3, 2, 4, 5, 6).reshape(B, rows, ncb, m, h_c, DH)

    k_n, v_n = gather_band(k_l), gather_band(v_l)
    qn = q_l.reshape(B, rows, ncb, NA_QCOLS, h_c, DH)
    q_col = cb[:, None] * NA_QCOLS + jnp.arange(NA_QCOLS)[None]
    q_cs = jnp.clip(q_col - KW // 2, 0, GRID_W - KW)
    key_col = jnp.tile(col_idx, (1, kh))
    key_row = jnp.repeat(row_idx, NA_KCOLS, axis=1)
    in_win = ((key_col[:, None, :] >= q_cs[:, :, None])
              & (key_col[:, None, :] < q_cs[:, :, None] + KW))
    dr = key_row - r[:, None] + (KH_MAX - 1)
    dc = jnp.clip(key_col[:, None, :] - q_col[:, :, None] + (KW - 1), 0, 2 * KW - 2)
    bias = rpb[:, dr[:, None, None, :], dc[None]]
    s_nb = jnp.einsum('brcqhd,brcmhd->bhrcqm', qn, k_n) * scale + bias
    s_nb = jnp.where(in_win, s_nb, NEG_INF)
    s_x = jnp.einsum('brcqhd,bkhd->bhrcqk', qn, k_c) * scale
    p = jax.nn.softmax(jnp.concatenate([s_nb, s_x], -1), -1)
    o = (jnp.einsum('bhrcqm,brcmhd->brcqhd', p[..., :m], v_n)
         + jnp.einsum('bhrcqk,bkhd->brcqhd', p[..., m:], v_c))
    o_lat = o.reshape(B, L, h_c * DH)
    o_ctx = None
    if ctx_out:
        qc = q_c.reshape(B, Lc, h_c, DH)
        s_c = jnp.einsum('bqhd,bkhd->bhqk', qc, k_c) * scale
        p_c = jax.nn.softmax(s_c, -1)
        o_ctx = jnp.einsum('bhqk,bkhd->bqhd', p_c, v_c).reshape(B, Lc, h_c * DH)
    return o_ctx, o_lat


def _route(logits):
    T = logits.shape[0]
    g_logits = logits[:, :N_GROUPS]
    g_prob = jax.nn.softmax(g_logits, -1)
    g_sel = jnp.argmax(g_logits, -1)
    e_logits = logits[:, N_GROUPS:N_GROUPS + N_EXPERTS].reshape(T, N_GROUPS, E_PER_GROUP)
    e_in = jnp.take_along_axis(e_logits, g_sel[:, None, None], 1)[:, 0]
    top_v, top_i = lax.top_k(e_in, TOP_K)
    wts = jax.nn.softmax(top_v, -1) * jnp.take_along_axis(g_prob, g_sel[:, None], 1)
    expert = g_sel[:, None] * E_PER_GROUP + top_i
    return expert.astype(jnp.int32), wts


def _dispatch(expert, tm):
    T = expert.shape[0]
    S = T * TOP_K
    e_flat = expert.reshape(S)
    order = jnp.argsort(e_flat)
    e_sorted = e_flat[order]
    tok_sorted = order // TOP_K
    counts = jnp.bincount(e_flat, length=N_EXPERTS)
    starts = jnp.cumsum(counts) - counts
    padded = (counts + tm - 1) // tm * tm
    pends = jnp.cumsum(padded)
    pstarts = pends - padded
    dest = pstarts[e_sorted] + jnp.arange(S) - starts[e_sorted]
    n_tiles = -(-S // tm) + N_EXPERTS
    tile_expert = jnp.minimum(jnp.searchsorted(pends, jnp.arange(n_tiles) * tm, side='right'),
                              N_EXPERTS - 1).astype(jnp.int32)
    n_used = (pends[-1] // tm).astype(jnp.int32).reshape(1)
    pos = jnp.zeros((S,), jnp.int32).at[order].set(dest.astype(jnp.int32)).reshape(T, TOP_K)
    src_tok = jnp.zeros((n_tiles * tm,), jnp.int32).at[dest].set(tok_sorted.astype(jnp.int32))
    return tile_expert, n_used, pos, src_tok


def _moe(f, logits, w_gate, w_up, w_down):
    T, d = f.shape
    expert, wts = _route(logits)
    tile_expert, n_used, pos, src_tok = _dispatch(expert, MOE_TM)
    buf = f[src_tok]
    out = _moe_call(tile_expert, n_used, buf, w_gate, w_up, w_down, MOE_TM)
    y = out[pos[:, 0]].astype(F32) * wts[:, 0:1] + out[pos[:, 1]].astype(F32) * wts[:, 1:2]
    return y


def kernel(x, c, ctx, c_ctx, w_mod, b_mod, norm_mix, norm_ffn, w_in, conv_a, a_log, dt_bias, gdn_norm, sink_b, rpb_c, w_out, w_router_group, b_router_group, w_router_expert, b_router_expert, w_gate, w_up, w_down, norm_final):
    B, L, D = x.shape
    Lc = ctx.shape[1]
    depth = w_mod.shape[0]
    d_a = D // 2
    d_b = D // 4
    d_c = D - d_a - d_b
    h_a = d_a // DK_A
    n_ab = 4 * h_a
    kvb = 2 * KV_B * DH
    o_ab = 4 * d_a
    n_main = 4 * d_a + d_b + kvb + 3 * d_c

    cc = jnp.zeros((MOD_ROWS, D), F32).at[:B].set(c).at[B].set(c_ctx)
    mod = _mod_call(cc, w_mod, b_mod).reshape(depth, MOD_ROWS, N_MOD, D)
    rope = _rope_tables(L)

    tn_in = n_main
    for cand in (1280, 1024, 768, 640, 512, 384, 256, 128):
        if n_main % cand == 0:
            tn_in = cand
            break

    x_l, x_c = x, ctx
    for l in range(depth):
        last = l == depth - 1
        ctx_out = not last
        mod_l = mod[l, :B]
        mod_c = mod[l, B:B + 1]
        ml = [mod_l[:, i:i + 1] for i in range(N_MOD)]
        mc = [mod_c[:, i:i + 1] for i in range(N_MOD)]
        wl = w_in[l]
        o_qb = o_ab + n_ab
        o_kvb = o_qb + d_b
        o_qc = o_kvb + kvb
        w_main = jnp.concatenate([wl[:, :o_ab], wl[:, o_qb:o_kvb], wl[:, o_qc:], wl[:, o_kvb:o_qc]], -1).astype(BF16)
        w_ab = jnp.pad(wl[:, o_ab:o_qb], ((0, 0), (0, LANE - n_ab))).astype(BF16)
        gain_mix = norm_mix[l].reshape(1, D)
        p_l, ab_l = _inproj_call(x_l, ml[0], ml[1], gain_mix, w_main, w_ab, min(L, 1024), tn_in)
        p_c, ab_c = _inproj_call(x_c, mc[0], mc[1], gain_mix, w_main, w_ab, min(Lc, 1024), tn_in)

        oA_c, oA_l = _gdn_call(p_c, p_l, ab_c, ab_l, conv_a[l], a_log[l], dt_bias[l], gdn_norm[l], h_a)
        oB_l = _gqa_call(p_l, p_l, p_c, sink_b[l], rope, D, d_b, True)
        tbl = _na_bias_table(rpb_c[l])
        oC_l = _na_call(p_l, p_l, p_c, tbl, D, d_c, True)
        o_l = jnp.concatenate([oA_l, oB_l, oC_l], -1)
        if ctx_out:
            oB_c = _gqa_call(p_c, p_l, p_c, sink_b[l], rope, D, d_b, False)
            oC_c = _na_call(p_c, p_l, p_c, tbl, D, d_c, False)

        w_o = w_out[l].astype(BF16)
        w_r = jnp.pad(jnp.concatenate([w_router_group[l], w_router_expert[l]], -1),
                      ((0, 0), (0, ROUTER_COLS - N_GROUPS - N_EXPERTS)))
        b_r = jnp.pad(jnp.concatenate([b_router_group[l], b_router_expert[l]], -1),
                      (0, ROUTER_COLS - N_GROUPS - N_EXPERTS)).reshape(1, ROUTER_COLS)
        gain_ffn = norm_ffn[l].reshape(1, D)
        x_l, f_l, lg_l = _outproj_call(o_l, x_l, ml[2], ml[3], ml[4], gain_ffn, w_o, w_r, b_r, min(L, 256))
        wg, wu, wd = w_gate[l].astype(BF16), w_up[l].astype(BF16), w_down[l].astype(BF16)
        if ctx_out:
            o_c = jnp.concatenate([oA_c, oB_c, oC_c], -1)
            x_c, f_c, lg_c = _outproj_call(o_c, x_c, mc[2], mc[3], mc[4], gain_ffn, w_o, w_r, b_r, min(Lc, 256))
            f_all = jnp.concatenate([f_c.reshape(B * Lc, D), f_l.reshape(B * L, D)], 0)
            lg_all = jnp.concatenate([lg_c.reshape(B * Lc, ROUTER_COLS), lg_l.reshape(B * L, ROUTER_COLS)], 0)
            y = _moe(f_all, lg_all, wg, wu, wd)
            y_c, y_l = y[:B * Lc], y[B * Lc:]
            x_c = _resid_call(x_c, y_c.reshape(B, Lc, D), mc[5], gain_ffn, min(Lc, 256), False)
        else:
            y_l = _moe(f_l.reshape(B * L, D), lg_l.reshape(B * L, ROUTER_COLS), wg, wu, wd)
        x_l = _resid_call(x_l, y_l.reshape(B, L, D), ml[5], norm_final.reshape(1, D), min(L, 256), last)
    return x_l
```

```python
import functools
import math

import jax
import jax.numpy as jnp
from jax import lax
from jax.experimental import pallas as pl
from jax.experimental.pallas import tpu as pltpu

F32 = jnp.float32
BF16 = jnp.bfloat16
HIGHEST = lax.Precision.HIGHEST

EPS = 1e-6
NEG_INF = -1e30
N_MOD = 6
GRID_W = 64
DH = 64
DK_A = 128
CONV_W = 5
CHUNK = 64
KV_B = 2
WIN = 128
QBLK = 128
ROPE_BASE = 10000.0
KH_MAX = 8
KW = 16
NA_QCOLS = 16
NA_KCOLS = 32
N_GROUPS = 4
E_PER_GROUP = 8
N_EXPERTS = N_GROUPS * E_PER_GROUP
TOP_K = 2
LANE = 128
MOD_ROWS = 16
ROUTER_COLS = LANE
MOE_TM = 256


def _params(sem, vmem_mb):
    return pltpu.CompilerParams(dimension_semantics=sem, vmem_limit_bytes=vmem_mb << 20)


def _mod_kernel(c_ref, w_ref, b_ref, o_ref):
    c = c_ref[...]
    s = c * jax.nn.sigmoid(c)
    o_ref[0] = jnp.dot(s, w_ref[0], preferred_element_type=F32, precision=HIGHEST) + b_ref[0]


def _mod_call(cc, w_mod, b_mod):
    depth, d, n = w_mod.shape
    tn = min(n, 1024)
    return pl.pallas_call(
        _mod_kernel,
        grid=(depth, n // tn),
        in_specs=[pl.BlockSpec((MOD_ROWS, d), lambda l, j: (0, 0)),
                  pl.BlockSpec((1, d, tn), lambda l, j: (l, 0, j)),
                  pl.BlockSpec((1, 1, tn), lambda l, j: (l, 0, j))],
        out_specs=pl.BlockSpec((1, MOD_ROWS, tn), lambda l, j: (l, 0, j)),
        out_shape=jax.ShapeDtypeStruct((depth, MOD_ROWS, n), F32),
        compiler_params=_params(("parallel", "parallel"), 48),
        name="mod_proj",
    )(cc, w_mod, b_mod.reshape(depth, 1, n))


def _inproj_kernel(x_ref, shift_ref, scale_ref, g_ref, w_ref, wab_ref, o_ref, ab_ref, h_ref):
    @pl.when(pl.program_id(2) == 0)
    def _():
        x = x_ref[0]
        y = x * lax.rsqrt(jnp.mean(x * x, -1, keepdims=True) + EPS) * g_ref[...]
        h = (y * (1.0 + scale_ref[0]) + shift_ref[0]).astype(BF16)
        h_ref[...] = h
        ab_ref[0] = jnp.dot(h, wab_ref[...], preferred_element_type=F32)

    o_ref[0] = jnp.dot(h_ref[...], w_ref[...], preferred_element_type=F32).astype(o_ref.dtype)


def _inproj_call(x, shift, scale, gain, w_main, w_ab, tm, tn):
    bn, seq, d = x.shape
    n = w_main.shape[1]
    per_batch = shift.shape[0] > 1
    mod_map = (lambda b, i, j: (b, 0, 0)) if per_batch else (lambda b, i, j: (0, 0, 0))
    return pl.pallas_call(
        _inproj_kernel,
        grid=(bn, seq // tm, n // tn),
        in_specs=[pl.BlockSpec((1, tm, d), lambda b, i, j: (b, i, 0)),
                  pl.BlockSpec((1, 1, d), mod_map),
                  pl.BlockSpec((1, 1, d), mod_map),
                  pl.BlockSpec((1, d), lambda b, i, j: (0, 0)),
                  pl.BlockSpec((d, tn), lambda b, i, j: (0, j)),
                  pl.BlockSpec((d, LANE), lambda b, i, j: (0, 0))],
        out_specs=[pl.BlockSpec((1, tm, tn), lambda b, i, j: (b, i, j)),
                   pl.BlockSpec((1, tm, LANE), lambda b, i, j: (b, i, 0))],
        out_shape=[jax.ShapeDtypeStruct((bn, seq, n), BF16),
                   jax.ShapeDtypeStruct((bn, seq, LANE), F32)],
        scratch_shapes=[pltpu.VMEM((tm, d), BF16)],
        compiler_params=_params(("parallel", "parallel", "arbitrary"), 56),
        name="in_proj",
    )(x, shift, scale, gain, w_main, w_ab)


def _outproj_kernel(o_ref, x_ref, gate_ref, shift_ref, scale_ref, g_ref, w_ref, wr_ref, br_ref,
                    xn_ref, f_ref, lg_ref):
    acc = jnp.dot(o_ref[0], w_ref[...], preferred_element_type=F32)
    xn = x_ref[0] + gate_ref[0] * acc
    xn_ref[0] = xn
    y = xn * lax.rsqrt(jnp.mean(xn * xn, -1, keepdims=True) + EPS) * g_ref[...]
    f = y * (1.0 + scale_ref[0]) + shift_ref[0]
    f_hi = f.astype(BF16)
    f_ref[0] = f_hi
    f_lo = (f - f_hi.astype(F32)).astype(BF16)
    hh_hl = jnp.dot(f_hi, wr_ref[...], preferred_element_type=F32)
    lh = jnp.dot(f_lo, wr_ref[:, :ROUTER_COLS], preferred_element_type=F32)
    lg_ref[0] = hh_hl[:, :ROUTER_COLS] + hh_hl[:, ROUTER_COLS:] + lh + br_ref[...]


def _outproj_call(o, x, gate, shift, scale, gain, w_out, w_r, b_r, tm):
    bn, seq, d = x.shape
    per_batch = gate.shape[0] > 1
    mod_map = (lambda b, i: (b, 0, 0)) if per_batch else (lambda b, i: (0, 0, 0))
    row = pl.BlockSpec((1, tm, d), lambda b, i: (b, i, 0))
    return pl.pallas_call(
        _outproj_kernel,
        grid=(bn, seq // tm),
        in_specs=[row, row,
                  pl.BlockSpec((1, 1, d), mod_map),
                  pl.BlockSpec((1, 1, d), mod_map),
                  pl.BlockSpec((1, 1, d), mod_map),
                  pl.BlockSpec((1, d), lambda b, i: (0, 0)),
                  pl.BlockSpec((d, d), lambda b, i: (0, 0)),
                  pl.BlockSpec((d, 2 * ROUTER_COLS), lambda b, i: (0, 0)),
                  pl.BlockSpec((1, ROUTER_COLS), lambda b, i: (0, 0))],
        out_specs=[row, row, pl.BlockSpec((1, tm, ROUTER_COLS), lambda b, i: (b, i, 0))],
        out_shape=[jax.ShapeDtypeStruct((bn, seq, d), F32),
                   jax.ShapeDtypeStruct((bn, seq, d), BF16),
                   jax.ShapeDtypeStruct((bn, seq, ROUTER_COLS), F32)],
        compiler_params=_params(("parallel", "parallel"), 56),
        name="out_proj",
    )(o, x, gate, shift, scale, gain, w_out, w_r, b_r)


def _moe_kernel(te_ref, nu_ref, x_ref, wg_ref, wu_ref, wd_ref, o_ref):
    @pl.when(pl.program_id(0) < nu_ref[0])
    def _():
        x = x_ref[...]
        g = jnp.dot(x, wg_ref[0], preferred_element_type=F32)
        u = jnp.dot(x, wu_ref[0], preferred_element_type=F32)
        a = (g * jax.nn.sigmoid(g) * u).astype(BF16)
        o_ref[...] = jnp.dot(a, wd_ref[0], preferred_element_type=F32).astype(o_ref.dtype)


def _moe_call(tile_expert, n_used, buf, w_gate, w_up, w_down, tm):
    p, d = buf.shape
    f = w_gate.shape[-1]
    grid_spec = pltpu.PrefetchScalarGridSpec(
        num_scalar_prefetch=2,
        grid=(p // tm,),
        in_specs=[pl.BlockSpec((tm, d), lambda i, te, nu: (i, 0)),
                  pl.BlockSpec((1, d, f), lambda i, te, nu: (te[i], 0, 0)),
                  pl.BlockSpec((1, d, f), lambda i, te, nu: (te[i], 0, 0)),
                  pl.BlockSpec((1, f, d), lambda i, te, nu: (te[i], 0, 0))],
        out_specs=pl.BlockSpec((tm, d), lambda i, te, nu: (i, 0)),
    )
    return pl.pallas_call(
        _moe_kernel,
        grid_spec=grid_spec,
        out_shape=jax.ShapeDtypeStruct((p, d), BF16),
        compiler_params=_params(("arbitrary",), 60),
        name="moe_experts",
    )(tile_expert, n_used, buf, w_gate, w_up, w_down)


ROUTE_TM = 256


def _route_kernel(lg_ref, ids_ref, wts_ref, cnt_ref, carry_ref):
    @pl.when(pl.program_id(0) == 0)
    def _():
        carry_ref[...] = jnp.zeros_like(carry_ref)

    lg = lg_ref[...]
    tm = lg.shape[0]
    lane = lax.broadcasted_iota(jnp.int32, lg.shape, 1)
    big = jnp.int32(ROUTER_COLS)
    is_g = lane < N_GROUPS
    gmax = jnp.max(jnp.where(is_g, lg, -jnp.inf), -1, keepdims=True)
    g_sel = jnp.min(jnp.where(is_g & (lg == gmax), lane, big), -1, keepdims=True)
    p_g = 1.0 / jnp.sum(jnp.where(is_g, jnp.exp(lg - gmax), 0.0), -1, keepdims=True)
    lo = N_GROUPS + E_PER_GROUP * g_sel
    in_e = (lane >= lo) & (lane < lo + E_PER_GROUP)
    el = jnp.where(in_e, lg, -jnp.inf)
    v1 = jnp.max(el, -1, keepdims=True)
    i1 = jnp.min(jnp.where(el == v1, lane, big), -1, keepdims=True)
    el2 = jnp.where(lane == i1, -jnp.inf, el)
    v2 = jnp.max(el2, -1, keepdims=True)
    i2 = jnp.min(jnp.where(el2 == v2, lane, big), -1, keepdims=True)
    t = jnp.exp(v2 - v1)
    w1 = p_g / (1.0 + t)
    w2 = w1 * t
    e1 = i1 - N_GROUPS
    e2 = i2 - N_GROUPS
    hit1 = lane == e1
    hit2 = lane == e2
    onehot = jnp.where(hit1 | hit2, 1.0, 0.0)
    ri = lax.broadcasted_iota(jnp.int32, (tm, tm), 0)
    ci = lax.broadcasted_iota(jnp.int32, (tm, tm), 1)
    below = jnp.where(ci < ri, 1.0, 0.0).astype(BF16)
    before = carry_ref[...] + jnp.dot(below, onehot.astype(BF16), preferred_element_type=F32)
    r1 = jnp.sum(jnp.where(hit1, before, 0.0), -1, keepdims=True).astype(jnp.int32)
    r2 = jnp.sum(jnp.where(hit2, before, 0.0), -1, keepdims=True).astype(jnp.int32)
    ids_ref[...] = jnp.where(lane == 0, e1, jnp.where(lane == 1, e2, jnp.where(lane == 2, r1, jnp.where(lane == 3, r2, 0))))
    wts_ref[...] = jnp.where(lane == 0, w1, jnp.where(lane == 1, w2, 0.0))
    carry_ref[...] = carry_ref[...] + jnp.sum(onehot, 0, keepdims=True)
    cnt_ref[...] = carry_ref[...]


def _route_call(logits):
    t = logits.shape[0]
    tm = min(t, ROUTE_TM)
    row = pl.BlockSpec((tm, ROUTER_COLS), lambda i: (i, 0))
    return pl.pallas_call(
        _route_kernel,
        grid=(t // tm,),
        in_specs=[row],
        out_specs=[row, row, pl.BlockSpec((1, ROUTER_COLS), lambda i: (0, 0))],
        out_shape=[jax.ShapeDtypeStruct((t, ROUTER_COLS), jnp.int32),
                   jax.ShapeDtypeStruct((t, ROUTER_COLS), F32),
                   jax.ShapeDtypeStruct((1, ROUTER_COLS), F32)],
        scratch_shapes=[pltpu.VMEM((1, ROUTER_COLS), F32)],
        compiler_params=_params(("arbitrary",), 32),
        name="route",
    )(logits)


def _resid_kernel(x_ref, y0_ref, y1_ref, w_ref, gate_ref, g_ref, o_ref, *, final_norm):
    w = w_ref[0]
    y = y0_ref[0].astype(F32) * w[:, 0:1] + y1_ref[0].astype(F32) * w[:, 1:2]
    xn = x_ref[0] + gate_ref[0] * y
    if final_norm:
        xn = xn * lax.rsqrt(jnp.mean(xn * xn, -1, keepdims=True) + EPS) * g_ref[...]
    o_ref[0] = xn


def _resid_call(x, y0, y1, w, gate, gain, tm, final_norm):
    bn, seq, d = x.shape
    per_batch = gate.shape[0] > 1
    mod_map = (lambda b, i: (b, 0, 0)) if per_batch else (lambda b, i: (0, 0, 0))
    row = pl.BlockSpec((1, tm, d), lambda b, i: (b, i, 0))
    return pl.pallas_call(
        functools.partial(_resid_kernel, final_norm=final_norm),
        grid=(bn, seq // tm),
        in_specs=[row, row, row, pl.BlockSpec((1, tm, ROUTER_COLS), lambda b, i: (b, i, 0)),
                  pl.BlockSpec((1, 1, d), mod_map), pl.BlockSpec((1, d), lambda b, i: (0, 0))],
        out_specs=row,
        out_shape=jax.ShapeDtypeStruct((bn, seq, d), F32),
        compiler_params=_params(("parallel", "parallel"), 48),
        name="moe_residual",
    )(x, y0, y1, w, gate, gain)


def _softplus(x):
    return jnp.maximum(x, 0.0) + jnp.log1p(jnp.exp(-jnp.abs(x)))


def _gdn_prep(x_ref, w_ref, dst_ref, off, mode):
    x = x_ref[0].astype(F32)
    n = x.shape[0]
    rows = lax.broadcasted_iota(jnp.int32, (n, 1), 0)
    pad = CONV_W // 2
    acc = x * w_ref[pad:pad + 1, :]
    for o in range(-pad, pad + 1):
        if o == 0:
            continue
        xs = pltpu.roll(x, (-o) % n, 0)
        valid = (rows + o >= 0) if o < 0 else (rows + o < n)
        acc = acc + jnp.where(valid, xs, 0.0) * w_ref[o + pad:o + pad + 1, :]
    y = acc * jax.nn.sigmoid(acc)
    if mode != "v":
        y = y * lax.rsqrt(jnp.sum(y * y, -1, keepdims=True) + EPS)
    if mode == "q":
        y = y * DK_A ** -0.5
    dst_ref[off:off + n, :] = y


def _gdn_kernel(alog_ref, dtb_ref,
                qc_ref, kc_ref, vc_ref, zc_ref, ql_ref, kl_ref, vl_ref, zl_ref,
                wq_ref, wk_ref, wv_ref, grow_ref, bcol_ref, nw_ref,
                oc_ref, ol_ref,
                q_s, k_s, v_s, o_s, u_s, l1_s, l2_s, gl_s, gc_s, *, h_a, n_ctx):
    C = CHUNK
    lc = qc_ref.shape[1]
    n_tot = u_s.shape[1]
    h = pl.program_id(1)

    _gdn_prep(qc_ref, wq_ref, q_s, 0, "q")
    _gdn_prep(kc_ref, wk_ref, k_s, 0, "k")
    _gdn_prep(vc_ref, wv_ref, v_s, 0, "v")
    _gdn_prep(ql_ref, wq_ref, q_s, lc, "q")
    _gdn_prep(kl_ref, wk_ref, k_s, lc, "k")
    _gdn_prep(vl_ref, wv_ref, v_s, lc, "v")

    NB = 4
    W = NB * C
    ii = lax.broadcasted_iota(jnp.int32, (C, W), 0)
    jl = lax.broadcasted_iota(jnp.int32, (C, W), 1)
    blk = jl // C
    jm = jl - blk * C
    eye4 = ii == jm
    tri4 = ((blk < 2) & (ii >= jm)) | ((blk >= 2) & (ii <= jm))
    eye4f = eye4.astype(F32)
    lane2 = lax.broadcasted_iota(jnp.int32, (C, 2 * C), 1)
    lo2 = lane2 < C

    i2 = lax.broadcasted_iota(jnp.int32, (2 * C, 2 * C), 0)
    j2 = lax.broadcasted_iota(jnp.int32, (2 * C, 2 * C), 1)
    same = (i2 // C) == (j2 // C)
    for d in range(2):
        a = jnp.exp(jnp.full((1, 2 * C), alog_ref[d * h_a + h], F32))
        g = -a * _softplus(grow_ref[0, 0, d] + dtb_ref[d * h_a + h])
        tri = same & ((i2 <= j2) if d == 0 else (i2 >= j2))
        gc_s[d] = jnp.dot(g, tri.astype(F32), preferred_element_type=F32, precision=HIGHEST)

    def block_diag(x):
        return jnp.concatenate([jnp.where(blk == b, x, 0.0) for b in range(NB)], 0).astype(BF16)

    def pair_setup(p):
        r0 = pl.multiple_of(p * 2 * C, 2 * C)
        q2 = q_s[pl.ds(r0, 2 * C), :]
        k2 = k_s[pl.ds(r0, 2 * C), :]
        v2 = v_s[pl.ds(r0, 2 * C), :]
        k16 = k2.astype(BF16)
        gk = lax.dot_general(k16, k16, _NT, preferred_element_type=F32)
        gq = lax.dot_general(q2.astype(BF16), k16, _NT, preferred_element_type=F32)
        kk2 = jnp.where(lo2, gk[:C], gk[C:])
        qk2 = jnp.where(lo2, gq[:C], gq[C:])
        kk4 = jnp.concatenate([kk2, kk2], 1)
        qk4 = jnp.concatenate([qk2, qk2], 1)
        grow = [gc_s[d, pl.ds(p, 1), :] for d in range(2)]
        rowb = jnp.concatenate([jnp.broadcast_to(g, (C, 2 * C)) for g in grow], 1)
        diag = jnp.where(eye4, rowb, 0.0)
        gcol, bcol, tot = [], [], []
        for b in range(NB):
            d, par = b // 2, b % 2
            half = diag[:, d * 2 * C:(d + 1) * 2 * C]
            gcol.append(jnp.sum(jnp.where(lo2 if par == 0 else ~lo2, half, 0.0), axis=1, keepdims=True))
            bcol.append(jax.nn.sigmoid(bcol_ref[0, 0, pl.ds(r0 + par * C, C), d:d + 1]))
            e = par * C + (C - 1 if d == 0 else 0)
            tot.append(grow[d][:, e:e + 1])

        def pack(cols):
            return jnp.where(blk == 0, cols[0], jnp.where(blk == 1, cols[1], jnp.where(blk == 2, cols[2], cols[3])))

        gcc4 = pack(gcol)
        dec4 = jnp.where(tri4, jnp.exp(jnp.where(tri4, gcc4 - rowb, 0.0)), 0.0)
        m0 = jnp.where(eye4, 0.0, -(kk4 * pack(bcol) * dec4))
        return m0, (p, q2, k2, v2, qk4 * dec4, gcol, bcol, tot)

    def pair_finish(tinv4, aux):
        p, q2, k2, v2, attn4, gcol, bcol, tot = aux
        eg = [jnp.exp(g) for g in gcol]
        ks = (k2[:C], k2[C:])
        vs = (v2[:C], v2[C:])
        qs = (q2[:C], q2[C:])
        rhs = jnp.concatenate([jnp.concatenate([vs[b % 2] * bcol[b], ks[b % 2] * (bcol[b] * eg[b])], 1)
                               for b in range(NB)], 0).astype(BF16)
        uw = jnp.dot(block_diag(tinv4), rhs, preferred_element_type=F32)
        for b in range(NB):
            d, par = b // 2, b % 2
            c = 2 * p + par
            u_s[d, c] = uw[b * C:(b + 1) * C, :DK_A]
            l1_s[d, c] = jnp.concatenate([uw[b * C:(b + 1) * C, DK_A:], qs[par] * eg[b]], 0).astype(BF16)
            gl_s[d, c] = jnp.broadcast_to(jnp.exp(tot[b]), (1, DK_A))
        for d in range(2):
            f = jnp.concatenate([jnp.exp(tot[2 * d + par] - gcol[2 * d + par]) for par in range(2)], 0)
            kdec_t = (k2 * f).T
            l2_s[d, p] = jnp.concatenate([attn4[:, d * 2 * C:(d + 1) * 2 * C], kdec_t], 0).astype(BF16)

    n_pairs = n_tot // 2
    group = max(g for g in (1, 2, 3) if n_pairs % g == 0)

    def group_terms(i, _):
        st = [pair_setup(i * group + g) for g in range(group)]
        m0s = [s[0] for s in st]
        rs = [jnp.dot(m0.astype(BF16), block_diag(m0), preferred_element_type=F32) for m0 in m0s]
        qms = [eye4f + m0 for m0 in m0s]
        for _ in range(4):
            qrs = [jnp.dot(jnp.concatenate([qm, r], 0).astype(BF16), block_diag(r), preferred_element_type=F32)
                   for qm, r in zip(qms, rs)]
            qms = [qm + qr[:C] for qm, qr in zip(qms, qrs)]
            rs = [qr[C:] for qr in qrs]
        tinvs = [qm + jnp.dot(qm.astype(BF16), block_diag(r), preferred_element_type=F32)
                 for qm, r in zip(qms, rs)]
        for tinv4, s in zip(tinvs, st):
            pair_finish(tinv4, s[1])
        return 0

    lax.fori_loop(0, n_pairs // group, group_terms, 0)

    o_s[...] = jnp.zeros_like(o_s)

    def scan_step(i, carry):
        c_b = jnp.where(i < n_ctx, n_ctx - 1 - i, n_ctx + n_tot - 1 - i)
        new = []
        row_half = lax.broadcasted_iota(jnp.int32, (2 * C, DK_A), 0) // C
        for d, c, s in ((0, i, carry[0]), (1, c_b, carry[1])):
            r1 = jnp.dot(l1_s[d, c], s.astype(BF16), preferred_element_type=F32)
            v_new = u_s[d, c] - r1[:C]
            keep = row_half == c % 2
            v_ext = jnp.where(keep, jnp.concatenate([v_new, v_new], 0), 0.0).astype(BF16)
            r2 = jnp.dot(l2_s[d, c // 2], v_ext, preferred_element_type=F32)
            rows = pl.ds(pl.multiple_of(c * C, C), C)
            o_s[rows, :] = o_s[rows, :] + r1[C:] + r2[:C]
            new.append(s * gl_s[d, c] + r2[C:])
        return tuple(new)

    zero = jnp.zeros((DK_A, DK_A), F32)
    lax.fori_loop(0, n_tot, scan_step, (zero, zero))

    def finish(z_ref, o_ref, off):
        n = z_ref.shape[1]
        o = o_s[off:off + n, :]
        o = o * lax.rsqrt(jnp.mean(o * o, -1, keepdims=True) + EPS) * nw_ref[...]
        z = z_ref[0].astype(F32)
        o_ref[0] = (o * (z * jax.nn.sigmoid(z))).astype(o_ref.dtype)

    finish(zc_ref, oc_ref, 0)
    finish(zl_ref, ol_ref, lc)


def _gdn_call(p_c, p_l, ab_c, ab_l, conv_w, a_log, dt_bias, norm_w, h_a):
    B, L, _ = p_l.shape
    lc = p_c.shape[1]
    t = lc + L
    n_tot, n_ctx = t // CHUNK, lc // CHUNK
    d_a = h_a * DK_A
    ab = jnp.concatenate([ab_c, ab_l], 1)[..., :4 * h_a].reshape(B, t, 2, 2, h_a)
    assert n_tot % 2 == 0
    g_row = ab[:, :, :, 0].transpose(0, 3, 2, 1).reshape(B, h_a, 2, n_tot // 2, 2 * CHUNK)
    b_col = ab[:, :, :, 1].transpose(0, 3, 1, 2)

    def col(k, n):
        return pl.BlockSpec((1, n, DK_A), lambda b, h, *_: (b, 0, k * h_a + h))

    def tap(k):
        return pl.BlockSpec((CONV_W, DK_A), lambda b, h, *_: (0, k * h_a + h))

    grid_spec = pltpu.PrefetchScalarGridSpec(
        num_scalar_prefetch=2,
        grid=(B, h_a),
        in_specs=[col(0, lc), col(1, lc), col(2, lc), col(3, lc),
                  col(0, L), col(1, L), col(2, L), col(3, L),
                  tap(0), tap(1), tap(2),
                  pl.BlockSpec((1, 1, 2, n_tot // 2, 2 * CHUNK), lambda b, h, *_: (b, h, 0, 0, 0)),
                  pl.BlockSpec((1, 1, t, 2), lambda b, h, *_: (b, h, 0, 0)),
                  pl.BlockSpec((1, DK_A), lambda b, h, *_: (0, 0))],
        out_specs=[pl.BlockSpec((1, lc, DK_A), lambda b, h, *_: (b, 0, h)),
                   pl.BlockSpec((1, L, DK_A), lambda b, h, *_: (b, 0, h))],
        scratch_shapes=[pltpu.VMEM((t, DK_A), F32), pltpu.VMEM((t, DK_A), F32), pltpu.VMEM((t, DK_A), F32),
                        pltpu.VMEM((t, DK_A), F32),
                        pltpu.VMEM((2, n_tot, CHUNK, DK_A), F32),
                        pltpu.VMEM((2, n_tot, 2 * CHUNK, DK_A), BF16),
                        pltpu.VMEM((2, n_tot // 2, CHUNK + DK_A, 2 * CHUNK), BF16),
                        pltpu.VMEM((2, n_tot, 1, DK_A), F32),
                        pltpu.VMEM((2, n_tot // 2, 2 * CHUNK), F32)],
    )
    return pl.pallas_call(
        functools.partial(_gdn_kernel, h_a=h_a, n_ctx=n_ctx),
        grid_spec=grid_spec,
        out_shape=[jax.ShapeDtypeStruct((B, lc, d_a), BF16), jax.ShapeDtypeStruct((B, L, d_a), BF16)],
        compiler_params=_params(("parallel", "parallel"), 48),
        name="gdn",
    )(a_log.reshape(-1), dt_bias.reshape(-1), p_c, p_c, p_c, p_c, p_l, p_l, p_l, p_l,
      conv_w, conv_w, conv_w, g_row, b_col, norm_w.reshape(1, DK_A))


_NT = (((1,), (1,)), ((), ()))


def _softmax_rows(s, sink=None):
    m = jnp.max(s, -1, keepdims=True)
    if sink is not None:
        m = jnp.maximum(m, sink)
    e = jnp.exp(s - m)
    den = jnp.sum(e, -1, keepdims=True)
    if sink is not None:
        den = den + jnp.exp(sink - m)
    return (e * (1.0 / den)).astype(BF16)


def _gqa_kernel(sink_ref, q_ref, kvl_ref, kvc_ref, cos_ref, sa_ref, sb_ref, o_ref, *, n_heads, latent):
    G = n_heads // KV_B
    P = 2 * DH
    n = pl.program_id(1)
    seq = kvl_ref.shape[1]
    lc = kvc_ref.shape[1]

    def rope(x, r0):
        reps = x.shape[1] // P
        rows = pl.ds(pl.multiple_of(r0, QBLK), QBLK)
        c, a, b = [jnp.concatenate([t[rows, :]] * reps, 1) if reps > 1 else t[rows, :]
                   for t in (cos_ref, sa_ref, sb_ref)]
        w = x.shape[1]
        return x * c + pltpu.roll(x, w - DH // 4, 1) * a + pltpu.roll(x, DH // 4, 1) * b

    q = q_ref[0].astype(F32) * DH ** -0.5
    kc = kvc_ref[0][:, :P].astype(F32)
    vc = kvc_ref[0][:, P:].astype(F32)
    if latent:
        q = rope(q, n * QBLK)
        ks, vs = [], []
        for o in (-1, 0, 1):
            s0 = jnp.clip((n + o) * QBLK, 0, seq - QBLK)
            kv = kvl_ref[0, pl.ds(pl.multiple_of(s0, QBLK), QBLK), :].astype(F32)
            ks.append(rope(kv[:, :P], s0))
            vs.append(kv[:, P:])
        k = jnp.concatenate(ks + [kc], 0)
        v = jnp.concatenate(vs + [vc], 0)
        nk = 3 * QBLK + lc
        qi = lax.broadcasted_iota(jnp.int32, (QBLK, nk), 0)
        kj = lax.broadcasted_iota(jnp.int32, (QBLK, nk), 1)
        kpos = n * QBLK - QBLK + kj
        rel = kj - QBLK - qi
        valid = ((rel >= -WIN) & (rel <= WIN) & (kpos >= 0) & (kpos < seq)) | (kj >= 3 * QBLK)
    else:
        k, v = kc, vc
        valid = None
    q = q.astype(BF16)
    k_sw = pltpu.roll(k, DH, 1).astype(BF16)
    v_sw = pltpu.roll(v, DH, 1).astype(BF16)
    k = k.astype(BF16)
    v = v.astype(BF16)
    lane = lax.broadcasted_iota(jnp.int32, (QBLK, P), 1)
    zero = jnp.zeros((QBLK, P), BF16)
    scores = []
    for h in range(n_heads):
        qp = q[:, (h // 2) * P:(h // 2 + 1) * P]
        qh = jnp.where(lane < DH, qp, zero) if h % 2 == 0 else jnp.where(lane >= DH, qp, zero)
        straight = (h // G == h % 2)
        scores.append(lax.dot_general(qh, k if straight else k_sw, _NT, preferred_element_type=F32))
    probs = []
    for h, s in enumerate(scores):
        if valid is not None:
            s = jnp.where(valid, s, NEG_INF)
        probs.append(_softmax_rows(s, sink_ref[h]))
    outs = [jnp.dot(pr, v if (h // G == h % 2) else v_sw, preferred_element_type=F32) for h, pr in enumerate(probs)]
    for p in range(n_heads // 2):
        o_ref[0, :, p * P:(p + 1) * P] = jnp.where(lane < DH, outs[2 * p], outs[2 * p + 1]).astype(o_ref.dtype)


def _gqa_call(p_q, p_l, p_c, sink, tables, d_model, d_b, latent):
    B, nq, _ = p_q.shape
    L, lc = p_l.shape[1], p_c.shape[1]
    kvw = 2 * KV_B * DH
    n_heads = d_b // DH
    assert (n_heads // KV_B) % 2 == 0
    grid_spec = pltpu.PrefetchScalarGridSpec(
        num_scalar_prefetch=1,
        grid=(B, nq // QBLK),
        in_specs=[pl.BlockSpec((1, QBLK, d_b), lambda b, n, *_: (b, n, 2 * d_model // d_b)),
                  pl.BlockSpec((1, L, kvw), lambda b, n, *_: (b, 0, 3 * d_model // kvw)),
                  pl.BlockSpec((1, lc, kvw), lambda b, n, *_: (b, 0, 3 * d_model // kvw)),
                  pl.BlockSpec((L, 2 * DH), lambda b, n, *_: (0, 0)),
                  pl.BlockSpec((L, 2 * DH), lambda b, n, *_: (0, 0)),
                  pl.BlockSpec((L, 2 * DH), lambda b, n, *_: (0, 0))],
        out_specs=pl.BlockSpec((1, QBLK, d_b), lambda b, n, *_: (b, n, 0)),
    )
    return pl.pallas_call(
        functools.partial(_gqa_kernel, n_heads=n_heads, latent=latent),
        grid_spec=grid_spec,
        out_shape=jax.ShapeDtypeStruct((B, nq, d_b), BF16),
        compiler_params=_params(("parallel", "parallel"), 48),
        name="window_gqa" if latent else "ctx_gqa",
    )(sink, p_q, p_l, p_c, *tables)


def _rope_tables(n_tokens):
    t = jnp.arange(n_tokens)
    row = (t // GRID_W).astype(F32)
    col = (t % GRID_W).astype(F32)
    half = DH // 2
    inv = ROPE_BASE ** (-jnp.arange(0, half, 2, dtype=F32) / half)
    ang_r = row[:, None] * inv[None]
    ang_c = col[:, None] * inv[None]
    ang = jnp.concatenate([ang_r, ang_r, ang_c, ang_c], -1)
    cos, sin = jnp.cos(ang), jnp.sin(ang)
    first = (jnp.arange(DH) % half) < half // 2
    sa = jnp.where(first, -sin, 0.0)
    sb = jnp.where(first, 0.0, sin)
    return tuple(jnp.concatenate([u, u], -1) for u in (cos, sa, sb))


def _na_kernel(q_ref, k_ref, v_ref, kc_ref, vc_ref, tbl_ref, o_ref, *, n_heads, rows, latent):
    P = 2 * DH
    r = pl.program_id(1)
    nq = q_ref.shape[1]
    nb = KH_MAX * GRID_W
    q = (q_ref[0].astype(F32) * DH ** -0.5).astype(BF16)
    lane = lax.broadcasted_iota(jnp.int32, (nq, P), 1)
    zero = jnp.zeros((nq, P), BF16)
    if latent:
        r0 = jnp.clip(r - KH_MAX // 2, 0, rows - KH_MAX)
        band = pl.ds(pl.multiple_of(r0 * GRID_W, GRID_W), nb)
        d0 = r0 - r + (KH_MAX - 1)
    scores, vals = [], []
    for p in range(n_heads // 2):
        cols = slice(p * P, (p + 1) * P)
        qp = q[:, cols]
        kc = kc_ref[0, :, cols]
        vc = vc_ref[0, :, cols]
        if latent:
            kk = jnp.concatenate([k_ref[0, band, cols], kc], 0)
            vals.append(jnp.concatenate([v_ref[0, band, cols], vc], 0))
        else:
            kk = kc
            vals.append(vc)
        for half in range(2):
            qh = jnp.where(lane < DH, qp, zero) if half == 0 else jnp.where(lane >= DH, qp, zero)
            scores.append(lax.dot_general(qh, kk, _NT, preferred_element_type=F32))
    probs = []
    for h, s in enumerate(scores):
        if latent:
            bias = jnp.concatenate([tbl_ref[h, d0 + 2 * t] for t in range(KH_MAX // 2)], 1)
            s = jnp.concatenate([s[:, :nb] + bias, s[:, nb:]], 1)
        probs.append(_softmax_rows(s))
    outs = [jnp.dot(pr, vals[h // 2], preferred_element_type=F32) for h, pr in enumerate(probs)]
    for p in range(n_heads // 2):
        o_ref[0, :, p * P:(p + 1) * P] = jnp.where(lane < DH, outs[2 * p], outs[2 * p + 1]).astype(o_ref.dtype)


def _na_bias_table(rpb):
    qc = jnp.arange(GRID_W)[:, None]
    kc = jnp.arange(GRID_W)[None, :]
    dc = jnp.clip(kc - qc + (KW - 1), 0, 2 * KW - 2)
    q_cs = jnp.clip(qc - KW // 2, 0, GRID_W - KW)
    in_win = (kc >= q_cs) & (kc < q_cs + KW)
    t = jnp.where(in_win[None, None], rpb[:, :, dc], NEG_INF)
    return jnp.concatenate([t[:, :-1], t[:, 1:]], -1)


def _na_call(p_q, p_l, p_c, tbl, d_model, d_c, latent):
    B, nq, _ = p_q.shape
    L, lc = p_l.shape[1], p_c.shape[1]
    n_heads = d_c // DH
    rows = L // GRID_W
    assert rows >= KH_MAX and n_heads % 2 == 0
    cb = 2 * d_model // d_c
    qb = GRID_W
    return pl.pallas_call(
        functools.partial(_na_kernel, n_heads=n_heads, rows=rows, latent=latent),
        grid=(B, nq // qb),
        in_specs=[pl.BlockSpec((1, qb, d_c), lambda b, r: (b, r, cb + 1)),
                  pl.BlockSpec((1, L, d_c), lambda b, r: (b, 0, cb + 2)),
                  pl.BlockSpec((1, L, d_c), lambda b, r: (b, 0, cb + 3)),
                  pl.BlockSpec((1, lc, d_c), lambda b, r: (b, 0, cb + 2)),
                  pl.BlockSpec((1, lc, d_c), lambda b, r: (b, 0, cb + 3)),
                  pl.BlockSpec(tbl.shape, lambda b, r: (0, 0, 0, 0))],
        out_specs=pl.BlockSpec((1, qb, d_c), lambda b, r: (b, r, 0)),
        out_shape=jax.ShapeDtypeStruct((B, nq, d_c), BF16),
        compiler_params=_params(("parallel", "parallel"), 48),
        name="nbr_attn" if latent else "ctx_attn",
    )(p_q, p_l, p_l, p_c, p_c, tbl)


def _l2norm(u):
    return u * lax.rsqrt(jnp.sum(u * u, -1, keepdims=True) + EPS)


def _short_conv(u, w):
    pad = CONV_W // 2
    L = u.shape[1]
    up = jnp.pad(u, ((0, 0), (pad, pad), (0, 0)))
    return sum(up[:, i:i + L] * w[i] for i in range(CONV_W))


def _gdn_prepare(qkv, ab, conv_w, a_log, dt_bias, h_a):
    B, L, _ = qkv.shape
    qkv = jax.nn.silu(_short_conv(qkv, conv_w))
    q, k, v = jnp.split(qkv, 3, -1)
    q = _l2norm(q.reshape(B, L, h_a, DK_A))
    k = _l2norm(k.reshape(B, L, h_a, DK_A))
    v = v.reshape(B, L, h_a, DK_A)
    ab = ab.reshape(B, L, 2, 2, h_a)
    g = -jnp.exp(a_log) * jax.nn.softplus(ab[:, :, :, 0] + dt_bias)
    beta = jax.nn.sigmoid(ab[:, :, :, 1])
    return q, k, v, g, beta


def _gdn_chunked(q, k, v, g, beta, s0):
    B, L, H, dk = k.shape
    dv = v.shape[-1]
    n = L // CHUNK

    def chunks(u):
        return u.reshape(B, n, CHUNK, H, -1).transpose(0, 3, 1, 2, 4)

    qc, kc, vc = chunks(q * dk ** -0.5), chunks(k), chunks(v)
    gc = g.reshape(B, n, CHUNK, H).transpose(0, 3, 1, 2)
    bc = beta.reshape(B, n, CHUNK, H).transpose(0, 3, 1, 2)
    gcum = jnp.cumsum(gc, -1)
    tril = jnp.tril(jnp.ones((CHUNK, CHUNK), bool))
    strict = jnp.tril(jnp.ones((CHUNK, CHUNK), bool), -1)
    diff = gcum[..., :, None] - gcum[..., None, :]
    decay = jnp.where(tril, jnp.exp(jnp.where(tril, diff, 0.0)), 0.0)
    kb = kc * bc[..., None]
    vb = vc * bc[..., None]
    a = jnp.where(strict, jnp.einsum('bhnid,bhnjd->bhnij', kb, kc) * decay, 0.0)
    eye = jnp.eye(CHUNK, dtype=F32)
    tinv = lax.linalg.triangular_solve(eye + a, jnp.broadcast_to(eye, a.shape),
                                       left_side=True, lower=True, unit_diagonal=True)
    u = tinv @ vb
    w = tinv @ (kb * jnp.exp(gcum)[..., None])
    attn = jnp.einsum('bhnid,bhnjd->bhnij', qc, kc) * decay
    qg = qc * jnp.exp(gcum)[..., None]
    kdec = kc * jnp.exp(gcum[..., -1:] - gcum)[..., None]
    glast = jnp.exp(gcum[..., -1])
    xs = tuple(jnp.moveaxis(t, 2, 0) for t in (u, w, attn, qg, kdec, glast))

    def step(s, inp):
        u_i, w_i, attn_i, qg_i, kdec_i, gl_i = inp
        v_new = u_i - w_i @ s
        o_i = qg_i @ s + attn_i @ v_new
        s = s * gl_i[..., None, None] + jnp.einsum('bhcd,bhce->bhde', kdec_i, v_new)
        return s, o_i

    s_final, o = lax.scan(step, s0, xs)
    o = jnp.moveaxis(o, 0, 2).transpose(0, 2, 3, 1, 4).reshape(B, L, H, dv)
    return o, s_final


def _gdn_output(o, z, norm_w, h_a):
    B, L = z.shape[:2]
    o = o * lax.rsqrt(jnp.mean(o * o, -1, keepdims=True) + EPS) * norm_w
    zg = jax.nn.silu(z.reshape(B, L, h_a, DK_A))
    return (o * zg).reshape(B, L, h_a * DK_A)


def _gated_deltanet(qkv_c, ab_c, z_c, qkv_l, ab_l, z_l, conv_w, a_log, dt_bias, norm_w, ctx_out, h_a):
    qc, kc, vc, gc, bc = _gdn_prepare(qkv_c, ab_c, conv_w, a_log, dt_bias, h_a)
    ql, kl, vl, gl, bl = _gdn_prepare(qkv_l, ab_l, conv_w, a_log, dt_bias, h_a)
    B = ql.shape[0]
    s0 = jnp.zeros((B, h_a, DK_A, DK_A), F32)
    o_c_sum, o_l_sum = 0.0, 0.0
    for d in range(2):
        seq_c = (qc, kc, vc, gc[:, :, d], bc[:, :, d])
        seq_l = (ql, kl, vl, gl[:, :, d], bl[:, :, d])
        if d == 1:
            seq_c = tuple(jnp.flip(t, 1) for t in seq_c)
            seq_l = tuple(jnp.flip(t, 1) for t in seq_l)
        o_c, s_c = _gdn_chunked(*seq_c, s0)
        o_l, _ = _gdn_chunked(*seq_l, s_c)
        if d == 1:
            o_c, o_l = jnp.flip(o_c, 1), jnp.flip(o_l, 1)
        o_c_sum = o_c_sum + o_c
        o_l_sum = o_l_sum + o_l
    o_lat = _gdn_output(o_l_sum, z_l, norm_w, h_a)
    o_ctx = _gdn_output(o_c_sum, z_c, norm_w, h_a) if ctx_out else None
    return o_ctx, o_lat


def _rope_tables_ref(n_tokens, head_dim):
    t = jnp.arange(n_tokens)
    row = (t // GRID_W).astype(F32)
    col = (t % GRID_W).astype(F32)
    half = head_dim // 2
    inv = ROPE_BASE ** (-jnp.arange(0, half, 2, dtype=F32) / half)
    ang_r = row[:, None] * inv[None]
    ang_c = col[:, None] * inv[None]
    ang = jnp.concatenate([ang_r, ang_r, ang_c, ang_c], -1)
    return jnp.cos(ang), jnp.sin(ang)


def _apply_rope(x, cos, sin):
    half = x.shape[-1] // 2

    def rot(u):
        u1, u2 = jnp.split(u, 2, -1)
        return jnp.concatenate([-u2, u1], -1)

    xr = jnp.concatenate([rot(x[..., :half]), rot(x[..., half:])], -1)
    return x * cos[None, :, None, :] + xr * sin[None, :, None, :]


def _softmax_with_sink(s, sink):
    sk = jnp.broadcast_to(sink, s.shape[:-1] + (1,))
    p = jax.nn.softmax(jnp.concatenate([s, sk], -1), -1)
    return p[..., :-1]


def _window_gqa(q_c, kv_c, q_l, kv_l, sink, cos, sin, ctx_out):
    B, L = q_l.shape[:2]
    Lc = q_c.shape[1]
    h_b = q_l.shape[-1] // DH
    G = h_b // KV_B
    n = L // QBLK
    J = 3 * QBLK
    scale = DH ** -0.5
    k_l, v_l = [u.reshape(B, L, KV_B, DH) for u in jnp.split(kv_l, 2, -1)]
    k_c, v_c = [u.reshape(B, Lc, KV_B, DH) for u in jnp.split(kv_c, 2, -1)]
    q_l = _apply_rope(q_l.reshape(B, L, h_b, DH), cos, sin)
    k_l = _apply_rope(k_l, cos, sin)
    sink = sink.reshape(KV_B, G, 1, 1)

    def band(u):
        up = jnp.pad(u, ((0, 0), (QBLK, QBLK), (0, 0), (0, 0))).reshape(B, n + 2, QBLK, KV_B, DH)
        return jnp.concatenate([up[:, :-2], up[:, 1:-1], up[:, 2:]], axis=2)

    k_w, v_w = band(k_l), band(v_l)
    qb = q_l.reshape(B, n, QBLK, KV_B, G, DH)
    qi = jnp.arange(QBLK)[:, None]
    kj = jnp.arange(J)[None]
    kpos = (jnp.arange(n) * QBLK)[:, None, None] - QBLK + kj[None]
    in_win = (jnp.abs(kj - QBLK - qi)[None] <= WIN) & (kpos >= 0) & (kpos < L)
    s_w = jnp.einsum('bnqkgd,bnjkd->bnkgqj', qb, k_w) * scale
    s_w = jnp.where(in_win[None, :, None, None], s_w, NEG_INF)
    s_x = jnp.einsum('bnqkgd,bckd->bnkgqc', qb, k_c) * scale
    p = _softmax_with_sink(jnp.concatenate([s_w, s_x], -1), sink)
    o = (jnp.einsum('bnkgqj,bnjkd->bnqkgd', p[..., :J], v_w)
         + jnp.einsum('bnkgqc,bckd->bnqkgd', p[..., J:], v_c))
    o_lat = o.reshape(B, L, h_b * DH)
    o_ctx = None
    if ctx_out:
        qc = q_c.reshape(B, Lc, KV_B, G, DH)
        s_c = jnp.einsum('bqkgd,bckd->bkgqc', qc, k_c) * scale
        p_c = _softmax_with_sink(s_c, sink)
        o_ctx = jnp.einsum('bkgqc,bckd->bqkgd', p_c, v_c).reshape(B, Lc, h_b * DH)
    return o_ctx, o_lat


def _neighbourhood_attn(q_c, kv_c, q_l, kv_l, rpb, ctx_out):
    B, L = q_l.shape[:2]
    Lc = q_c.shape[1]
    h_c = q_l.shape[-1] // DH
    rows = L // GRID_W
    kh = min(KH_MAX, rows)
    ncb = GRID_W // NA_QCOLS
    m = kh * NA_KCOLS
    scale = DH ** -0.5
    k_l, v_l = [u.reshape(B, L, h_c, DH) for u in jnp.split(kv_l, 2, -1)]
    k_c, v_c = [u.reshape(B, Lc, h_c, DH) for u in jnp.split(kv_c, 2, -1)]
    r = jnp.arange(rows)
    row_idx = jnp.clip(r - kh // 2, 0, rows - kh)[:, None] + jnp.arange(kh)[None]
    cb = jnp.arange(ncb)
    col_idx = (jnp.clip(cb * NA_QCOLS - KW // 2, 0, GRID_W - NA_KCOLS)[:, None]
               + jnp.arange(NA_KCOLS)[None])

    def gather_band(u):
        u = u.reshape(B, rows, GRID_W, h_c, DH)[:, row_idx]
        u = u[:, :, :, col_idx]
        return u.transpose(0, 1, 3, 2, 4, 5, 6).reshape(B, rows, ncb, m, h_c, DH)

    k_n, v_n = gather_band(k_l), gather_band(v_l)
    qn = q_l.reshape(B, rows, ncb, NA_QCOLS, h_c, DH)
    q_col = cb[:, None] * NA_QCOLS + jnp.arange(NA_QCOLS)[None]
    q_cs = jnp.clip(q_col - KW // 2, 0, GRID_W - KW)
    key_col = jnp.tile(col_idx, (1, kh))
    key_row = jnp.repeat(row_idx, NA_KCOLS, axis=1)
    in_win = ((key_col[:, None, :] >= q_cs[:, :, None])
              & (key_col[:, None, :] < q_cs[:, :, None] + KW))
    dr = key_row - r[:, None] + (KH_MAX - 1)
    dc = jnp.clip(key_col[:, None, :] - q_col[:, :, None] + (KW - 1), 0, 2 * KW - 2)
    bias = rpb[:, dr[:, None, None, :], dc[None]]
    s_nb = jnp.einsum('brcqhd,brcmhd->bhrcqm', qn, k_n) * scale + bias
    s_nb = jnp.where(in_win, s_nb, NEG_INF)
    s_x = jnp.einsum('brcqhd,bkhd->bhrcqk', qn, k_c) * scale
    p = jax.nn.softmax(jnp.concatenate([s_nb, s_x], -1), -1)
    o = (jnp.einsum('bhrcqm,brcmhd->brcqhd', p[..., :m], v_n)
         + jnp.einsum('bhrcqk,bkhd->brcqhd', p[..., m:], v_c))
    o_lat = o.reshape(B, L, h_c * DH)
    o_ctx = None
    if ctx_out:
        qc = q_c.reshape(B, Lc, h_c, DH)
        s_c = jnp.einsum('bqhd,bkhd->bhqk', qc, k_c) * scale
        p_c = jax.nn.softmax(s_c, -1)
        o_ctx = jnp.einsum('bhqk,bkhd->bqhd', p_c, v_c).reshape(B, Lc, h_c * DH)
    return o_ctx, o_lat


def _route(logits):
    T = logits.shape[0]
    g_logits = logits[:, :N_GROUPS]
    g_prob = jax.nn.softmax(g_logits, -1)
    g_sel = jnp.argmax(g_logits, -1)
    e_logits = logits[:, N_GROUPS:N_GROUPS + N_EXPERTS].reshape(T, N_GROUPS, E_PER_GROUP)
    e_in = jnp.take_along_axis(e_logits, g_sel[:, None, None], 1)[:, 0]
    top_v, top_i = lax.top_k(e_in, TOP_K)
    wts = jax.nn.softmax(top_v, -1) * jnp.take_along_axis(g_prob, g_sel[:, None], 1)
    expert = g_sel[:, None] * E_PER_GROUP + top_i
    return expert.astype(jnp.int32), wts


def _dispatch(expert, tm):
    T = expert.shape[0]
    S = T * TOP_K
    e_flat = expert.reshape(S)
    order = jnp.argsort(e_flat)
    e_sorted = e_flat[order]
    tok_sorted = order // TOP_K
    counts = jnp.bincount(e_flat, length=N_EXPERTS)
    starts = jnp.cumsum(counts) - counts
    padded = (counts + tm - 1) // tm * tm
    pends = jnp.cumsum(padded)
    pstarts = pends - padded
    dest = pstarts[e_sorted] + jnp.arange(S) - starts[e_sorted]
    n_tiles = -(-S // tm) + N_EXPERTS
    tile_expert = jnp.minimum(jnp.searchsorted(pends, jnp.arange(n_tiles) * tm, side='right'),
                              N_EXPERTS - 1).astype(jnp.int32)
    n_used = (pends[-1] // tm).astype(jnp.int32).reshape(1)
    pos = jnp.zeros((S,), jnp.int32).at[order].set(dest.astype(jnp.int32)).reshape(T, TOP_K)
    src_tok = jnp.zeros((n_tiles * tm,), jnp.int32).at[dest].set(tok_sorted.astype(jnp.int32))
    return tile_expert, n_used, pos, src_tok


def _moe(f, logits, w_gate, w_up, w_down):
    T, d = f.shape
    tm = MOE_TM
    ids, wts, cnt = _route_call(logits)
    expert, rank = ids[:, :TOP_K], ids[:, TOP_K:2 * TOP_K]
    counts = cnt[0, :N_EXPERTS].astype(jnp.int32)
    padded = (counts + tm - 1) // tm * tm
    pends = jnp.cumsum(padded)
    pstarts = pends - padded
    dest = pstarts[expert] + rank
    n_tiles = -(-T * TOP_K // tm) + N_EXPERTS
    tile_expert = jnp.minimum(jnp.sum(pends[None, :] <= (jnp.arange(n_tiles) * tm)[:, None], -1),
                              N_EXPERTS - 1).astype(jnp.int32)
    n_used = (pends[-1] // tm).astype(jnp.int32).reshape(1)
    tok = jnp.broadcast_to(jnp.arange(T, dtype=jnp.int32)[:, None], (T, TOP_K))
    src_tok = jnp.zeros((n_tiles * tm,), jnp.int32).at[dest.reshape(-1)].set(tok.reshape(-1))
    out = _moe_call(tile_expert, n_used, f[src_tok], w_gate, w_up, w_down, tm)
    return out[dest[:, 0]], out[dest[:, 1]], wts


def kernel(x, c, ctx, c_ctx, w_mod, b_mod, norm_mix, norm_ffn, w_in, conv_a, a_log, dt_bias, gdn_norm, sink_b, rpb_c, w_out, w_router_group, b_router_group, w_router_expert, b_router_expert, w_gate, w_up, w_down, norm_final):
    B, L, D = x.shape
    Lc = ctx.shape[1]
    depth = w_mod.shape[0]
    d_a = D // 2
    d_b = D // 4
    d_c = D - d_a - d_b
    h_a = d_a // DK_A
    n_ab = 4 * h_a
    kvb = 2 * KV_B * DH
    o_ab = 4 * d_a
    n_main = 4 * d_a + d_b + kvb + 3 * d_c

    cc = jnp.zeros((MOD_ROWS, D), F32).at[:B].set(c).at[B].set(c_ctx)
    mod = _mod_call(cc, w_mod, b_mod).reshape(depth, MOD_ROWS, N_MOD, D)
    rope = _rope_tables(L)

    tn_in = n_main
    for cand in (1280, 1024, 768, 640, 512, 384, 256, 128):
        if n_main % cand == 0:
            tn_in = cand
            break

    x_l, x_c = x, ctx
    for l in range(depth):
        last = l == depth - 1
        ctx_out = not last
        mod_l = mod[l, :B]
        mod_c = mod[l, B:B + 1]
        ml = [mod_l[:, i:i + 1] for i in range(N_MOD)]
        mc = [mod_c[:, i:i + 1] for i in range(N_MOD)]
        wl = w_in[l]
        o_qb = o_ab + n_ab
        o_kvb = o_qb + d_b
        o_qc = o_kvb + kvb
        w_main = jnp.concatenate([wl[:, :o_ab], wl[:, o_qb:o_kvb], wl[:, o_qc:], wl[:, o_kvb:o_qc]], -1).astype(BF16)
        w_ab = jnp.pad(wl[:, o_ab:o_qb], ((0, 0), (0, LANE - n_ab))).astype(BF16)
        gain_mix = norm_mix[l].reshape(1, D)
        p_l, ab_l = _inproj_call(x_l, ml[0], ml[1], gain_mix, w_main, w_ab, min(L, 1024), tn_in)
        p_c, ab_c = _inproj_call(x_c, mc[0], mc[1], gain_mix, w_main, w_ab, min(Lc, 1024), tn_in)

        oA_c, oA_l = _gdn_call(p_c, p_l, ab_c, ab_l, conv_a[l], a_log[l], dt_bias[l], gdn_norm[l], h_a)
        oB_l = _gqa_call(p_l, p_l, p_c, sink_b[l], rope, D, d_b, True)
        tbl = _na_bias_table(rpb_c[l])
        oC_l = _na_call(p_l, p_l, p_c, tbl, D, d_c, True)
        o_l = jnp.concatenate([oA_l, oB_l, oC_l], -1)
        if ctx_out:
            oB_c = _gqa_call(p_c, p_l, p_c, sink_b[l], rope, D, d_b, False)
            oC_c = _na_call(p_c, p_l, p_c, tbl, D, d_c, False)

        w_o = w_out[l].astype(BF16)
        w_r = jnp.pad(jnp.concatenate([w_router_group[l], w_router_expert[l]], -1),
                      ((0, 0), (0, ROUTER_COLS - N_GROUPS - N_EXPERTS)))
        w_r_hi = w_r.astype(BF16)
        w_r = jnp.concatenate([w_r_hi, (w_r - w_r_hi.astype(F32)).astype(BF16)], -1)
        b_r = jnp.pad(jnp.concatenate([b_router_group[l], b_router_expert[l]], -1),
                      (0, ROUTER_COLS - N_GROUPS - N_EXPERTS)).reshape(1, ROUTER_COLS)
        gain_ffn = norm_ffn[l].reshape(1, D)
        x_l, f_l, lg_l = _outproj_call(o_l, x_l, ml[2], ml[3], ml[4], gain_ffn, w_o, w_r, b_r, min(L, 256))
        wg, wu, wd = w_gate[l].astype(BF16), w_up[l].astype(BF16), w_down[l].astype(BF16)
        if ctx_out:
            o_c = jnp.concatenate([oA_c, oB_c, oC_c], -1)
            x_c, f_c, lg_c = _outproj_call(o_c, x_c, mc[2], mc[3], mc[4], gain_ffn, w_o, w_r, b_r, min(Lc, 256))
            f_all = jnp.concatenate([f_c.reshape(B * Lc, D), f_l.reshape(B * L, D)], 0)
            lg_all = jnp.concatenate([lg_c.reshape(B * Lc, ROUTER_COLS), lg_l.reshape(B * L, ROUTER_COLS)], 0)
            ys = _moe(f_all, lg_all, wg, wu, wd)
            y_c = [u[:B * Lc].reshape(B, Lc, -1) for u in ys]
            y_l = [u[B * Lc:].reshape(B, L, -1) for u in ys]
            x_c = _resid_call(x_c, *y_c, mc[5], gain_ffn, min(Lc, 256), False)
        else:
            y_l = [u.reshape(B, L, -1) for u in _moe(f_l.reshape(B * L, D), lg_l.reshape(B * L, ROUTER_COLS), wg, wu, wd)]
        x_l = _resid_call(x_l, *y_l, ml[5], norm_final.reshape(1, D), min(L, 256), last)
    return x_l
```

```python
import functools
import math

import jax
import jax.numpy as jnp
from jax import lax
from jax.experimental import pallas as pl
from jax.experimental.pallas import tpu as pltpu

F32 = jnp.float32
BF16 = jnp.bfloat16
HIGHEST = lax.Precision.HIGHEST

EPS = 1e-6
NEG_INF = -1e30
N_MOD = 6
GRID_W = 64
DH = 64
DK_A = 128
CONV_W = 5
CHUNK = 64
KV_B = 2
WIN = 128
QBLK = 128
ROPE_BASE = 10000.0
KH_MAX = 8
KW = 16
NA_QCOLS = 16
NA_KCOLS = 32
N_GROUPS = 4
E_PER_GROUP = 8
N_EXPERTS = N_GROUPS * E_PER_GROUP
TOP_K = 2
LANE = 128
MOD_ROWS = 16
ROUTER_COLS = LANE
MOE_TM = 256


def _params(sem, vmem_mb):
    return pltpu.CompilerParams(dimension_semantics=sem, vmem_limit_bytes=vmem_mb << 20)


def _mod_kernel(c_ref, w_ref, b_ref, o_ref):
    c = c_ref[...]
    s = c * jax.nn.sigmoid(c)
    o_ref[0] = jnp.dot(s, w_ref[0], preferred_element_type=F32, precision=HIGHEST) + b_ref[0]


def _mod_call(cc, w_mod, b_mod):
    depth, d, n = w_mod.shape
    tn = min(n, 1024)
    return pl.pallas_call(
        _mod_kernel,
        grid=(depth, n // tn),
        in_specs=[pl.BlockSpec((MOD_ROWS, d), lambda l, j: (0, 0)),
                  pl.BlockSpec((1, d, tn), lambda l, j: (l, 0, j)),
                  pl.BlockSpec((1, 1, tn), lambda l, j: (l, 0, j))],
        out_specs=pl.BlockSpec((1, MOD_ROWS, tn), lambda l, j: (l, 0, j)),
        out_shape=jax.ShapeDtypeStruct((depth, MOD_ROWS, n), F32),
        compiler_params=_params(("parallel", "parallel"), 48),
        name="mod_proj",
    )(cc, w_mod, b_mod.reshape(depth, 1, n))


def _winprep_kernel(w_ref, m_ref, ab_ref, *, bounds):
    o_ab, o_qb, o_kvb, o_qc = bounds
    x = w_ref[0]
    m_ref[0] = jnp.concatenate([x[:, :o_ab], x[:, o_qb:o_kvb], x[:, o_qc:], x[:, o_kvb:o_qc]], 1).astype(BF16)
    ab = x[:, o_ab:o_qb]
    ab_ref[0] = jnp.concatenate([ab, jnp.zeros((x.shape[0], LANE - ab.shape[1]), F32)], 1).astype(BF16)


def _winprep_call(w_in, bounds):
    depth, d, n = w_in.shape
    n_main = n - (bounds[1] - bounds[0])
    rt = min(d, 256)
    return pl.pallas_call(
        functools.partial(_winprep_kernel, bounds=bounds),
        grid=(depth, d // rt),
        in_specs=[pl.BlockSpec((1, rt, n), lambda l, i: (l, i, 0))],
        out_specs=[pl.BlockSpec((1, rt, n_main), lambda l, i: (l, i, 0)),
                   pl.BlockSpec((1, rt, LANE), lambda l, i: (l, i, 0))],
        out_shape=[jax.ShapeDtypeStruct((depth, d, n_main), BF16), jax.ShapeDtypeStruct((depth, d, LANE), BF16)],
        compiler_params=_params(("parallel", "parallel"), 48),
        name="w_in_prep",
    )(w_in)


def _cast_chunks(w, n):
    rows = w.shape[0] * w.shape[1]
    if rows % n or (rows // n) % 16:
        return None
    return w.reshape(n, rows // n, w.shape[2])


def _cast_specs(chunks, n_inner):
    blk = (1,) + chunks.shape[1:]
    spec = pl.BlockSpec(blk, lambda b, j, *_: (b * n_inner + j, 0, 0))
    return spec, jax.ShapeDtypeStruct(chunks.shape, BF16)


def _inproj_kernel(x_ref, shift_ref, scale_ref, g_ref, w_ref, wab_ref, o_ref, ab_ref, h_ref):
    @pl.when(pl.program_id(2) == 0)
    def _():
        x = x_ref[0]
        y = x * lax.rsqrt(jnp.mean(x * x, -1, keepdims=True) + EPS) * g_ref[...]
        h = (y * (1.0 + scale_ref[0]) + shift_ref[0]).astype(BF16)
        h_ref[...] = h
        ab_ref[0] = jnp.dot(h, wab_ref[0], preferred_element_type=F32)

    o_ref[0] = jnp.dot(h_ref[...], w_ref[0], preferred_element_type=F32).astype(o_ref.dtype)


def _inproj_call(x, shift, scale, gain, w_main, w_ab, layer, tm, tn):
    bn, seq, d = x.shape
    n = w_main.shape[2]
    per_batch = shift.shape[0] > 1
    mod_map = (lambda b, i, j: (b, 0, 0)) if per_batch else (lambda b, i, j: (0, 0, 0))
    return pl.pallas_call(
        _inproj_kernel,
        grid=(bn, seq // tm, n // tn),
        in_specs=[pl.BlockSpec((1, tm, d), lambda b, i, j: (b, i, 0)),
                  pl.BlockSpec((1, 1, d), mod_map),
                  pl.BlockSpec((1, 1, d), mod_map),
                  pl.BlockSpec((1, d), lambda b, i, j: (0, 0)),
                  pl.BlockSpec((1, d, tn), lambda b, i, j: (layer, 0, j)),
                  pl.BlockSpec((1, d, LANE), lambda b, i, j: (layer, 0, 0))],
        out_specs=[pl.BlockSpec((1, tm, tn), lambda b, i, j: (b, i, j)),
                   pl.BlockSpec((1, tm, LANE), lambda b, i, j: (b, i, 0))],
        out_shape=[jax.ShapeDtypeStruct((bn, seq, n), BF16),
                   jax.ShapeDtypeStruct((bn, seq, LANE), F32)],
        scratch_shapes=[pltpu.VMEM((tm, d), BF16)],
        compiler_params=_params(("parallel", "parallel", "arbitrary"), 56),
        name="in_proj",
    )(x, shift, scale, gain, w_main, w_ab)


def _outproj_kernel(oa_ref, ob_ref, oc_ref, x_ref, gate_ref, shift_ref, scale_ref, g_ref, w_ref, wr_ref, br_ref,
                    xn_ref, f_ref, lg_ref):
    acc, r0 = None, 0
    for o_ref in (oa_ref, ob_ref, oc_ref):
        r1 = r0 + o_ref.shape[2]
        part = jnp.dot(o_ref[0], w_ref[r0:r1, :], preferred_element_type=F32)
        acc = part if acc is None else acc + part
        r0 = r1
    xn = x_ref[0] + gate_ref[0] * acc
    xn_ref[0] = xn
    y = xn * lax.rsqrt(jnp.mean(xn * xn, -1, keepdims=True) + EPS) * g_ref[...]
    f = y * (1.0 + scale_ref[0]) + shift_ref[0]
    f_hi = f.astype(BF16)
    f_ref[0] = f_hi
    f_lo = (f - f_hi.astype(F32)).astype(BF16)
    hh_hl = jnp.dot(f_hi, wr_ref[...], preferred_element_type=F32)
    lh = jnp.dot(f_lo, wr_ref[:, :ROUTER_COLS], preferred_element_type=F32)
    lg_ref[0] = hh_hl[:, :ROUTER_COLS] + hh_hl[:, ROUTER_COLS:] + lh + br_ref[...]


def _outproj_call(os, x, gate, shift, scale, gain, w_out, w_r, b_r, tm):
    bn, seq, d = x.shape
    per_batch = gate.shape[0] > 1
    mod_map = (lambda b, i: (b, 0, 0)) if per_batch else (lambda b, i: (0, 0, 0))
    row = pl.BlockSpec((1, tm, d), lambda b, i: (b, i, 0))
    o_specs = [pl.BlockSpec((1, tm, o.shape[2]), lambda b, i: (b, i, 0)) for o in os]
    return pl.pallas_call(
        _outproj_kernel,
        grid=(bn, seq // tm),
        in_specs=[*o_specs, row,
                  pl.BlockSpec((1, 1, d), mod_map),
                  pl.BlockSpec((1, 1, d), mod_map),
                  pl.BlockSpec((1, 1, d), mod_map),
                  pl.BlockSpec((1, d), lambda b, i: (0, 0)),
                  pl.BlockSpec((d, d), lambda b, i: (0, 0)),
                  pl.BlockSpec((d, 2 * ROUTER_COLS), lambda b, i: (0, 0)),
                  pl.BlockSpec((1, ROUTER_COLS), lambda b, i: (0, 0))],
        out_specs=[row, row, pl.BlockSpec((1, tm, ROUTER_COLS), lambda b, i: (b, i, 0))],
        out_shape=[jax.ShapeDtypeStruct((bn, seq, d), F32),
                   jax.ShapeDtypeStruct((bn, seq, d), BF16),
                   jax.ShapeDtypeStruct((bn, seq, ROUTER_COLS), F32)],
        compiler_params=_params(("parallel", "parallel"), 56),
        name="out_proj",
    )(*os, x, gate, shift, scale, gain, w_out, w_r, b_r)


def _moe_kernel(te_ref, nu_ref, x_ref, wg_ref, wu_ref, wd_ref, o_ref):
    @pl.when(pl.program_id(0) < nu_ref[0])
    def _():
        x = x_ref[...]
        g = jnp.dot(x, wg_ref[0], preferred_element_type=F32)
        u = jnp.dot(x, wu_ref[0], preferred_element_type=F32)
        a = (g * jax.nn.sigmoid(g) * u).astype(BF16)
        o_ref[...] = jnp.dot(a, wd_ref[0], preferred_element_type=F32).astype(o_ref.dtype)


def _moe_call(tile_expert, n_used, buf, w_gate, w_up, w_down, tm):
    p, d = buf.shape
    f = w_gate.shape[-1]
    grid_spec = pltpu.PrefetchScalarGridSpec(
        num_scalar_prefetch=2,
        grid=(p // tm,),
        in_specs=[pl.BlockSpec((tm, d), lambda i, te, nu: (i, 0)),
                  pl.BlockSpec((1, d, f), lambda i, te, nu: (te[i], 0, 0)),
                  pl.BlockSpec((1, d, f), lambda i, te, nu: (te[i], 0, 0)),
                  pl.BlockSpec((1, f, d), lambda i, te, nu: (te[i], 0, 0))],
        out_specs=pl.BlockSpec((tm, d), lambda i, te, nu: (i, 0)),
    )
    return pl.pallas_call(
        _moe_kernel,
        grid_spec=grid_spec,
        out_shape=jax.ShapeDtypeStruct((p, d), BF16),
        compiler_params=_params(("arbitrary",), 60),
        name="moe_experts",
    )(tile_expert, n_used, buf, w_gate, w_up, w_down)


ROUTE_TM = 256


def _route_kernel(lg_ref, ids_ref, wts_ref, cnt_ref, carry_ref):
    @pl.when(pl.program_id(0) == 0)
    def _():
        carry_ref[...] = jnp.zeros_like(carry_ref)

    lg = lg_ref[...]
    tm = lg.shape[0]
    lane = lax.broadcasted_iota(jnp.int32, lg.shape, 1)
    big = jnp.int32(ROUTER_COLS)
    is_g = lane < N_GROUPS
    gmax = jnp.max(jnp.where(is_g, lg, -jnp.inf), -1, keepdims=True)
    g_sel = jnp.min(jnp.where(is_g & (lg == gmax), lane, big), -1, keepdims=True)
    p_g = 1.0 / jnp.sum(jnp.where(is_g, jnp.exp(lg - gmax), 0.0), -1, keepdims=True)
    lo = N_GROUPS + E_PER_GROUP * g_sel
    in_e = (lane >= lo) & (lane < lo + E_PER_GROUP)
    el = jnp.where(in_e, lg, -jnp.inf)
    v1 = jnp.max(el, -1, keepdims=True)
    i1 = jnp.min(jnp.where(el == v1, lane, big), -1, keepdims=True)
    el2 = jnp.where(lane == i1, -jnp.inf, el)
    v2 = jnp.max(el2, -1, keepdims=True)
    i2 = jnp.min(jnp.where(el2 == v2, lane, big), -1, keepdims=True)
    t = jnp.exp(v2 - v1)
    w1 = p_g / (1.0 + t)
    w2 = w1 * t
    e1 = i1 - N_GROUPS
    e2 = i2 - N_GROUPS
    hit1 = lane == e1
    hit2 = lane == e2
    onehot = jnp.where(hit1 | hit2, 1.0, 0.0)
    ri = lax.broadcasted_iota(jnp.int32, (tm, tm), 0)
    ci = lax.broadcasted_iota(jnp.int32, (tm, tm), 1)
    below = jnp.where(ci < ri, 1.0, 0.0).astype(BF16)
    before = carry_ref[...] + jnp.dot(below, onehot.astype(BF16), preferred_element_type=F32)
    r1 = jnp.sum(jnp.where(hit1, before, 0.0), -1, keepdims=True).astype(jnp.int32)
    r2 = jnp.sum(jnp.where(hit2, before, 0.0), -1, keepdims=True).astype(jnp.int32)
    ids_ref[...] = jnp.where(lane == 0, e1, jnp.where(lane == 1, e2, jnp.where(lane == 2, r1, jnp.where(lane == 3, r2, 0))))
    wts_ref[...] = jnp.where(lane == 0, w1, jnp.where(lane == 1, w2, 0.0))
    carry_ref[...] = carry_ref[...] + jnp.sum(onehot, 0, keepdims=True)
    cnt_ref[...] = carry_ref[...]


def _route_call(logits):
    t = logits.shape[0]
    tm = min(t, ROUTE_TM)
    row = pl.BlockSpec((tm, ROUTER_COLS), lambda i: (i, 0))
    return pl.pallas_call(
        _route_kernel,
        grid=(t // tm,),
        in_specs=[row],
        out_specs=[row, row, pl.BlockSpec((1, ROUTER_COLS), lambda i: (0, 0))],
        out_shape=[jax.ShapeDtypeStruct((t, ROUTER_COLS), jnp.int32),
                   jax.ShapeDtypeStruct((t, ROUTER_COLS), F32),
                   jax.ShapeDtypeStruct((1, ROUTER_COLS), F32)],
        scratch_shapes=[pltpu.VMEM((1, ROUTER_COLS), F32)],
        compiler_params=_params(("arbitrary",), 32),
        name="route",
    )(logits)


def _resid_kernel(x_ref, y0_ref, y1_ref, w_ref, gate_ref, g_ref, o_ref, *, final_norm):
    w = w_ref[0]
    y = y0_ref[0].astype(F32) * w[:, 0:1] + y1_ref[0].astype(F32) * w[:, 1:2]
    xn = x_ref[0] + gate_ref[0] * y
    if final_norm:
        xn = xn * lax.rsqrt(jnp.mean(xn * xn, -1, keepdims=True) + EPS) * g_ref[...]
    o_ref[0] = xn


def _resid_call(x, y0, y1, w, gate, gain, tm, final_norm):
    bn, seq, d = x.shape
    per_batch = gate.shape[0] > 1
    mod_map = (lambda b, i: (b, 0, 0)) if per_batch else (lambda b, i: (0, 0, 0))
    row = pl.BlockSpec((1, tm, d), lambda b, i: (b, i, 0))
    return pl.pallas_call(
        functools.partial(_resid_kernel, final_norm=final_norm),
        grid=(bn, seq // tm),
        in_specs=[row, row, row, pl.BlockSpec((1, tm, ROUTER_COLS), lambda b, i: (b, i, 0)),
                  pl.BlockSpec((1, 1, d), mod_map), pl.BlockSpec((1, d), lambda b, i: (0, 0))],
        out_specs=row,
        out_shape=jax.ShapeDtypeStruct((bn, seq, d), F32),
        compiler_params=_params(("parallel", "parallel"), 48),
        name="moe_residual",
    )(x, y0, y1, w, gate, gain)


def _softplus(x):
    return jnp.maximum(x, 0.0) + jnp.log1p(jnp.exp(-jnp.abs(x)))


def _gdn_prep(x_ref, w_ref, dst_ref, off, mode):
    x = x_ref[0].astype(F32)
    n = x.shape[0]
    rows = lax.broadcasted_iota(jnp.int32, (n, 1), 0)
    pad = CONV_W // 2
    acc = x * w_ref[pad:pad + 1, :]
    for o in range(-pad, pad + 1):
        if o == 0:
            continue
        xs = pltpu.roll(x, (-o) % n, 0)
        valid = (rows + o >= 0) if o < 0 else (rows + o < n)
        acc = acc + jnp.where(valid, xs, 0.0) * w_ref[o + pad:o + pad + 1, :]
    y = acc * jax.nn.sigmoid(acc)
    if mode != "v":
        y = y * lax.rsqrt(jnp.sum(y * y, -1, keepdims=True) + EPS)
    if mode == "q":
        y = y * DK_A ** -0.5
    dst_ref[off:off + n, :] = y


def _gdn_kernel(*refs, h_a, n_ctx, cast):
    if cast:
        refs = list(refs)
        cast_dst = refs.pop(19)
        cast_src = refs.pop(16)
        cast_dst[...] = cast_src[...].astype(BF16)
    (alog_ref, dtb_ref,
     qc_ref, kc_ref, vc_ref, zc_ref, ql_ref, kl_ref, vl_ref, zl_ref,
     wq_ref, wk_ref, wv_ref, grow_ref, bcol_ref, nw_ref,
     oc_ref, ol_ref,
     q_s, k_s, v_s, o_s, u_s, l1_s, l2_s, gl_s, gc_s) = refs
    C = CHUNK
    lc = qc_ref.shape[1]
    n_tot = u_s.shape[1]
    h = pl.program_id(1)

    _gdn_prep(qc_ref, wq_ref, q_s, 0, "q")
    _gdn_prep(kc_ref, wk_ref, k_s, 0, "k")
    _gdn_prep(vc_ref, wv_ref, v_s, 0, "v")
    _gdn_prep(ql_ref, wq_ref, q_s, lc, "q")
    _gdn_prep(kl_ref, wk_ref, k_s, lc, "k")
    _gdn_prep(vl_ref, wv_ref, v_s, lc, "v")

    NB = 4
    W = NB * C
    ii = lax.broadcasted_iota(jnp.int32, (C, W), 0)
    jl = lax.broadcasted_iota(jnp.int32, (C, W), 1)
    blk = jl // C
    jm = jl - blk * C
    eye4 = ii == jm
    tri4 = ((blk < 2) & (ii >= jm)) | ((blk >= 2) & (ii <= jm))
    eye4f = eye4.astype(F32)
    lane2 = lax.broadcasted_iota(jnp.int32, (C, 2 * C), 1)
    lo2 = lane2 < C

    i2 = lax.broadcasted_iota(jnp.int32, (2 * C, 2 * C), 0)
    j2 = lax.broadcasted_iota(jnp.int32, (2 * C, 2 * C), 1)
    same = (i2 // C) == (j2 // C)
    for d in range(2):
        a = jnp.exp(jnp.full((1, 2 * C), alog_ref[d * h_a + h], F32))
        g = -a * _softplus(grow_ref[0, 0, d] + dtb_ref[d * h_a + h])
        tri = same & ((i2 <= j2) if d == 0 else (i2 >= j2))
        gc_s[d] = jnp.dot(g, tri.astype(F32), preferred_element_type=F32, precision=HIGHEST)

    def block_diag(x):
        return jnp.concatenate([jnp.where(blk == b, x, 0.0) for b in range(NB)], 0).astype(BF16)

    def pair_setup(p):
        r0 = pl.multiple_of(p * 2 * C, 2 * C)
        q2 = q_s[pl.ds(r0, 2 * C), :]
        k2 = k_s[pl.ds(r0, 2 * C), :]
        v2 = v_s[pl.ds(r0, 2 * C), :]
        k16 = k2.astype(BF16)
        gk = lax.dot_general(k16, k16, _NT, preferred_element_type=F32)
        gq = lax.dot_general(q2.astype(BF16), k16, _NT, preferred_element_type=F32)
        kk2 = jnp.where(lo2, gk[:C], gk[C:])
        qk2 = jnp.where(lo2, gq[:C], gq[C:])
        kk4 = jnp.concatenate([kk2, kk2], 1)
        qk4 = jnp.concatenate([qk2, qk2], 1)
        grow = [gc_s[d, pl.ds(p, 1), :] for d in range(2)]
        rowb = jnp.concatenate([jnp.broadcast_to(g, (C, 2 * C)) for g in grow], 1)
        diag = jnp.where(eye4, rowb, 0.0)
        gcol, bcol, tot = [], [], []
        for b in range(NB):
            d, par = b // 2, b % 2
            half = diag[:, d * 2 * C:(d + 1) * 2 * C]
            gcol.append(jnp.sum(jnp.where(lo2 if par == 0 else ~lo2, half, 0.0), axis=1, keepdims=True))
            bcol.append(jax.nn.sigmoid(bcol_ref[0, 0, pl.ds(r0 + par * C, C), d:d + 1]))
            e = par * C + (C - 1 if d == 0 else 0)
            tot.append(grow[d][:, e:e + 1])

        def pack(cols):
            return jnp.where(blk == 0, cols[0], jnp.where(blk == 1, cols[1], jnp.where(blk == 2, cols[2], cols[3])))

        gcc4 = pack(gcol)
        dec4 = jnp.where(tri4, jnp.exp(jnp.where(tri4, gcc4 - rowb, 0.0)), 0.0)
        m0 = jnp.where(eye4, 0.0, -(kk4 * pack(bcol) * dec4))
        return m0, (p, q2, k2, v2, qk4 * dec4, gcol, bcol, tot)

    def pair_finish(tinv4, aux):
        p, q2, k2, v2, attn4, gcol, bcol, tot = aux
        eg = [jnp.exp(g) for g in gcol]
        ks = (k2[:C], k2[C:])
        vs = (v2[:C], v2[C:])
        qs = (q2[:C], q2[C:])
        rhs = jnp.concatenate([jnp.concatenate([vs[b % 2] * bcol[b], ks[b % 2] * (bcol[b] * eg[b])], 1)
                               for b in range(NB)], 0).astype(BF16)
        uw = jnp.dot(block_diag(tinv4), rhs, preferred_element_type=F32)
        for b in range(NB):
            d, par = b // 2, b % 2
            c = 2 * p + par
            u_s[d, c] = uw[b * C:(b + 1) * C, :DK_A]
            l1_s[d, c] = jnp.concatenate([uw[b * C:(b + 1) * C, DK_A:], qs[par] * eg[b]], 0).astype(BF16)
            gl_s[d, c] = jnp.broadcast_to(jnp.exp(tot[b]), (1, DK_A))
        for d in range(2):
            f = jnp.concatenate([jnp.exp(tot[2 * d + par] - gcol[2 * d + par]) for par in range(2)], 0)
            kdec_t = (k2 * f).T
            l2_s[d, p] = jnp.concatenate([attn4[:, d * 2 * C:(d + 1) * 2 * C], kdec_t], 0).astype(BF16)

    n_pairs = n_tot // 2
    group = max(g for g in (1, 2, 3) if n_pairs % g == 0)

    def group_terms(i, _):
        st = [pair_setup(i * group + g) for g in range(group)]
        m0s = [s[0] for s in st]
        rs = [jnp.dot(m0.astype(BF16), block_diag(m0), preferred_element_type=F32) for m0 in m0s]
        qms = [eye4f + m0 for m0 in m0s]
        for _ in range(4):
            qrs = [jnp.dot(jnp.concatenate([qm, r], 0).astype(BF16), block_diag(r), preferred_element_type=F32)
                   for qm, r in zip(qms, rs)]
            qms = [qm + qr[:C] for qm, qr in zip(qms, qrs)]
            rs = [qr[C:] for qr in qrs]
        tinvs = [qm + jnp.dot(qm.astype(BF16), block_diag(r), preferred_element_type=F32)
                 for qm, r in zip(qms, rs)]
        for tinv4, s in zip(tinvs, st):
            pair_finish(tinv4, s[1])
        return 0

    lax.fori_loop(0, n_pairs // group, group_terms, 0)

    o_s[...] = jnp.zeros_like(o_s)

    def scan_step(i, carry):
        c_b = jnp.where(i < n_ctx, n_ctx - 1 - i, n_ctx + n_tot - 1 - i)
        new = []
        row_half = lax.broadcasted_iota(jnp.int32, (2 * C, DK_A), 0) // C
        for d, c, s in ((0, i, carry[0]), (1, c_b, carry[1])):
            r1 = jnp.dot(l1_s[d, c], s.astype(BF16), preferred_element_type=F32)
            v_new = u_s[d, c] - r1[:C]
            keep = row_half == c % 2
            v_ext = jnp.where(keep, jnp.concatenate([v_new, v_new], 0), 0.0).astype(BF16)
            r2 = jnp.dot(l2_s[d, c // 2], v_ext, preferred_element_type=F32)
            rows = pl.ds(pl.multiple_of(c * C, C), C)
            o_s[rows, :] = o_s[rows, :] + r1[C:] + r2[:C]
            new.append(s * gl_s[d, c] + r2[C:])
        return tuple(new)

    zero = jnp.zeros((DK_A, DK_A), F32)
    lax.fori_loop(0, n_tot, scan_step, (zero, zero))

    def finish(z_ref, o_ref, off):
        n = z_ref.shape[1]
        o = o_s[off:off + n, :]
        o = o * lax.rsqrt(jnp.mean(o * o, -1, keepdims=True) + EPS) * nw_ref[...]
        z = z_ref[0].astype(F32)
        o_ref[0] = (o * (z * jax.nn.sigmoid(z))).astype(o_ref.dtype)

    finish(zc_ref, oc_ref, 0)
    finish(zl_ref, ol_ref, lc)


def _gdn_call(p_c, p_l, ab_c, ab_l, conv_w, a_log, dt_bias, norm_w, h_a, cast_src=None):
    B, L, _ = p_l.shape
    lc = p_c.shape[1]
    t = lc + L
    n_tot, n_ctx = t // CHUNK, lc // CHUNK
    d_a = h_a * DK_A
    ab = jnp.concatenate([ab_c, ab_l], 1)[..., :4 * h_a].reshape(B, t, 2, 2, h_a)
    assert n_tot % 2 == 0
    g_row = ab[:, :, :, 0].transpose(0, 3, 2, 1).reshape(B, h_a, 2, n_tot // 2, 2 * CHUNK)
    b_col = ab[:, :, :, 1].transpose(0, 3, 1, 2)

    def col(k, n):
        return pl.BlockSpec((1, n, DK_A), lambda b, h, *_: (b, 0, k * h_a + h))

    def tap(k):
        return pl.BlockSpec((CONV_W, DK_A), lambda b, h, *_: (0, k * h_a + h))

    in_specs = [col(0, lc), col(1, lc), col(2, lc), col(3, lc),
                col(0, L), col(1, L), col(2, L), col(3, L),
                tap(0), tap(1), tap(2),
                pl.BlockSpec((1, 1, 2, n_tot // 2, 2 * CHUNK), lambda b, h, *_: (b, h, 0, 0, 0)),
                pl.BlockSpec((1, 1, t, 2), lambda b, h, *_: (b, h, 0, 0)),
                pl.BlockSpec((1, DK_A), lambda b, h, *_: (0, 0))]
    out_specs = [pl.BlockSpec((1, lc, DK_A), lambda b, h, *_: (b, 0, h)),
                 pl.BlockSpec((1, L, DK_A), lambda b, h, *_: (b, 0, h))]
    out_shape = [jax.ShapeDtypeStruct((B, lc, d_a), BF16), jax.ShapeDtypeStruct((B, L, d_a), BF16)]
    extra = ()
    if cast_src is not None:
        spec, shape = _cast_specs(cast_src, h_a)
        in_specs.append(spec)
        out_specs.append(spec)
        out_shape.append(shape)
        extra = (cast_src,)
    grid_spec = pltpu.PrefetchScalarGridSpec(
        num_scalar_prefetch=2,
        grid=(B, h_a),
        in_specs=in_specs,
        out_specs=out_specs,
        scratch_shapes=[pltpu.VMEM((t, DK_A), F32), pltpu.VMEM((t, DK_A), F32), pltpu.VMEM((t, DK_A), F32),
                        pltpu.VMEM((t, DK_A), F32),
                        pltpu.VMEM((2, n_tot, CHUNK, DK_A), F32),
                        pltpu.VMEM((2, n_tot, 2 * CHUNK, DK_A), BF16),
                        pltpu.VMEM((2, n_tot // 2, CHUNK + DK_A, 2 * CHUNK), BF16),
                        pltpu.VMEM((2, n_tot, 1, DK_A), F32),
                        pltpu.VMEM((2, n_tot // 2, 2 * CHUNK), F32)],
    )
    return pl.pallas_call(
        functools.partial(_gdn_kernel, h_a=h_a, n_ctx=n_ctx, cast=cast_src is not None),
        grid_spec=grid_spec,
        out_shape=out_shape,
        compiler_params=_params(("parallel", "parallel"), 56),
        name="gdn",
    )(a_log.reshape(-1), dt_bias.reshape(-1), p_c, p_c, p_c, p_c, p_l, p_l, p_l, p_l,
      conv_w, conv_w, conv_w, g_row, b_col, norm_w.reshape(1, DK_A), *extra)


_NT = (((1,), (1,)), ((), ()))


def _softmax_rows(s, sink=None):
    m = jnp.max(s, -1, keepdims=True)
    if sink is not None:
        m = jnp.maximum(m, sink)
    e = jnp.exp(s - m)
    den = jnp.sum(e, -1, keepdims=True)
    if sink is not None:
        den = den + jnp.exp(sink - m)
    return (e * (1.0 / den)).astype(BF16)


def _gqa_kernel(sink_ref, q_ref, kvl_ref, kvc_ref, cos_ref, sa_ref, sb_ref, *rest, n_heads, latent):
    if len(rest) == 3:
        cast_src, o_ref, cast_dst = rest
        cast_dst[...] = cast_src[...].astype(BF16)
    else:
        o_ref, = rest
    G = n_heads // KV_B
    P = 2 * DH
    n = pl.program_id(1)
    seq = kvl_ref.shape[1]
    lc = kvc_ref.shape[1]

    def rope(x, r0):
        reps = x.shape[1] // P
        rows = pl.ds(pl.multiple_of(r0, QBLK), QBLK)
        c, a, b = [jnp.concatenate([t[rows, :]] * reps, 1) if reps > 1 else t[rows, :]
                   for t in (cos_ref, sa_ref, sb_ref)]
        w = x.shape[1]
        return x * c + pltpu.roll(x, w - DH // 4, 1) * a + pltpu.roll(x, DH // 4, 1) * b

    q = q_ref[0].astype(F32) * DH ** -0.5
    kc = kvc_ref[0][:, :P].astype(F32)
    vc = kvc_ref[0][:, P:].astype(F32)
    if latent:
        q = rope(q, n * QBLK)
        ks, vs = [], []
        for o in (-1, 0, 1):
            s0 = jnp.clip((n + o) * QBLK, 0, seq - QBLK)
            kv = kvl_ref[0, pl.ds(pl.multiple_of(s0, QBLK), QBLK), :].astype(F32)
            ks.append(rope(kv[:, :P], s0))
            vs.append(kv[:, P:])
        k = jnp.concatenate(ks + [kc], 0)
        v = jnp.concatenate(vs + [vc], 0)
        nk = 3 * QBLK + lc
        qi = lax.broadcasted_iota(jnp.int32, (QBLK, nk), 0)
        kj = lax.broadcasted_iota(jnp.int32, (QBLK, nk), 1)
        kpos = n * QBLK - QBLK + kj
        rel = kj - QBLK - qi
        valid = ((rel >= -WIN) & (rel <= WIN) & (kpos >= 0) & (kpos < seq)) | (kj >= 3 * QBLK)
    else:
        k, v = kc, vc
        valid = None
    q = q.astype(BF16)
    k_sw = pltpu.roll(k, DH, 1).astype(BF16)
    v_sw = pltpu.roll(v, DH, 1).astype(BF16)
    k = k.astype(BF16)
    v = v.astype(BF16)
    lane = lax.broadcasted_iota(jnp.int32, (QBLK, P), 1)
    zero = jnp.zeros((QBLK, P), BF16)
    scores = []
    for h in range(n_heads):
        qp = q[:, (h // 2) * P:(h // 2 + 1) * P]
        qh = jnp.where(lane < DH, qp, zero) if h % 2 == 0 else jnp.where(lane >= DH, qp, zero)
        straight = (h // G == h % 2)
        scores.append(lax.dot_general(qh, k if straight else k_sw, _NT, preferred_element_type=F32))
    probs = []
    for h, s in enumerate(scores):
        if valid is not None:
            s = jnp.where(valid, s, NEG_INF)
        probs.append(_softmax_rows(s, sink_ref[h]))
    outs = [jnp.dot(pr, v if (h // G == h % 2) else v_sw, preferred_element_type=F32) for h, pr in enumerate(probs)]
    for p in range(n_heads // 2):
        o_ref[0, :, p * P:(p + 1) * P] = jnp.where(lane < DH, outs[2 * p], outs[2 * p + 1]).astype(o_ref.dtype)


def _gqa_call(p_q, p_l, p_c, sink, tables, d_model, d_b, latent, cast_src=None):
    B, nq, _ = p_q.shape
    L, lc = p_l.shape[1], p_c.shape[1]
    kvw = 2 * KV_B * DH
    n_heads = d_b // DH
    assert (n_heads // KV_B) % 2 == 0
    in_specs = [pl.BlockSpec((1, QBLK, d_b), lambda b, n, *_: (b, n, 2 * d_model // d_b)),
                pl.BlockSpec((1, L, kvw), lambda b, n, *_: (b, 0, 3 * d_model // kvw)),
                pl.BlockSpec((1, lc, kvw), lambda b, n, *_: (b, 0, 3 * d_model // kvw)),
                pl.BlockSpec((L, 2 * DH), lambda b, n, *_: (0, 0)),
                pl.BlockSpec((L, 2 * DH), lambda b, n, *_: (0, 0)),
                pl.BlockSpec((L, 2 * DH), lambda b, n, *_: (0, 0))]
    out_specs = [pl.BlockSpec((1, QBLK, d_b), lambda b, n, *_: (b, n, 0))]
    out_shape = [jax.ShapeDtypeStruct((B, nq, d_b), BF16)]
    extra = ()
    if cast_src is not None:
        spec, shape = _cast_specs(cast_src, nq // QBLK)
        in_specs.append(spec)
        out_specs.append(spec)
        out_shape.append(shape)
        extra = (cast_src,)
    grid_spec = pltpu.PrefetchScalarGridSpec(
        num_scalar_prefetch=1, grid=(B, nq // QBLK), in_specs=in_specs, out_specs=out_specs)
    return pl.pallas_call(
        functools.partial(_gqa_kernel, n_heads=n_heads, latent=latent),
        grid_spec=grid_spec,
        out_shape=out_shape,
        compiler_params=_params(("parallel", "parallel"), 48),
        name="window_gqa" if latent else "ctx_gqa",
    )(sink, p_q, p_l, p_c, *tables, *extra)


def _rope_tables(n_tokens):
    t = jnp.arange(n_tokens)
    row = (t // GRID_W).astype(F32)
    col = (t % GRID_W).astype(F32)
    half = DH // 2
    inv = ROPE_BASE ** (-jnp.arange(0, half, 2, dtype=F32) / half)
    ang_r = row[:, None] * inv[None]
    ang_c = col[:, None] * inv[None]
    ang = jnp.concatenate([ang_r, ang_r, ang_c, ang_c], -1)
    cos, sin = jnp.cos(ang), jnp.sin(ang)
    first = (jnp.arange(DH) % half) < half // 2
    sa = jnp.where(first, -sin, 0.0)
    sb = jnp.where(first, 0.0, sin)
    return tuple(jnp.concatenate([u, u], -1) for u in (cos, sa, sb))


def _na_kernel(q_ref, k_ref, v_ref, kc_ref, vc_ref, tbl_ref, *rest, n_heads, rows, latent):
    if len(rest) == 3:
        cast_src, o_ref, cast_dst = rest
        cast_dst[...] = cast_src[...].astype(BF16)
    else:
        o_ref, = rest
    P = 2 * DH
    r = pl.program_id(1)
    nq = q_ref.shape[1]
    nb = KH_MAX * GRID_W
    q = (q_ref[0].astype(F32) * DH ** -0.5).astype(BF16)
    lane = lax.broadcasted_iota(jnp.int32, (nq, P), 1)
    zero = jnp.zeros((nq, P), BF16)
    if latent:
        r0 = jnp.clip(r - KH_MAX // 2, 0, rows - KH_MAX)
        band = pl.ds(pl.multiple_of(r0 * GRID_W, GRID_W), nb)
        d0 = r0 - r + (KH_MAX - 1)
    scores, vals = [], []
    for p in range(n_heads // 2):
        cols = slice(p * P, (p + 1) * P)
        qp = q[:, cols]
        kc = kc_ref[0, :, cols]
        vc = vc_ref[0, :, cols]
        if latent:
            kk = jnp.concatenate([k_ref[0, band, cols], kc], 0)
            vals.append(jnp.concatenate([v_ref[0, band, cols], vc], 0))
        else:
            kk = kc
            vals.append(vc)
        for half in range(2):
            qh = jnp.where(lane < DH, qp, zero) if half == 0 else jnp.where(lane >= DH, qp, zero)
            scores.append(lax.dot_general(qh, kk, _NT, preferred_element_type=F32))
    probs = []
    for h, s in enumerate(scores):
        if latent:
            bias = jnp.concatenate([tbl_ref[h, d0 + 2 * t] for t in range(KH_MAX // 2)], 1)
            s = jnp.concatenate([s[:, :nb] + bias, s[:, nb:]], 1)
        probs.append(_softmax_rows(s))
    outs = [jnp.dot(pr, vals[h // 2], preferred_element_type=F32) for h, pr in enumerate(probs)]
    for p in range(n_heads // 2):
        o_ref[0, :, p * P:(p + 1) * P] = jnp.where(lane < DH, outs[2 * p], outs[2 * p + 1]).astype(o_ref.dtype)


def _na_bias_table(rpb):
    qc = jnp.arange(GRID_W)[:, None]
    kc = jnp.arange(GRID_W)[None, :]
    dc = jnp.clip(kc - qc + (KW - 1), 0, 2 * KW - 2)
    q_cs = jnp.clip(qc - KW // 2, 0, GRID_W - KW)
    in_win = (kc >= q_cs) & (kc < q_cs + KW)
    t = jnp.where(in_win[None, None], rpb[:, :, dc], NEG_INF)
    return jnp.concatenate([t[:, :-1], t[:, 1:]], -1)


def _na_call(p_q, p_l, p_c, tbl, d_model, d_c, latent, cast_src=None):
    B, nq, _ = p_q.shape
    L, lc = p_l.shape[1], p_c.shape[1]
    n_heads = d_c // DH
    rows = L // GRID_W
    assert rows >= KH_MAX and n_heads % 2 == 0
    cb = 2 * d_model // d_c
    qb = GRID_W
    in_specs = [pl.BlockSpec((1, qb, d_c), lambda b, r: (b, r, cb + 1)),
                pl.BlockSpec((1, L, d_c), lambda b, r: (b, 0, cb + 2)),
                pl.BlockSpec((1, L, d_c), lambda b, r: (b, 0, cb + 3)),
                pl.BlockSpec((1, lc, d_c), lambda b, r: (b, 0, cb + 2)),
                pl.BlockSpec((1, lc, d_c), lambda b, r: (b, 0, cb + 3)),
                pl.BlockSpec(tbl.shape, lambda b, r: (0, 0, 0, 0))]
    out_specs = [pl.BlockSpec((1, qb, d_c), lambda b, r: (b, r, 0))]
    out_shape = [jax.ShapeDtypeStruct((B, nq, d_c), BF16)]
    extra = ()
    if cast_src is not None:
        spec, shape = _cast_specs(cast_src, nq // qb)
        in_specs.append(spec)
        out_specs.append(spec)
        out_shape.append(shape)
        extra = (cast_src,)
    return pl.pallas_call(
        functools.partial(_na_kernel, n_heads=n_heads, rows=rows, latent=latent),
        grid=(B, nq // qb),
        in_specs=in_specs,
        out_specs=out_specs,
        out_shape=out_shape,
        compiler_params=_params(("parallel", "parallel"), 48),
        name="nbr_attn" if latent else "ctx_attn",
    )(p_q, p_l, p_l, p_c, p_c, tbl, *extra)


def _l2norm(u):
    return u * lax.rsqrt(jnp.sum(u * u, -1, keepdims=True) + EPS)


def _short_conv(u, w):
    pad = CONV_W // 2
    L = u.shape[1]
    up = jnp.pad(u, ((0, 0), (pad, pad), (0, 0)))
    return sum(up[:, i:i + L] * w[i] for i in range(CONV_W))


def _gdn_prepare(qkv, ab, conv_w, a_log, dt_bias, h_a):
    B, L, _ = qkv.shape
    qkv = jax.nn.silu(_short_conv(qkv, conv_w))
    q, k, v = jnp.split(qkv, 3, -1)
    q = _l2norm(q.reshape(B, L, h_a, DK_A))
    k = _l2norm(k.reshape(B, L, h_a, DK_A))
    v = v.reshape(B, L, h_a, DK_A)
    ab = ab.reshape(B, L, 2, 2, h_a)
    g = -jnp.exp(a_log) * jax.nn.softplus(ab[:, :, :, 0] + dt_bias)
    beta = jax.nn.sigmoid(ab[:, :, :, 1])
    return q, k, v, g, beta


def _gdn_chunked(q, k, v, g, beta, s0):
    B, L, H, dk = k.shape
    dv = v.shape[-1]
    n = L // CHUNK

    def chunks(u):
        return u.reshape(B, n, CHUNK, H, -1).transpose(0, 3, 1, 2, 4)

    qc, kc, vc = chunks(q * dk ** -0.5), chunks(k), chunks(v)
    gc = g.reshape(B, n, CHUNK, H).transpose(0, 3, 1, 2)
    bc = beta.reshape(B, n, CHUNK, H).transpose(0, 3, 1, 2)
    gcum = jnp.cumsum(gc, -1)
    tril = jnp.tril(jnp.ones((CHUNK, CHUNK), bool))
    strict = jnp.tril(jnp.ones((CHUNK, CHUNK), bool), -1)
    diff = gcum[..., :, None] - gcum[..., None, :]
    decay = jnp.where(tril, jnp.exp(jnp.where(tril, diff, 0.0)), 0.0)
    kb = kc * bc[..., None]
    vb = vc * bc[..., None]
    a = jnp.where(strict, jnp.einsum('bhnid,bhnjd->bhnij', kb, kc) * decay, 0.0)
    eye = jnp.eye(CHUNK, dtype=F32)
    tinv = lax.linalg.triangular_solve(eye + a, jnp.broadcast_to(eye, a.shape),
                                       left_side=True, lower=True, unit_diagonal=True)
    u = tinv @ vb
    w = tinv @ (kb * jnp.exp(gcum)[..., None])
    attn = jnp.einsum('bhnid,bhnjd->bhnij', qc, kc) * decay
    qg = qc * jnp.exp(gcum)[..., None]
    kdec = kc * jnp.exp(gcum[..., -1:] - gcum)[..., None]
    glast = jnp.exp(gcum[..., -1])
    xs = tuple(jnp.moveaxis(t, 2, 0) for t in (u, w, attn, qg, kdec, glast))

    def step(s, inp):
        u_i, w_i, attn_i, qg_i, kdec_i, gl_i = inp
        v_new = u_i - w_i @ s
        o_i = qg_i @ s + attn_i @ v_new
        s = s * gl_i[..., None, None] + jnp.einsum('bhcd,bhce->bhde', kdec_i, v_new)
        return s, o_i

    s_final, o = lax.scan(step, s0, xs)
    o = jnp.moveaxis(o, 0, 2).transpose(0, 2, 3, 1, 4).reshape(B, L, H, dv)
    return o, s_final


def _gdn_output(o, z, norm_w, h_a):
    B, L = z.shape[:2]
    o = o * lax.rsqrt(jnp.mean(o * o, -1, keepdims=True) + EPS) * norm_w
    zg = jax.nn.silu(z.reshape(B, L, h_a, DK_A))
    return (o * zg).reshape(B, L, h_a * DK_A)


def _gated_deltanet(qkv_c, ab_c, z_c, qkv_l, ab_l, z_l, conv_w, a_log, dt_bias, norm_w, ctx_out, h_a):
    qc, kc, vc, gc, bc = _gdn_prepare(qkv_c, ab_c, conv_w, a_log, dt_bias, h_a)
    ql, kl, vl, gl, bl = _gdn_prepare(qkv_l, ab_l, conv_w, a_log, dt_bias, h_a)
    B = ql.shape[0]
    s0 = jnp.zeros((B, h_a, DK_A, DK_A), F32)
    o_c_sum, o_l_sum = 0.0, 0.0
    for d in range(2):
        seq_c = (qc, kc, vc, gc[:, :, d], bc[:, :, d])
        seq_l = (ql, kl, vl, gl[:, :, d], bl[:, :, d])
        if d == 1:
            seq_c = tuple(jnp.flip(t, 1) for t in seq_c)
            seq_l = tuple(jnp.flip(t, 1) for t in seq_l)
        o_c, s_c = _gdn_chunked(*seq_c, s0)
        o_l, _ = _gdn_chunked(*seq_l, s_c)
        if d == 1:
            o_c, o_l = jnp.flip(o_c, 1), jnp.flip(o_l, 1)
        o_c_sum = o_c_sum + o_c
        o_l_sum = o_l_sum + o_l
    o_lat = _gdn_output(o_l_sum, z_l, norm_w, h_a)
    o_ctx = _gdn_output(o_c_sum, z_c, norm_w, h_a) if ctx_out else None
    return o_ctx, o_lat


def _rope_tables_ref(n_tokens, head_dim):
    t = jnp.arange(n_tokens)
    row = (t // GRID_W).astype(F32)
    col = (t % GRID_W).astype(F32)
    half = head_dim // 2
    inv = ROPE_BASE ** (-jnp.arange(0, half, 2, dtype=F32) / half)
    ang_r = row[:, None] * inv[None]
    ang_c = col[:, None] * inv[None]
    ang = jnp.concatenate([ang_r, ang_r, ang_c, ang_c], -1)
    return jnp.cos(ang), jnp.sin(ang)


def _apply_rope(x, cos, sin):
    half = x.shape[-1] // 2

    def rot(u):
        u1, u2 = jnp.split(u, 2, -1)
        return jnp.concatenate([-u2, u1], -1)

    xr = jnp.concatenate([rot(x[..., :half]), rot(x[..., half:])], -1)
    return x * cos[None, :, None, :] + xr * sin[None, :, None, :]


def _softmax_with_sink(s, sink):
    sk = jnp.broadcast_to(sink, s.shape[:-1] + (1,))
    p = jax.nn.softmax(jnp.concatenate([s, sk], -1), -1)
    return p[..., :-1]


def _window_gqa(q_c, kv_c, q_l, kv_l, sink, cos, sin, ctx_out):
    B, L = q_l.shape[:2]
    Lc = q_c.shape[1]
    h_b = q_l.shape[-1] // DH
    G = h_b // KV_B
    n = L // QBLK
    J = 3 * QBLK
    scale = DH ** -0.5
    k_l, v_l = [u.reshape(B, L, KV_B, DH) for u in jnp.split(kv_l, 2, -1)]
    k_c, v_c = [u.reshape(B, Lc, KV_B, DH) for u in jnp.split(kv_c, 2, -1)]
    q_l = _apply_rope(q_l.reshape(B, L, h_b, DH), cos, sin)
    k_l = _apply_rope(k_l, cos, sin)
    sink = sink.reshape(KV_B, G, 1, 1)

    def band(u):
        up = jnp.pad(u, ((0, 0), (QBLK, QBLK), (0, 0), (0, 0))).reshape(B, n + 2, QBLK, KV_B, DH)
        return jnp.concatenate([up[:, :-2], up[:, 1:-1], up[:, 2:]], axis=2)

    k_w, v_w = band(k_l), band(v_l)
    qb = q_l.reshape(B, n, QBLK, KV_B, G, DH)
    qi = jnp.arange(QBLK)[:, None]
    kj = jnp.arange(J)[None]
    kpos = (jnp.arange(n) * QBLK)[:, None, None] - QBLK + kj[None]
    in_win = (jnp.abs(kj - QBLK - qi)[None] <= WIN) & (kpos >= 0) & (kpos < L)
    s_w = jnp.einsum('bnqkgd,bnjkd->bnkgqj', qb, k_w) * scale
    s_w = jnp.where(in_win[None, :, None, None], s_w, NEG_INF)
    s_x = jnp.einsum('bnqkgd,bckd->bnkgqc', qb, k_c) * scale
    p = _softmax_with_sink(jnp.concatenate([s_w, s_x], -1), sink)
    o = (jnp.einsum('bnkgqj,bnjkd->bnqkgd', p[..., :J], v_w)
         + jnp.einsum('bnkgqc,bckd->bnqkgd', p[..., J:], v_c))
    o_lat = o.reshape(B, L, h_b * DH)
    o_ctx = None
    if ctx_out:
        qc = q_c.reshape(B, Lc, KV_B, G, DH)
        s_c = jnp.einsum('bqkgd,bckd->bkgqc', qc, k_c) * scale
        p_c = _softmax_with_sink(s_c, sink)
        o_ctx = jnp.einsum('bkgqc,bckd->bqkgd', p_c, v_c).reshape(B, Lc, h_b * DH)
    return o_ctx, o_lat


def _neighbourhood_attn(q_c, kv_c, q_l, kv_l, rpb, ctx_out):
    B, L = q_l.shape[:2]
    Lc = q_c.shape[1]
    h_c = q_l.shape[-1] // DH
    rows = L // GRID_W
    kh = min(KH_MAX, rows)
    ncb = GRID_W // NA_QCOLS
    m = kh * NA_KCOLS
    scale = DH ** -0.5
    k_l, v_l = [u.reshape(B, L, h_c, DH) for u in jnp.split(kv_l, 2, -1)]
    k_c, v_c = [u.reshape(B, Lc, h_c, DH) for u in jnp.split(kv_c, 2, -1)]
    r = jnp.arange(rows)
    row_idx = jnp.clip(r - kh // 2, 0, rows - kh)[:, None] + jnp.arange(kh)[None]
    cb = jnp.arange(ncb)
    col_idx = (jnp.clip(cb * NA_QCOLS - KW // 2, 0, GRID_W - NA_KCOLS)[:, None]
               + jnp.arange(NA_KCOLS)[None])

    def gather_band(u):
        u = u.reshape(B, rows, GRID_W, h_c, DH)[:, row_idx]
        u = u[:, :, :, col_idx]
        return u.transpose(0, 1, 3, 2, 4, 5, 6).reshape(B, rows, ncb, m, h_c, DH)

    k_n, v_n = gather_band(k_l), gather_band(v_l)
    qn = q_l.reshape(B, rows, ncb, NA_QCOLS, h_c, DH)
    q_col = cb[:, None] * NA_QCOLS + jnp.arange(NA_QCOLS)[None]
    q_cs = jnp.clip(q_col - KW // 2, 0, GRID_W - KW)
    key_col = jnp.tile(col_idx, (1, kh))
    key_row = jnp.repeat(row_idx, NA_KCOLS, axis=1)
    in_win = ((key_col[:, None, :] >= q_cs[:, :, None])
              & (key_col[:, None, :] < q_cs[:, :, None] + KW))
    dr = key_row - r[:, None] + (KH_MAX - 1)
    dc = jnp.clip(key_col[:, None, :] - q_col[:, :, None] + (KW - 1), 0, 2 * KW - 2)
    bias = rpb[:, dr[:, None, None, :], dc[None]]
    s_nb = jnp.einsum('brcqhd,brcmhd->bhrcqm', qn, k_n) * scale + bias
    s_nb = jnp.where(in_win, s_nb, NEG_INF)
    s_x = jnp.einsum('brcqhd,bkhd->bhrcqk', qn, k_c) * scale
    p = jax.nn.softmax(jnp.concatenate([s_nb, s_x], -1), -1)
    o = (jnp.einsum('bhrcqm,brcmhd->brcqhd', p[..., :m], v_n)
         + jnp.einsum('bhrcqk,bkhd->brcqhd', p[..., m:], v_c))
    o_lat = o.reshape(B, L, h_c * DH)
    o_ctx = None
    if ctx_out:
        qc = q_c.reshape(B, Lc, h_c, DH)
        s_c = jnp.einsum('bqhd,bkhd->bhqk', qc, k_c) * scale
        p_c = jax.nn.softmax(s_c, -1)
        o_ctx = jnp.einsum('bhqk,bkhd->bqhd', p_c, v_c).reshape(B, Lc, h_c * DH)
    return o_ctx, o_lat


def _route(logits):
    T = logits.shape[0]
    g_logits = logits[:, :N_GROUPS]
    g_prob = jax.nn.softmax(g_logits, -1)
    g_sel = jnp.argmax(g_logits, -1)
    e_logits = logits[:, N_GROUPS:N_GROUPS + N_EXPERTS].reshape(T, N_GROUPS, E_PER_GROUP)
    e_in = jnp.take_along_axis(e_logits, g_sel[:, None, None], 1)[:, 0]
    top_v, top_i = lax.top_k(e_in, TOP_K)
    wts = jax.nn.softmax(top_v, -1) * jnp.take_along_axis(g_prob, g_sel[:, None], 1)
    expert = g_sel[:, None] * E_PER_GROUP + top_i
    return expert.astype(jnp.int32), wts


def _dispatch(expert, tm):
    T = expert.shape[0]
    S = T * TOP_K
    e_flat = expert.reshape(S)
    order = jnp.argsort(e_flat)
    e_sorted = e_flat[order]
    tok_sorted = order // TOP_K
    counts = jnp.bincount(e_flat, length=N_EXPERTS)
    starts = jnp.cumsum(counts) - counts
    padded = (counts + tm - 1) // tm * tm
    pends = jnp.cumsum(padded)
    pstarts = pends - padded
    dest = pstarts[e_sorted] + jnp.arange(S) - starts[e_sorted]
    n_tiles = -(-S // tm) + N_EXPERTS
    tile_expert = jnp.minimum(jnp.searchsorted(pends, jnp.arange(n_tiles) * tm, side='right'),
                              N_EXPERTS - 1).astype(jnp.int32)
    n_used = (pends[-1] // tm).astype(jnp.int32).reshape(1)
    pos = jnp.zeros((S,), jnp.int32).at[order].set(dest.astype(jnp.int32)).reshape(T, TOP_K)
    src_tok = jnp.zeros((n_tiles * tm,), jnp.int32).at[dest].set(tok_sorted.astype(jnp.int32))
    return tile_expert, n_used, pos, src_tok


def _moe(f, logits, w_gate, w_up, w_down):
    T, d = f.shape
    tm = MOE_TM
    ids, wts, cnt = _route_call(logits)
    expert, rank = ids[:, :TOP_K], ids[:, TOP_K:2 * TOP_K]
    counts = cnt[0, :N_EXPERTS].astype(jnp.int32)
    padded = (counts + tm - 1) // tm * tm
    pends = jnp.cumsum(padded)
    pstarts = pends - padded
    start = jnp.sum(jnp.where(expert[..., None] == jnp.arange(N_EXPERTS), pstarts, 0), -1)
    dest = start + rank
    n_tiles = -(-T * TOP_K // tm) + N_EXPERTS
    tile_expert = jnp.minimum(jnp.sum(pends[None, :] <= (jnp.arange(n_tiles) * tm)[:, None], -1),
                              N_EXPERTS - 1).astype(jnp.int32)
    n_used = (pends[-1] // tm).astype(jnp.int32).reshape(1)
    tok = jnp.broadcast_to(jnp.arange(T, dtype=jnp.int32)[:, None], (T, TOP_K))
    src_tok = jnp.zeros((n_tiles * tm,), jnp.int32).at[dest.reshape(-1)].set(tok.reshape(-1))
    out = _moe_call(tile_expert, n_used, f[src_tok], w_gate, w_up, w_down, tm)
    return out[dest[:, 0]], out[dest[:, 1]], wts


def kernel(x, c, ctx, c_ctx, w_mod, b_mod, norm_mix, norm_ffn, w_in, conv_a, a_log, dt_bias, gdn_norm, sink_b, rpb_c, w_out, w_router_group, b_router_group, w_router_expert, b_router_expert, w_gate, w_up, w_down, norm_final):
    B, L, D = x.shape
    Lc = ctx.shape[1]
    depth = w_mod.shape[0]
    d_a = D // 2
    d_b = D // 4
    d_c = D - d_a - d_b
    h_a = d_a // DK_A
    n_ab = 4 * h_a
    kvb = 2 * KV_B * DH
    o_ab = 4 * d_a
    n_main = 4 * d_a + d_b + kvb + 3 * d_c

    cc = jnp.zeros((MOD_ROWS, D), F32).at[:B].set(c).at[B].set(c_ctx)
    mod = _mod_call(cc, w_mod, b_mod).reshape(depth, MOD_ROWS, N_MOD, D)
    rope = _rope_tables(L)

    o_qb = o_ab + n_ab
    o_kvb = o_qb + d_b
    w_main, w_ab = _winprep_call(w_in, (o_ab, o_qb, o_kvb, o_kvb + kvb))

    tn_in = n_main
    for cand in (1280, 1024, 768, 640, 512, 384, 256, 128):
        if n_main % cand == 0:
            tn_in = cand
            break

    x_l, x_c = x, ctx
    for l in range(depth):
        last = l == depth - 1
        ctx_out = not last
        mod_l = mod[l, :B]
        mod_c = mod[l, B:B + 1]
        ml = [mod_l[:, i:i + 1] for i in range(N_MOD)]
        mc = [mod_c[:, i:i + 1] for i in range(N_MOD)]
        gain_mix = norm_mix[l].reshape(1, D)
        p_l, ab_l = _inproj_call(x_l, ml[0], ml[1], gain_mix, w_main, w_ab, l, min(L, 1024), tn_in)
        p_c, ab_c = _inproj_call(x_c, mc[0], mc[1], gain_mix, w_main, w_ab, l, min(Lc, 1024), tn_in)

        n_e, _, d_f = w_gate.shape[1:]
        src_d = _cast_chunks(w_down[l], B * h_a)
        src_u = _cast_chunks(w_up[l], B * (L // QBLK))
        src_g = _cast_chunks(w_gate[l], B * (L // GRID_W))
        oA_c, oA_l, *wd = _gdn_call(p_c, p_l, ab_c, ab_l, conv_a[l], a_log[l], dt_bias[l], gdn_norm[l], h_a, src_d)
        oB_l, *wu = _gqa_call(p_l, p_l, p_c, sink_b[l], rope, D, d_b, True, src_u)
        tbl = _na_bias_table(rpb_c[l])
        oC_l, *wg = _na_call(p_l, p_l, p_c, tbl, D, d_c, True, src_g)
        wd = wd[0].reshape(n_e, d_f, D) if wd else w_down[l].astype(BF16)
        wu = wu[0].reshape(n_e, D, d_f) if wu else w_up[l].astype(BF16)
        wg = wg[0].reshape(n_e, D, d_f) if wg else w_gate[l].astype(BF16)
        if ctx_out:
            oB_c, = _gqa_call(p_c, p_l, p_c, sink_b[l], rope, D, d_b, False)
            oC_c, = _na_call(p_c, p_l, p_c, tbl, D, d_c, False)

        w_o = w_out[l].astype(BF16)
        w_r = jnp.pad(jnp.concatenate([w_router_group[l], w_router_expert[l]], -1),
                      ((0, 0), (0, ROUTER_COLS - N_GROUPS - N_EXPERTS)))
        w_r_hi = w_r.astype(BF16)
        w_r = jnp.concatenate([w_r_hi, (w_r - w_r_hi.astype(F32)).astype(BF16)], -1)
        b_r = jnp.pad(jnp.concatenate([b_router_group[l], b_router_expert[l]], -1),
                      (0, ROUTER_COLS - N_GROUPS - N_EXPERTS)).reshape(1, ROUTER_COLS)
        gain_ffn = norm_ffn[l].reshape(1, D)
        x_l, f_l, lg_l = _outproj_call((oA_l, oB_l, oC_l), x_l, ml[2], ml[3], ml[4], gain_ffn, w_o, w_r, b_r,
                                       min(L, 256))
        if ctx_out:
            x_c, f_c, lg_c = _outproj_call((oA_c, oB_c, oC_c), x_c, mc[2], mc[3], mc[4], gain_ffn, w_o, w_r, b_r,
                                           min(Lc, 256))
            f_all = jnp.concatenate([f_c.reshape(B * Lc, D), f_l.reshape(B * L, D)], 0)
            lg_all = jnp.concatenate([lg_c.reshape(B * Lc, ROUTER_COLS), lg_l.reshape(B * L, ROUTER_COLS)], 0)
            ys = _moe(f_all, lg_all, wg, wu, wd)
            y_c = [u[:B * Lc].reshape(B, Lc, -1) for u in ys]
            y_l = [u[B * Lc:].reshape(B, L, -1) for u in ys]
            x_c = _resid_call(x_c, *y_c, mc[5], gain_ffn, min(Lc, 256), False)
        else:
            y_l = [u.reshape(B, L, -1) for u in _moe(f_l.reshape(B * L, D), lg_l.reshape(B * L, ROUTER_COLS), wg, wu, wd)]
        x_l = _resid_call(x_l, *y_l, ml[5], norm_final.reshape(1, D), min(L, 256), last)
    return x_l
```

```python
import functools
import math

import jax
import jax.numpy as jnp
from jax import lax
from jax.experimental import pallas as pl
from jax.experimental.pallas import tpu as pltpu

F32 = jnp.float32
BF16 = jnp.bfloat16
HIGHEST = lax.Precision.HIGHEST

EPS = 1e-6
NEG_INF = -1e30
N_MOD = 6
GRID_W = 64
DH = 64
DK_A = 128
CONV_W = 5
CHUNK = 64
KV_B = 2
WIN = 128
QBLK = 128
ROPE_BASE = 10000.0
KH_MAX = 8
KW = 16
NA_QCOLS = 16
NA_KCOLS = 32
N_GROUPS = 4
E_PER_GROUP = 8
N_EXPERTS = N_GROUPS * E_PER_GROUP
TOP_K = 2
LANE = 128
MOD_ROWS = 16
ROUTER_COLS = LANE
MOE_TM = 256


def _params(sem, vmem_mb):
    return pltpu.CompilerParams(dimension_semantics=sem, vmem_limit_bytes=vmem_mb << 20)


def _mod_kernel(c_ref, w_ref, b_ref, o_ref):
    c = c_ref[...]
    s = c * jax.nn.sigmoid(c)
    o_ref[0] = jnp.dot(s, w_ref[0], preferred_element_type=F32, precision=HIGHEST) + b_ref[0]


def _mod_call(cc, w_mod, b_mod):
    depth, d, n = w_mod.shape
    tn = min(n, 1024)
    return pl.pallas_call(
        _mod_kernel,
        grid=(depth, n // tn),
        in_specs=[pl.BlockSpec((MOD_ROWS, d), lambda l, j: (0, 0)),
                  pl.BlockSpec((1, d, tn), lambda l, j: (l, 0, j)),
                  pl.BlockSpec((1, 1, tn), lambda l, j: (l, 0, j))],
        out_specs=pl.BlockSpec((1, MOD_ROWS, tn), lambda l, j: (l, 0, j)),
        out_shape=jax.ShapeDtypeStruct((depth, MOD_ROWS, n), F32),
        compiler_params=_params(("parallel", "parallel"), 48),
        name="mod_proj",
    )(cc, w_mod, b_mod.reshape(depth, 1, n))


def _winprep_kernel(w_ref, m_ref, ab_ref, *, bounds):
    o_ab, o_qb, o_kvb, o_qc = bounds
    x = w_ref[0]
    m_ref[0] = jnp.concatenate([x[:, :o_ab], x[:, o_qb:o_kvb], x[:, o_qc:], x[:, o_kvb:o_qc]], 1).astype(BF16)
    ab = x[:, o_ab:o_qb]
    ab_ref[0] = jnp.concatenate([ab, jnp.zeros((x.shape[0], LANE - ab.shape[1]), F32)], 1).astype(BF16)


def _winprep_call(w_in, bounds):
    depth, d, n = w_in.shape
    n_main = n - (bounds[1] - bounds[0])
    rt = min(d, 256)
    return pl.pallas_call(
        functools.partial(_winprep_kernel, bounds=bounds),
        grid=(depth, d // rt),
        in_specs=[pl.BlockSpec((1, rt, n), lambda l, i: (l, i, 0))],
        out_specs=[pl.BlockSpec((1, rt, n_main), lambda l, i: (l, i, 0)),
                   pl.BlockSpec((1, rt, LANE), lambda l, i: (l, i, 0))],
        out_shape=[jax.ShapeDtypeStruct((depth, d, n_main), BF16), jax.ShapeDtypeStruct((depth, d, LANE), BF16)],
        compiler_params=_params(("parallel", "parallel"), 48),
        name="w_in_prep",
    )(w_in)


def _cast_chunks(w, layer, n):
    depth, rows = w.shape[0], w.shape[1] * w.shape[2]
    if rows % n or (rows // n) % 16:
        return None
    return w.reshape(depth * n, rows // n, w.shape[3]), layer * n, n


def _cast_specs(cast, n_inner):
    chunks, first, n = cast
    blk = (1,) + chunks.shape[1:]
    src = pl.BlockSpec(blk, lambda b, j, *_: (first + b * n_inner + j, 0, 0))
    dst = pl.BlockSpec(blk, lambda b, j, *_: (b * n_inner + j, 0, 0))
    return src, dst, jax.ShapeDtypeStruct((n,) + chunks.shape[1:], BF16)


def _inproj_kernel(x_ref, shift_ref, scale_ref, g_ref, w_ref, wab_ref, o_ref, ab_ref, h_ref):
    @pl.when(pl.program_id(2) == 0)
    def _():
        x = x_ref[0]
        y = x * lax.rsqrt(jnp.mean(x * x, -1, keepdims=True) + EPS) * g_ref[...]
        h = (y * (1.0 + scale_ref[0]) + shift_ref[0]).astype(BF16)
        h_ref[...] = h
        ab_ref[0] = jnp.dot(h, wab_ref[0], preferred_element_type=F32)

    o_ref[0] = jnp.dot(h_ref[...], w_ref[0], preferred_element_type=F32).astype(o_ref.dtype)


def _inproj_call(x, shift, scale, gain, w_main, w_ab, layer, tm, tn):
    bn, seq, d = x.shape
    n = w_main.shape[2]
    per_batch = shift.shape[0] > 1
    mod_map = (lambda b, i, j: (b, 0, 0)) if per_batch else (lambda b, i, j: (0, 0, 0))
    return pl.pallas_call(
        _inproj_kernel,
        grid=(bn, seq // tm, n // tn),
        in_specs=[pl.BlockSpec((1, tm, d), lambda b, i, j: (b, i, 0)),
                  pl.BlockSpec((1, 1, d), mod_map),
                  pl.BlockSpec((1, 1, d), mod_map),
                  pl.BlockSpec((1, d), lambda b, i, j: (0, 0)),
                  pl.BlockSpec((1, d, tn), lambda b, i, j: (layer, 0, j)),
                  pl.BlockSpec((1, d, LANE), lambda b, i, j: (layer, 0, 0))],
        out_specs=[pl.BlockSpec((1, tm, tn), lambda b, i, j: (b, i, j)),
                   pl.BlockSpec((1, tm, LANE), lambda b, i, j: (b, i, 0))],
        out_shape=[jax.ShapeDtypeStruct((bn, seq, n), BF16),
                   jax.ShapeDtypeStruct((bn, seq, LANE), F32)],
        scratch_shapes=[pltpu.VMEM((tm, d), BF16)],
        compiler_params=_params(("parallel", "parallel", "arbitrary"), 56),
        name="in_proj",
    )(x, shift, scale, gain, w_main, w_ab)


def _outproj_kernel(oa_ref, ob_ref, oc_ref, x_ref, gate_ref, shift_ref, scale_ref, g_ref, w_ref, wr_ref, br_ref,
                    xn_ref, f_ref, lg_ref):
    acc, r0 = None, 0
    for o_ref in (oa_ref, ob_ref, oc_ref):
        r1 = r0 + o_ref.shape[2]
        part = jnp.dot(o_ref[0], w_ref[r0:r1, :], preferred_element_type=F32)
        acc = part if acc is None else acc + part
        r0 = r1
    xn = x_ref[0] + gate_ref[0] * acc
    xn_ref[0] = xn
    y = xn * lax.rsqrt(jnp.mean(xn * xn, -1, keepdims=True) + EPS) * g_ref[...]
    f = y * (1.0 + scale_ref[0]) + shift_ref[0]
    f_hi = f.astype(BF16)
    f_ref[0] = f_hi
    f_lo = (f - f_hi.astype(F32)).astype(BF16)
    hh_hl = jnp.dot(f_hi, wr_ref[...], preferred_element_type=F32)
    lh = jnp.dot(f_lo, wr_ref[:, :ROUTER_COLS], preferred_element_type=F32)
    lg_ref[0] = hh_hl[:, :ROUTER_COLS] + hh_hl[:, ROUTER_COLS:] + lh + br_ref[...]


def _outproj_call(os, x, gate, shift, scale, gain, w_out, w_r, b_r, tm):
    bn, seq, d = x.shape
    per_batch = gate.shape[0] > 1
    mod_map = (lambda b, i: (b, 0, 0)) if per_batch else (lambda b, i: (0, 0, 0))
    row = pl.BlockSpec((1, tm, d), lambda b, i: (b, i, 0))
    o_specs = [pl.BlockSpec((1, tm, o.shape[2]), lambda b, i: (b, i, 0)) for o in os]
    return pl.pallas_call(
        _outproj_kernel,
        grid=(bn, seq // tm),
        in_specs=[*o_specs, row,
                  pl.BlockSpec((1, 1, d), mod_map),
                  pl.BlockSpec((1, 1, d), mod_map),
                  pl.BlockSpec((1, 1, d), mod_map),
                  pl.BlockSpec((1, d), lambda b, i: (0, 0)),
                  pl.BlockSpec((d, d), lambda b, i: (0, 0)),
                  pl.BlockSpec((d, 2 * ROUTER_COLS), lambda b, i: (0, 0)),
                  pl.BlockSpec((1, ROUTER_COLS), lambda b, i: (0, 0))],
        out_specs=[row, row, pl.BlockSpec((1, tm, ROUTER_COLS), lambda b, i: (b, i, 0))],
        out_shape=[jax.ShapeDtypeStruct((bn, seq, d), F32),
                   jax.ShapeDtypeStruct((bn, seq, d), BF16),
                   jax.ShapeDtypeStruct((bn, seq, ROUTER_COLS), F32)],
        compiler_params=_params(("parallel", "parallel"), 56),
        name="out_proj",
    )(*os, x, gate, shift, scale, gain, w_out, w_r, b_r)


def _moe_kernel(te_ref, nu_ref, x_ref, wg_ref, wu_ref, wd_ref, o_ref):
    @pl.when(pl.program_id(0) < nu_ref[0])
    def _():
        x = x_ref[...]
        g = jnp.dot(x, wg_ref[0], preferred_element_type=F32)
        u = jnp.dot(x, wu_ref[0], preferred_element_type=F32)
        a = (g * jax.nn.sigmoid(g) * u).astype(BF16)
        o_ref[...] = jnp.dot(a, wd_ref[0], preferred_element_type=F32).astype(o_ref.dtype)


def _moe_call(tile_expert, n_used, buf, w_gate, w_up, w_down, tm):
    p, d = buf.shape
    f = w_gate.shape[-1]
    grid_spec = pltpu.PrefetchScalarGridSpec(
        num_scalar_prefetch=2,
        grid=(p // tm,),
        in_specs=[pl.BlockSpec((tm, d), lambda i, te, nu: (i, 0)),
                  pl.BlockSpec((1, d, f), lambda i, te, nu: (te[i], 0, 0)),
                  pl.BlockSpec((1, d, f), lambda i, te, nu: (te[i], 0, 0)),
                  pl.BlockSpec((1, f, d), lambda i, te, nu: (te[i], 0, 0))],
        out_specs=pl.BlockSpec((tm, d), lambda i, te, nu: (i, 0)),
    )
    return pl.pallas_call(
        _moe_kernel,
        grid_spec=grid_spec,
        out_shape=jax.ShapeDtypeStruct((p, d), BF16),
        compiler_params=_params(("arbitrary",), 60),
        name="moe_experts",
    )(tile_expert, n_used, buf, w_gate, w_up, w_down)


ROUTE_TM = 256


def _route_kernel(lg_ref, ids_ref, wts_ref, cnt_ref, carry_ref):
    @pl.when(pl.program_id(0) == 0)
    def _():
        carry_ref[...] = jnp.zeros_like(carry_ref)

    lg = lg_ref[...]
    tm = lg.shape[0]
    lane = lax.broadcasted_iota(jnp.int32, lg.shape, 1)
    big = jnp.int32(ROUTER_COLS)
    is_g = lane < N_GROUPS
    gmax = jnp.max(jnp.where(is_g, lg, -jnp.inf), -1, keepdims=True)
    g_sel = jnp.min(jnp.where(is_g & (lg == gmax), lane, big), -1, keepdims=True)
    p_g = 1.0 / jnp.sum(jnp.where(is_g, jnp.exp(lg - gmax), 0.0), -1, keepdims=True)
    lo = N_GROUPS + E_PER_GROUP * g_sel
    in_e = (lane >= lo) & (lane < lo + E_PER_GROUP)
    el = jnp.where(in_e, lg, -jnp.inf)
    v1 = jnp.max(el, -1, keepdims=True)
    i1 = jnp.min(jnp.where(el == v1, lane, big), -1, keepdims=True)
    el2 = jnp.where(lane == i1, -jnp.inf, el)
    v2 = jnp.max(el2, -1, keepdims=True)
    i2 = jnp.min(jnp.where(el2 == v2, lane, big), -1, keepdims=True)
    t = jnp.exp(v2 - v1)
    w1 = p_g / (1.0 + t)
    w2 = w1 * t
    e1 = i1 - N_GROUPS
    e2 = i2 - N_GROUPS
    hit1 = lane == e1
    hit2 = lane == e2
    onehot = jnp.where(hit1 | hit2, 1.0, 0.0)
    ri = lax.broadcasted_iota(jnp.int32, (tm, tm), 0)
    ci = lax.broadcasted_iota(jnp.int32, (tm, tm), 1)
    below = jnp.where(ci < ri, 1.0, 0.0).astype(BF16)
    before = carry_ref[...] + jnp.dot(below, onehot.astype(BF16), preferred_element_type=F32)
    r1 = jnp.sum(jnp.where(hit1, before, 0.0), -1, keepdims=True).astype(jnp.int32)
    r2 = jnp.sum(jnp.where(hit2, before, 0.0), -1, keepdims=True).astype(jnp.int32)
    ids_ref[...] = jnp.where(lane == 0, e1, jnp.where(lane == 1, e2, jnp.where(lane == 2, r1, jnp.where(lane == 3, r2, 0))))
    wts_ref[...] = jnp.where(lane == 0, w1, jnp.where(lane == 1, w2, 0.0))
    carry_ref[...] = carry_ref[...] + jnp.sum(onehot, 0, keepdims=True)
    cnt_ref[...] = carry_ref[...]


def _route_call(logits):
    t = logits.shape[0]
    tm = min(t, ROUTE_TM)
    row = pl.BlockSpec((tm, ROUTER_COLS), lambda i: (i, 0))
    return pl.pallas_call(
        _route_kernel,
        grid=(t // tm,),
        in_specs=[row],
        out_specs=[row, row, pl.BlockSpec((1, ROUTER_COLS), lambda i: (0, 0))],
        out_shape=[jax.ShapeDtypeStruct((t, ROUTER_COLS), jnp.int32),
                   jax.ShapeDtypeStruct((t, ROUTER_COLS), F32),
                   jax.ShapeDtypeStruct((1, ROUTER_COLS), F32)],
        scratch_shapes=[pltpu.VMEM((1, ROUTER_COLS), F32)],
        compiler_params=_params(("arbitrary",), 32),
        name="route",
    )(logits)


def _resid_kernel(x_ref, y0_ref, y1_ref, w_ref, gate_ref, g_ref, o_ref, *, final_norm):
    w = w_ref[0]
    y = y0_ref[0].astype(F32) * w[:, 0:1] + y1_ref[0].astype(F32) * w[:, 1:2]
    xn = x_ref[0] + gate_ref[0] * y
    if final_norm:
        xn = xn * lax.rsqrt(jnp.mean(xn * xn, -1, keepdims=True) + EPS) * g_ref[...]
    o_ref[0] = xn


def _resid_call(x, y0, y1, w, gate, gain, tm, final_norm):
    bn, seq, d = x.shape
    per_batch = gate.shape[0] > 1
    mod_map = (lambda b, i: (b, 0, 0)) if per_batch else (lambda b, i: (0, 0, 0))
    row = pl.BlockSpec((1, tm, d), lambda b, i: (b, i, 0))
    return pl.pallas_call(
        functools.partial(_resid_kernel, final_norm=final_norm),
        grid=(bn, seq // tm),
        in_specs=[row, row, row, pl.BlockSpec((1, tm, ROUTER_COLS), lambda b, i: (b, i, 0)),
                  pl.BlockSpec((1, 1, d), mod_map), pl.BlockSpec((1, d), lambda b, i: (0, 0))],
        out_specs=row,
        out_shape=jax.ShapeDtypeStruct((bn, seq, d), F32),
        compiler_params=_params(("parallel", "parallel"), 48),
        name="moe_residual",
    )(x, y0, y1, w, gate, gain)


def _softplus(x):
    return jnp.maximum(x, 0.0) + jnp.log1p(jnp.exp(-jnp.abs(x)))


def _gdn_prep(x_ref, w_ref, dst_ref, off, mode):
    x = x_ref[0].astype(F32)
    n = x.shape[0]
    rows = lax.broadcasted_iota(jnp.int32, (n, 1), 0)
    pad = CONV_W // 2
    acc = x * w_ref[pad:pad + 1, :]
    for o in range(-pad, pad + 1):
        if o == 0:
            continue
        xs = pltpu.roll(x, (-o) % n, 0)
        valid = (rows + o >= 0) if o < 0 else (rows + o < n)
        acc = acc + jnp.where(valid, xs, 0.0) * w_ref[o + pad:o + pad + 1, :]
    y = acc * jax.nn.sigmoid(acc)
    if mode != "v":
        y = y * lax.rsqrt(jnp.sum(y * y, -1, keepdims=True) + EPS)
    if mode == "q":
        y = y * DK_A ** -0.5
    dst_ref[off:off + n, :] = y


def _gdn_kernel(*refs, h_a, n_ctx, cast):
    if cast:
        refs = list(refs)
        cast_dst = refs.pop(19)
        cast_src = refs.pop(16)
        cast_dst[...] = cast_src[...].astype(BF16)
    (alog_ref, dtb_ref,
     qc_ref, kc_ref, vc_ref, zc_ref, ql_ref, kl_ref, vl_ref, zl_ref,
     wq_ref, wk_ref, wv_ref, grow_ref, bcol_ref, nw_ref,
     oc_ref, ol_ref,
     q_s, k_s, v_s, o_s, u_s, l1_s, l2_s, gl_s, gc_s) = refs
    C = CHUNK
    lc = qc_ref.shape[1]
    n_tot = u_s.shape[1]
    h = pl.program_id(1)

    _gdn_prep(qc_ref, wq_ref, q_s, 0, "q")
    _gdn_prep(kc_ref, wk_ref, k_s, 0, "k")
    _gdn_prep(vc_ref, wv_ref, v_s, 0, "v")
    _gdn_prep(ql_ref, wq_ref, q_s, lc, "q")
    _gdn_prep(kl_ref, wk_ref, k_s, lc, "k")
    _gdn_prep(vl_ref, wv_ref, v_s, lc, "v")

    NB = 4
    W = NB * C
    ii = lax.broadcasted_iota(jnp.int32, (C, W), 0)
    jl = lax.broadcasted_iota(jnp.int32, (C, W), 1)
    blk = jl // C
    jm = jl - blk * C
    eye4 = ii == jm
    tri4 = ((blk < 2) & (ii >= jm)) | ((blk >= 2) & (ii <= jm))
    eye4f = eye4.astype(F32)
    lane2 = lax.broadcasted_iota(jnp.int32, (C, 2 * C), 1)
    lo2 = lane2 < C

    i2 = lax.broadcasted_iota(jnp.int32, (2 * C, 2 * C), 0)
    j2 = lax.broadcasted_iota(jnp.int32, (2 * C, 2 * C), 1)
    same = (i2 // C) == (j2 // C)
    for d in range(2):
        a = jnp.exp(jnp.full((1, 2 * C), alog_ref[d * h_a + h], F32))
        g = -a * _softplus(grow_ref[0, 0, d] + dtb_ref[d * h_a + h])
        tri = same & ((i2 <= j2) if d == 0 else (i2 >= j2))
        gc_s[d] = jnp.dot(g, tri.astype(F32), preferred_element_type=F32, precision=HIGHEST)

    def block_diag(x):
        return jnp.concatenate([jnp.where(blk == b, x, 0.0) for b in range(NB)], 0).astype(BF16)

    def pair_setup(p):
        r0 = pl.multiple_of(p * 2 * C, 2 * C)
        q2 = q_s[pl.ds(r0, 2 * C), :]
        k2 = k_s[pl.ds(r0, 2 * C), :]
        v2 = v_s[pl.ds(r0, 2 * C), :]
        k16 = k2.astype(BF16)
        gk = lax.dot_general(k16, k16, _NT, preferred_element_type=F32)
        gq = lax.dot_general(q2.astype(BF16), k16, _NT, preferred_element_type=F32)
        kk2 = jnp.where(lo2, gk[:C], gk[C:])
        qk2 = jnp.where(lo2, gq[:C], gq[C:])
        kk4 = jnp.concatenate([kk2, kk2], 1)
        qk4 = jnp.concatenate([qk2, qk2], 1)
        grow = [gc_s[d, pl.ds(p, 1), :] for d in range(2)]
        rowb = jnp.concatenate([jnp.broadcast_to(g, (C, 2 * C)) for g in grow], 1)
        diag = jnp.where(eye4, rowb, 0.0)
        gcol, bcol, tot = [], [], []
        for b in range(NB):
            d, par = b // 2, b % 2
            half = diag[:, d * 2 * C:(d + 1) * 2 * C]
            gcol.append(jnp.sum(jnp.where(lo2 if par == 0 else ~lo2, half, 0.0), axis=1, keepdims=True))
            bcol.append(jax.nn.sigmoid(bcol_ref[0, 0, pl.ds(r0 + par * C, C), d:d + 1]))
            e = par * C + (C - 1 if d == 0 else 0)
            tot.append(grow[d][:, e:e + 1])

        def pack(cols):
            return jnp.where(blk == 0, cols[0], jnp.where(blk == 1, cols[1], jnp.where(blk == 2, cols[2], cols[3])))

        gcc4 = pack(gcol)
        dec4 = jnp.where(tri4, jnp.exp(jnp.where(tri4, gcc4 - rowb, 0.0)), 0.0)
        m0 = jnp.where(eye4, 0.0, -(kk4 * pack(bcol) * dec4))
        return m0, (p, q2, k2, v2, qk4 * dec4, gcol, bcol, tot)

    def pair_finish(tinv4, aux):
        p, q2, k2, v2, attn4, gcol, bcol, tot = aux
        eg = [jnp.exp(g) for g in gcol]
        ks = (k2[:C], k2[C:])
        vs = (v2[:C], v2[C:])
        qs = (q2[:C], q2[C:])
        rhs = jnp.concatenate([jnp.concatenate([vs[b % 2] * bcol[b], ks[b % 2] * (bcol[b] * eg[b])], 1)
                               for b in range(NB)], 0).astype(BF16)
        uw = jnp.dot(block_diag(tinv4), rhs, preferred_element_type=F32)
        for b in range(NB):
            d, par = b // 2, b % 2
            c = 2 * p + par
            u_s[d, c] = uw[b * C:(b + 1) * C, :DK_A]
            l1_s[d, c] = jnp.concatenate([uw[b * C:(b + 1) * C, DK_A:], qs[par] * eg[b]], 0).astype(BF16)
            gl_s[d, c] = jnp.broadcast_to(jnp.exp(tot[b]), (1, DK_A))
        for d in range(2):
            f = jnp.concatenate([jnp.exp(tot[2 * d + par] - gcol[2 * d + par]) for par in range(2)], 0)
            kdec_t = (k2 * f).T
            l2_s[d, p] = jnp.concatenate([attn4[:, d * 2 * C:(d + 1) * 2 * C], kdec_t], 0).astype(BF16)

    n_pairs = n_tot // 2
    group = max(g for g in (1, 2, 3) if n_pairs % g == 0)

    def group_terms(i, _):
        st = [pair_setup(i * group + g) for g in range(group)]
        m0s = [s[0] for s in st]
        rs = [jnp.dot(m0.astype(BF16), block_diag(m0), preferred_element_type=F32) for m0 in m0s]
        qms = [eye4f + m0 for m0 in m0s]
        for _ in range(4):
            qrs = [jnp.dot(jnp.concatenate([qm, r], 0).astype(BF16), block_diag(r), preferred_element_type=F32)
                   for qm, r in zip(qms, rs)]
            qms = [qm + qr[:C] for qm, qr in zip(qms, qrs)]
            rs = [qr[C:] for qr in qrs]
        tinvs = [qm + jnp.dot(qm.astype(BF16), block_diag(r), preferred_element_type=F32)
                 for qm, r in zip(qms, rs)]
        for tinv4, s in zip(tinvs, st):
            pair_finish(tinv4, s[1])
        return 0

    lax.fori_loop(0, n_pairs // group, group_terms, 0)

    o_s[...] = jnp.zeros_like(o_s)

    def scan_step(i, carry):
        c_b = jnp.where(i < n_ctx, n_ctx - 1 - i, n_ctx + n_tot - 1 - i)
        new = []
        row_half = lax.broadcasted_iota(jnp.int32, (2 * C, DK_A), 0) // C
        for d, c, s in ((0, i, carry[0]), (1, c_b, carry[1])):
            r1 = jnp.dot(l1_s[d, c], s.astype(BF16), preferred_element_type=F32)
            v_new = u_s[d, c] - r1[:C]
            keep = row_half == c % 2
            v_ext = jnp.where(keep, jnp.concatenate([v_new, v_new], 0), 0.0).astype(BF16)
            r2 = jnp.dot(l2_s[d, c // 2], v_ext, preferred_element_type=F32)
            rows = pl.ds(pl.multiple_of(c * C, C), C)
            o_s[rows, :] = o_s[rows, :] + r1[C:] + r2[:C]
            new.append(s * gl_s[d, c] + r2[C:])
        return tuple(new)

    zero = jnp.zeros((DK_A, DK_A), F32)
    lax.fori_loop(0, n_tot, scan_step, (zero, zero))

    def finish(z_ref, o_ref, off):
        n = z_ref.shape[1]
        o = o_s[off:off + n, :]
        o = o * lax.rsqrt(jnp.mean(o * o, -1, keepdims=True) + EPS) * nw_ref[...]
        z = z_ref[0].astype(F32)
        o_ref[0] = (o * (z * jax.nn.sigmoid(z))).astype(o_ref.dtype)

    finish(zc_ref, oc_ref, 0)
    finish(zl_ref, ol_ref, lc)


def _gdn_call(p_c, p_l, ab_c, ab_l, conv_w, a_log, dt_bias, norm_w, h_a, cast_src=None):
    B, L, _ = p_l.shape
    lc = p_c.shape[1]
    t = lc + L
    n_tot, n_ctx = t // CHUNK, lc // CHUNK
    d_a = h_a * DK_A
    ab = jnp.concatenate([ab_c, ab_l], 1)[..., :4 * h_a].reshape(B, t, 2, 2, h_a)
    assert n_tot % 2 == 0
    g_row = ab[:, :, :, 0].transpose(0, 3, 2, 1).reshape(B, h_a, 2, n_tot // 2, 2 * CHUNK)
    b_col = ab[:, :, :, 1].transpose(0, 3, 1, 2)

    def col(k, n):
        return pl.BlockSpec((1, n, DK_A), lambda b, h, *_: (b, 0, k * h_a + h))

    def tap(k):
        return pl.BlockSpec((CONV_W, DK_A), lambda b, h, *_: (0, k * h_a + h))

    in_specs = [col(0, lc), col(1, lc), col(2, lc), col(3, lc),
                col(0, L), col(1, L), col(2, L), col(3, L),
                tap(0), tap(1), tap(2),
                pl.BlockSpec((1, 1, 2, n_tot // 2, 2 * CHUNK), lambda b, h, *_: (b, h, 0, 0, 0)),
                pl.BlockSpec((1, 1, t, 2), lambda b, h, *_: (b, h, 0, 0)),
                pl.BlockSpec((1, DK_A), lambda b, h, *_: (0, 0))]
    out_specs = [pl.BlockSpec((1, lc, DK_A), lambda b, h, *_: (b, 0, h)),
                 pl.BlockSpec((1, L, DK_A), lambda b, h, *_: (b, 0, h))]
    out_shape = [jax.ShapeDtypeStruct((B, lc, d_a), BF16), jax.ShapeDtypeStruct((B, L, d_a), BF16)]
    extra = ()
    if cast_src is not None:
        src, dst, shape = _cast_specs(cast_src, h_a)
        in_specs.append(src)
        out_specs.append(dst)
        out_shape.append(shape)
        extra = (cast_src[0],)
    grid_spec = pltpu.PrefetchScalarGridSpec(
        num_scalar_prefetch=2,
        grid=(B, h_a),
        in_specs=in_specs,
        out_specs=out_specs,
        scratch_shapes=[pltpu.VMEM((t, DK_A), F32), pltpu.VMEM((t, DK_A), F32), pltpu.VMEM((t, DK_A), F32),
                        pltpu.VMEM((t, DK_A), F32),
                        pltpu.VMEM((2, n_tot, CHUNK, DK_A), F32),
                        pltpu.VMEM((2, n_tot, 2 * CHUNK, DK_A), BF16),
                        pltpu.VMEM((2, n_tot // 2, CHUNK + DK_A, 2 * CHUNK), BF16),
                        pltpu.VMEM((2, n_tot, 1, DK_A), F32),
                        pltpu.VMEM((2, n_tot // 2, 2 * CHUNK), F32)],
    )
    return pl.pallas_call(
        functools.partial(_gdn_kernel, h_a=h_a, n_ctx=n_ctx, cast=cast_src is not None),
        grid_spec=grid_spec,
        out_shape=out_shape,
        compiler_params=_params(("parallel", "parallel"), 56),
        name="gdn",
    )(a_log.reshape(-1), dt_bias.reshape(-1), p_c, p_c, p_c, p_c, p_l, p_l, p_l, p_l,
      conv_w, conv_w, conv_w, g_row, b_col, norm_w.reshape(1, DK_A), *extra)


_NT = (((1,), (1,)), ((), ()))


def _softmax_rows(s, sink=None):
    m = jnp.max(s, -1, keepdims=True)
    if sink is not None:
        m = jnp.maximum(m, sink)
    e = jnp.exp(s - m)
    den = jnp.sum(e, -1, keepdims=True)
    if sink is not None:
        den = den + jnp.exp(sink - m)
    return (e * (1.0 / den)).astype(BF16)


def _gqa_kernel(sink_ref, q_ref, kvl_ref, kvc_ref, cos_ref, sa_ref, sb_ref, *rest, n_heads, latent):
    if len(rest) == 3:
        cast_src, o_ref, cast_dst = rest
        cast_dst[...] = cast_src[...].astype(BF16)
    else:
        o_ref, = rest
    G = n_heads // KV_B
    P = 2 * DH
    n = pl.program_id(1)
    seq = kvl_ref.shape[1]
    lc = kvc_ref.shape[1]

    def rope(x, r0):
        reps = x.shape[1] // P
        rows = pl.ds(pl.multiple_of(r0, QBLK), QBLK)
        c, a, b = [jnp.concatenate([t[rows, :]] * reps, 1) if reps > 1 else t[rows, :]
                   for t in (cos_ref, sa_ref, sb_ref)]
        w = x.shape[1]
        return x * c + pltpu.roll(x, w - DH // 4, 1) * a + pltpu.roll(x, DH // 4, 1) * b

    q = q_ref[0].astype(F32) * DH ** -0.5
    kc = kvc_ref[0][:, :P].astype(F32)
    vc = kvc_ref[0][:, P:].astype(F32)
    if latent:
        q = rope(q, n * QBLK)
        ks, vs = [], []
        for o in (-1, 0, 1):
            s0 = jnp.clip((n + o) * QBLK, 0, seq - QBLK)
            kv = kvl_ref[0, pl.ds(pl.multiple_of(s0, QBLK), QBLK), :].astype(F32)
            ks.append(rope(kv[:, :P], s0))
            vs.append(kv[:, P:])
        k = jnp.concatenate(ks + [kc], 0)
        v = jnp.concatenate(vs + [vc], 0)
        nk = 3 * QBLK + lc
        qi = lax.broadcasted_iota(jnp.int32, (QBLK, nk), 0)
        kj = lax.broadcasted_iota(jnp.int32, (QBLK, nk), 1)
        kpos = n * QBLK - QBLK + kj
        rel = kj - QBLK - qi
        valid = ((rel >= -WIN) & (rel <= WIN) & (kpos >= 0) & (kpos < seq)) | (kj >= 3 * QBLK)
    else:
        k, v = kc, vc
        valid = None
    q = q.astype(BF16)
    k_sw = pltpu.roll(k, DH, 1).astype(BF16)
    v_sw = pltpu.roll(v, DH, 1).astype(BF16)
    k = k.astype(BF16)
    v = v.astype(BF16)
    lane = lax.broadcasted_iota(jnp.int32, (QBLK, P), 1)
    zero = jnp.zeros((QBLK, P), BF16)
    scores = []
    for h in range(n_heads):
        qp = q[:, (h // 2) * P:(h // 2 + 1) * P]
        qh = jnp.where(lane < DH, qp, zero) if h % 2 == 0 else jnp.where(lane >= DH, qp, zero)
        straight = (h // G == h % 2)
        scores.append(lax.dot_general(qh, k if straight else k_sw, _NT, preferred_element_type=F32))
    probs = []
    for h, s in enumerate(scores):
        if valid is not None:
            s = jnp.where(valid, s, NEG_INF)
        probs.append(_softmax_rows(s, sink_ref[h]))
    outs = [jnp.dot(pr, v if (h // G == h % 2) else v_sw, preferred_element_type=F32) for h, pr in enumerate(probs)]
    for p in range(n_heads // 2):
        o_ref[0, :, p * P:(p + 1) * P] = jnp.where(lane < DH, outs[2 * p], outs[2 * p + 1]).astype(o_ref.dtype)


def _gqa_call(p_q, p_l, p_c, sink, tables, d_model, d_b, latent, cast_src=None):
    B, nq, _ = p_q.shape
    L, lc = p_l.shape[1], p_c.shape[1]
    kvw = 2 * KV_B * DH
    n_heads = d_b // DH
    assert (n_heads // KV_B) % 2 == 0
    in_specs = [pl.BlockSpec((1, QBLK, d_b), lambda b, n, *_: (b, n, 2 * d_model // d_b)),
                pl.BlockSpec((1, L, kvw), lambda b, n, *_: (b, 0, 3 * d_model // kvw)),
                pl.BlockSpec((1, lc, kvw), lambda b, n, *_: (b, 0, 3 * d_model // kvw)),
                pl.BlockSpec((L, 2 * DH), lambda b, n, *_: (0, 0)),
                pl.BlockSpec((L, 2 * DH), lambda b, n, *_: (0, 0)),
                pl.BlockSpec((L, 2 * DH), lambda b, n, *_: (0, 0))]
    out_specs = [pl.BlockSpec((1, QBLK, d_b), lambda b, n, *_: (b, n, 0))]
    out_shape = [jax.ShapeDtypeStruct((B, nq, d_b), BF16)]
    extra = ()
    if cast_src is not None:
        src, dst, shape = _cast_specs(cast_src, nq // QBLK)
        in_specs.append(src)
        out_specs.append(dst)
        out_shape.append(shape)
        extra = (cast_src[0],)
    grid_spec = pltpu.PrefetchScalarGridSpec(
        num_scalar_prefetch=1, grid=(B, nq // QBLK), in_specs=in_specs, out_specs=out_specs)
    return pl.pallas_call(
        functools.partial(_gqa_kernel, n_heads=n_heads, latent=latent),
        grid_spec=grid_spec,
        out_shape=out_shape,
        compiler_params=_params(("parallel", "parallel"), 48),
        name="window_gqa" if latent else "ctx_gqa",
    )(sink, p_q, p_l, p_c, *tables, *extra)


def _rope_tables(n_tokens):
    t = jnp.arange(n_tokens)
    row = (t // GRID_W).astype(F32)
    col = (t % GRID_W).astype(F32)
    half = DH // 2
    inv = ROPE_BASE ** (-jnp.arange(0, half, 2, dtype=F32) / half)
    ang_r = row[:, None] * inv[None]
    ang_c = col[:, None] * inv[None]
    ang = jnp.concatenate([ang_r, ang_r, ang_c, ang_c], -1)
    cos, sin = jnp.cos(ang), jnp.sin(ang)
    first = (jnp.arange(DH) % half) < half // 2
    sa = jnp.where(first, -sin, 0.0)
    sb = jnp.where(first, 0.0, sin)
    return tuple(jnp.concatenate([u, u], -1) for u in (cos, sa, sb))


def _na_kernel(q_ref, k_ref, v_ref, kc_ref, vc_ref, tbl_ref, *rest, n_heads, rows, latent):
    if len(rest) == 3:
        cast_src, o_ref, cast_dst = rest
        cast_dst[...] = cast_src[...].astype(BF16)
    else:
        o_ref, = rest
    P = 2 * DH
    r = pl.program_id(1)
    nq = q_ref.shape[1]
    nb = KH_MAX * GRID_W
    q = (q_ref[0].astype(F32) * DH ** -0.5).astype(BF16)
    lane = lax.broadcasted_iota(jnp.int32, (nq, P), 1)
    zero = jnp.zeros((nq, P), BF16)
    if latent:
        r0 = jnp.clip(r - KH_MAX // 2, 0, rows - KH_MAX)
        band = pl.ds(pl.multiple_of(r0 * GRID_W, GRID_W), nb)
        d0 = r0 - r + (KH_MAX - 1)
    scores, vals = [], []
    for p in range(n_heads // 2):
        cols = slice(p * P, (p + 1) * P)
        qp = q[:, cols]
        kc = kc_ref[0, :, cols]
        vc = vc_ref[0, :, cols]
        if latent:
            kk = jnp.concatenate([k_ref[0, band, cols], kc], 0)
            vals.append(jnp.concatenate([v_ref[0, band, cols], vc], 0))
        else:
            kk = kc
            vals.append(vc)
        for half in range(2):
            qh = jnp.where(lane < DH, qp, zero) if half == 0 else jnp.where(lane >= DH, qp, zero)
            scores.append(lax.dot_general(qh, kk, _NT, preferred_element_type=F32))
    probs = []
    for h, s in enumerate(scores):
        if latent:
            bias = jnp.concatenate([tbl_ref[h, d0 + 2 * t] for t in range(KH_MAX // 2)], 1)
            s = jnp.concatenate([s[:, :nb] + bias, s[:, nb:]], 1)
        probs.append(_softmax_rows(s))
    outs = [jnp.dot(pr, vals[h // 2], preferred_element_type=F32) for h, pr in enumerate(probs)]
    for p in range(n_heads // 2):
        o_ref[0, :, p * P:(p + 1) * P] = jnp.where(lane < DH, outs[2 * p], outs[2 * p + 1]).astype(o_ref.dtype)


def _na_bias_table(rpb):
    qc = jnp.arange(GRID_W)[:, None]
    kc = jnp.arange(GRID_W)[None, :]
    dc = jnp.clip(kc - qc + (KW - 1), 0, 2 * KW - 2)
    q_cs = jnp.clip(qc - KW // 2, 0, GRID_W - KW)
    in_win = (kc >= q_cs) & (kc < q_cs + KW)
    t = jnp.where(in_win[None, None], rpb[:, :, dc], NEG_INF)
    return jnp.concatenate([t[:, :-1], t[:, 1:]], -1)


def _na_call(p_q, p_l, p_c, tbl, d_model, d_c, latent, cast_src=None):
    B, nq, _ = p_q.shape
    L, lc = p_l.shape[1], p_c.shape[1]
    n_heads = d_c // DH
    rows = L // GRID_W
    assert rows >= KH_MAX and n_heads % 2 == 0
    cb = 2 * d_model // d_c
    qb = GRID_W
    in_specs = [pl.BlockSpec((1, qb, d_c), lambda b, r: (b, r, cb + 1)),
                pl.BlockSpec((1, L, d_c), lambda b, r: (b, 0, cb + 2)),
                pl.BlockSpec((1, L, d_c), lambda b, r: (b, 0, cb + 3)),
                pl.BlockSpec((1, lc, d_c), lambda b, r: (b, 0, cb + 2)),
                pl.BlockSpec((1, lc, d_c), lambda b, r: (b, 0, cb + 3)),
                pl.BlockSpec(tbl.shape, lambda b, r: (0, 0, 0, 0))]
    out_specs = [pl.BlockSpec((1, qb, d_c), lambda b, r: (b, r, 0))]
    out_shape = [jax.ShapeDtypeStruct((B, nq, d_c), BF16)]
    extra = ()
    if cast_src is not None:
        src, dst, shape = _cast_specs(cast_src, nq // qb)
        in_specs.append(src)
        out_specs.append(dst)
        out_shape.append(shape)
        extra = (cast_src[0],)
    return pl.pallas_call(
        functools.partial(_na_kernel, n_heads=n_heads, rows=rows, latent=latent),
        grid=(B, nq // qb),
        in_specs=in_specs,
        out_specs=out_specs,
        out_shape=out_shape,
        compiler_params=_params(("parallel", "parallel"), 48),
        name="nbr_attn" if latent else "ctx_attn",
    )(p_q, p_l, p_l, p_c, p_c, tbl, *extra)


def _l2norm(u):
    return u * lax.rsqrt(jnp.sum(u * u, -1, keepdims=True) + EPS)


def _short_conv(u, w):
    pad = CONV_W // 2
    L = u.shape[1]
    up = jnp.pad(u, ((0, 0), (pad, pad), (0, 0)))
    return sum(up[:, i:i + L] * w[i] for i in range(CONV_W))


def _gdn_prepare(qkv, ab, conv_w, a_log, dt_bias, h_a):
    B, L, _ = qkv.shape
    qkv = jax.nn.silu(_short_conv(qkv, conv_w))
    q, k, v = jnp.split(qkv, 3, -1)
    q = _l2norm(q.reshape(B, L, h_a, DK_A))
    k = _l2norm(k.reshape(B, L, h_a, DK_A))
    v = v.reshape(B, L, h_a, DK_A)
    ab = ab.reshape(B, L, 2, 2, h_a)
    g = -jnp.exp(a_log) * jax.nn.softplus(ab[:, :, :, 0] + dt_bias)
    beta = jax.nn.sigmoid(ab[:, :, :, 1])
    return q, k, v, g, beta


def _gdn_chunked(q, k, v, g, beta, s0):
    B, L, H, dk = k.shape
    dv = v.shape[-1]
    n = L // CHUNK

    def chunks(u):
        return u.reshape(B, n, CHUNK, H, -1).transpose(0, 3, 1, 2, 4)

    qc, kc, vc = chunks(q * dk ** -0.5), chunks(k), chunks(v)
    gc = g.reshape(B, n, CHUNK, H).transpose(0, 3, 1, 2)
    bc = beta.reshape(B, n, CHUNK, H).transpose(0, 3, 1, 2)
    gcum = jnp.cumsum(gc, -1)
    tril = jnp.tril(jnp.ones((CHUNK, CHUNK), bool))
    strict = jnp.tril(jnp.ones((CHUNK, CHUNK), bool), -1)
    diff = gcum[..., :, None] - gcum[..., None, :]
    decay = jnp.where(tril, jnp.exp(jnp.where(tril, diff, 0.0)), 0.0)
    kb = kc * bc[..., None]
    vb = vc * bc[..., None]
    a = jnp.where(strict, jnp.einsum('bhnid,bhnjd->bhnij', kb, kc) * decay, 0.0)
    eye = jnp.eye(CHUNK, dtype=F32)
    tinv = lax.linalg.triangular_solve(eye + a, jnp.broadcast_to(eye, a.shape),
                                       left_side=True, lower=True, unit_diagonal=True)
    u = tinv @ vb
    w = tinv @ (kb * jnp.exp(gcum)[..., None])
    attn = jnp.einsum('bhnid,bhnjd->bhnij', qc, kc) * decay
    qg = qc * jnp.exp(gcum)[..., None]
    kdec = kc * jnp.exp(gcum[..., -1:] - gcum)[..., None]
    glast = jnp.exp(gcum[..., -1])
    xs = tuple(jnp.moveaxis(t, 2, 0) for t in (u, w, attn, qg, kdec, glast))

    def step(s, inp):
        u_i, w_i, attn_i, qg_i, kdec_i, gl_i = inp
        v_new = u_i - w_i @ s
        o_i = qg_i @ s + attn_i @ v_new
        s = s * gl_i[..., None, None] + jnp.einsum('bhcd,bhce->bhde', kdec_i, v_new)
        return s, o_i

    s_final, o = lax.scan(step, s0, xs)
    o = jnp.moveaxis(o, 0, 2).transpose(0, 2, 3, 1, 4).reshape(B, L, H, dv)
    return o, s_final


def _gdn_output(o, z, norm_w, h_a):
    B, L = z.shape[:2]
    o = o * lax.rsqrt(jnp.mean(o * o, -1, keepdims=True) + EPS) * norm_w
    zg = jax.nn.silu(z.reshape(B, L, h_a, DK_A))
    return (o * zg).reshape(B, L, h_a * DK_A)


def _gated_deltanet(qkv_c, ab_c, z_c, qkv_l, ab_l, z_l, conv_w, a_log, dt_bias, norm_w, ctx_out, h_a):
    qc, kc, vc, gc, bc = _gdn_prepare(qkv_c, ab_c, conv_w, a_log, dt_bias, h_a)
    ql, kl, vl, gl, bl = _gdn_prepare(qkv_l, ab_l, conv_w, a_log, dt_bias, h_a)
    B = ql.shape[0]
    s0 = jnp.zeros((B, h_a, DK_A, DK_A), F32)
    o_c_sum, o_l_sum = 0.0, 0.0
    for d in range(2):
        seq_c = (qc, kc, vc, gc[:, :, d], bc[:, :, d])
        seq_l = (ql, kl, vl, gl[:, :, d], bl[:, :, d])
        if d == 1:
            seq_c = tuple(jnp.flip(t, 1) for t in seq_c)
            seq_l = tuple(jnp.flip(t, 1) for t in seq_l)
        o_c, s_c = _gdn_chunked(*seq_c, s0)
        o_l, _ = _gdn_chunked(*seq_l, s_c)
        if d == 1:
            o_c, o_l = jnp.flip(o_c, 1), jnp.flip(o_l, 1)
        o_c_sum = o_c_sum + o_c
        o_l_sum = o_l_sum + o_l
    o_lat = _gdn_output(o_l_sum, z_l, norm_w, h_a)
    o_ctx = _gdn_output(o_c_sum, z_c, norm_w, h_a) if ctx_out else None
    return o_ctx, o_lat


def _rope_tables_ref(n_tokens, head_dim):
    t = jnp.arange(n_tokens)
    row = (t // GRID_W).astype(F32)
    col = (t % GRID_W).astype(F32)
    half = head_dim // 2
    inv = ROPE_BASE ** (-jnp.arange(0, half, 2, dtype=F32) / half)
    ang_r = row[:, None] * inv[None]
    ang_c = col[:, None] * inv[None]
    ang = jnp.concatenate([ang_r, ang_r, ang_c, ang_c], -1)
    return jnp.cos(ang), jnp.sin(ang)


def _apply_rope(x, cos, sin):
    half = x.shape[-1] // 2

    def rot(u):
        u1, u2 = jnp.split(u, 2, -1)
        return jnp.concatenate([-u2, u1], -1)

    xr = jnp.concatenate([rot(x[..., :half]), rot(x[..., half:])], -1)
    return x * cos[None, :, None, :] + xr * sin[None, :, None, :]


def _softmax_with_sink(s, sink):
    sk = jnp.broadcast_to(sink, s.shape[:-1] + (1,))
    p = jax.nn.softmax(jnp.concatenate([s, sk], -1), -1)
    return p[..., :-1]


def _window_gqa(q_c, kv_c, q_l, kv_l, sink, cos, sin, ctx_out):
    B, L = q_l.shape[:2]
    Lc = q_c.shape[1]
    h_b = q_l.shape[-1] // DH
    G = h_b // KV_B
    n = L // QBLK
    J = 3 * QBLK
    scale = DH ** -0.5
    k_l, v_l = [u.reshape(B, L, KV_B, DH) for u in jnp.split(kv_l, 2, -1)]
    k_c, v_c = [u.reshape(B, Lc, KV_B, DH) for u in jnp.split(kv_c, 2, -1)]
    q_l = _apply_rope(q_l.reshape(B, L, h_b, DH), cos, sin)
    k_l = _apply_rope(k_l, cos, sin)
    sink = sink.reshape(KV_B, G, 1, 1)

    def band(u):
        up = jnp.pad(u, ((0, 0), (QBLK, QBLK), (0, 0), (0, 0))).reshape(B, n + 2, QBLK, KV_B, DH)
        return jnp.concatenate([up[:, :-2], up[:, 1:-1], up[:, 2:]], axis=2)

    k_w, v_w = band(k_l), band(v_l)
    qb = q_l.reshape(B, n, QBLK, KV_B, G, DH)
    qi = jnp.arange(QBLK)[:, None]
    kj = jnp.arange(J)[None]
    kpos = (jnp.arange(n) * QBLK)[:, None, None] - QBLK + kj[None]
    in_win = (jnp.abs(kj - QBLK - qi)[None] <= WIN) & (kpos >= 0) & (kpos < L)
    s_w = jnp.einsum('bnqkgd,bnjkd->bnkgqj', qb, k_w) * scale
    s_w = jnp.where(in_win[None, :, None, None], s_w, NEG_INF)
    s_x = jnp.einsum('bnqkgd,bckd->bnkgqc', qb, k_c) * scale
    p = _softmax_with_sink(jnp.concatenate([s_w, s_x], -1), sink)
    o = (jnp.einsum('bnkgqj,bnjkd->bnqkgd', p[..., :J], v_w)
         + jnp.einsum('bnkgqc,bckd->bnqkgd', p[..., J:], v_c))
    o_lat = o.reshape(B, L, h_b * DH)
    o_ctx = None
    if ctx_out:
        qc = q_c.reshape(B, Lc, KV_B, G, DH)
        s_c = jnp.einsum('bqkgd,bckd->bkgqc', qc, k_c) * scale
        p_c = _softmax_with_sink(s_c, sink)
        o_ctx = jnp.einsum('bkgqc,bckd->bqkgd', p_c, v_c).reshape(B, Lc, h_b * DH)
    return o_ctx, o_lat


def _neighbourhood_attn(q_c, kv_c, q_l, kv_l, rpb, ctx_out):
    B, L = q_l.shape[:2]
    Lc = q_c.shape[1]
    h_c = q_l.shape[-1] // DH
    rows = L // GRID_W
    kh = min(KH_MAX, rows)
    ncb = GRID_W // NA_QCOLS
    m = kh * NA_KCOLS
    scale = DH ** -0.5
    k_l, v_l = [u.reshape(B, L, h_c, DH) for u in jnp.split(kv_l, 2, -1)]
    k_c, v_c = [u.reshape(B, Lc, h_c, DH) for u in jnp.split(kv_c, 2, -1)]
    r = jnp.arange(rows)
    row_idx = jnp.clip(r - kh // 2, 0, rows - kh)[:, None] + jnp.arange(kh)[None]
    cb = jnp.arange(ncb)
    col_idx = (jnp.clip(cb * NA_QCOLS - KW // 2, 0, GRID_W - NA_KCOLS)[:, None]
               + jnp.arange(NA_KCOLS)[None])

    def gather_band(u):
        u = u.reshape(B, rows, GRID_W, h_c, DH)[:, row_idx]
        u = u[:, :, :, col_idx]
        return u.transpose(0, 1, 3, 2, 4, 5, 6).reshape(B, rows, ncb, m, h_c, DH)

    k_n, v_n = gather_band(k_l), gather_band(v_l)
    qn = q_l.reshape(B, rows, ncb, NA_QCOLS, h_c, DH)
    q_col = cb[:, None] * NA_QCOLS + jnp.arange(NA_QCOLS)[None]
    q_cs = jnp.clip(q_col - KW // 2, 0, GRID_W - KW)
    key_col = jnp.tile(col_idx, (1, kh))
    key_row = jnp.repeat(row_idx, NA_KCOLS, axis=1)
    in_win = ((key_col[:, None, :] >= q_cs[:, :, None])
              & (key_col[:, None, :] < q_cs[:, :, None] + KW))
    dr = key_row - r[:, None] + (KH_MAX - 1)
    dc = jnp.clip(key_col[:, None, :] - q_col[:, :, None] + (KW - 1), 0, 2 * KW - 2)
    bias = rpb[:, dr[:, None, None, :], dc[None]]
    s_nb = jnp.einsum('brcqhd,brcmhd->bhrcqm', qn, k_n) * scale + bias
    s_nb = jnp.where(in_win, s_nb, NEG_INF)
    s_x = jnp.einsum('brcqhd,bkhd->bhrcqk', qn, k_c) * scale
    p = jax.nn.softmax(jnp.concatenate([s_nb, s_x], -1), -1)
    o = (jnp.einsum('bhrcqm,brcmhd->brcqhd', p[..., :m], v_n)
         + jnp.einsum('bhrcqk,bkhd->brcqhd', p[..., m:], v_c))
    o_lat = o.reshape(B, L, h_c * DH)
    o_ctx = None
    if ctx_out:
        qc = q_c.reshape(B, Lc, h_c, DH)
        s_c = jnp.einsum('bqhd,bkhd->bhqk', qc, k_c) * scale
        p_c = jax.nn.softmax(s_c, -1)
        o_ctx = jnp.einsum('bhqk,bkhd->bqhd', p_c, v_c).reshape(B, Lc, h_c * DH)
    return o_ctx, o_lat


def _route(logits):
    T = logits.shape[0]
    g_logits = logits[:, :N_GROUPS]
    g_prob = jax.nn.softmax(g_logits, -1)
    g_sel = jnp.argmax(g_logits, -1)
    e_logits = logits[:, N_GROUPS:N_GROUPS + N_EXPERTS].reshape(T, N_GROUPS, E_PER_GROUP)
    e_in = jnp.take_along_axis(e_logits, g_sel[:, None, None], 1)[:, 0]
    top_v, top_i = lax.top_k(e_in, TOP_K)
    wts = jax.nn.softmax(top_v, -1) * jnp.take_along_axis(g_prob, g_sel[:, None], 1)
    expert = g_sel[:, None] * E_PER_GROUP + top_i
    return expert.astype(jnp.int32), wts


def _dispatch(expert, tm):
    T = expert.shape[0]
    S = T * TOP_K
    e_flat = expert.reshape(S)
    order = jnp.argsort(e_flat)
    e_sorted = e_flat[order]
    tok_sorted = order // TOP_K
    counts = jnp.bincount(e_flat, length=N_EXPERTS)
    starts = jnp.cumsum(counts) - counts
    padded = (counts + tm - 1) // tm * tm
    pends = jnp.cumsum(padded)
    pstarts = pends - padded
    dest = pstarts[e_sorted] + jnp.arange(S) - starts[e_sorted]
    n_tiles = -(-S // tm) + N_EXPERTS
    tile_expert = jnp.minimum(jnp.searchsorted(pends, jnp.arange(n_tiles) * tm, side='right'),
                              N_EXPERTS - 1).astype(jnp.int32)
    n_used = (pends[-1] // tm).astype(jnp.int32).reshape(1)
    pos = jnp.zeros((S,), jnp.int32).at[order].set(dest.astype(jnp.int32)).reshape(T, TOP_K)
    src_tok = jnp.zeros((n_tiles * tm,), jnp.int32).at[dest].set(tok_sorted.astype(jnp.int32))
    return tile_expert, n_used, pos, src_tok


def _moe(f, logits, w_gate, w_up, w_down, split=0):
    T, d = f.shape
    tm = MOE_TM
    ids, wts, cnt = _route_call(logits)
    expert, rank = ids[:, :TOP_K], ids[:, TOP_K:2 * TOP_K]
    counts = cnt[0, :N_EXPERTS].astype(jnp.int32)
    padded = (counts + tm - 1) // tm * tm
    pends = jnp.cumsum(padded)
    pstarts = pends - padded
    start = jnp.sum(jnp.where(expert[..., None] == jnp.arange(N_EXPERTS), pstarts, 0), -1)
    dest = start + rank
    n_tiles = -(-T * TOP_K // tm) + N_EXPERTS
    tile_expert = jnp.minimum(jnp.sum(pends[None, :] <= (jnp.arange(n_tiles) * tm)[:, None], -1),
                              N_EXPERTS - 1).astype(jnp.int32)
    n_used = (pends[-1] // tm).astype(jnp.int32).reshape(1)
    tok = jnp.broadcast_to(jnp.arange(T, dtype=jnp.int32)[:, None], (T, TOP_K))
    src_tok = (jnp.arange(n_tiles * tm, dtype=jnp.int32) % T).at[dest.reshape(-1)].set(tok.reshape(-1))
    out = _moe_call(tile_expert, n_used, f[src_tok], w_gate, w_up, w_down, tm)
    parts = [(dest[:split], wts[:split]), (dest[split:], wts[split:])]
    return [(out[d_[:, 0]], out[d_[:, 1]], w_) for d_, w_ in parts]


def kernel(x, c, ctx, c_ctx, w_mod, b_mod, norm_mix, norm_ffn, w_in, conv_a, a_log, dt_bias, gdn_norm, sink_b, rpb_c, w_out, w_router_group, b_router_group, w_router_expert, b_router_expert, w_gate, w_up, w_down, norm_final):
    B, L, D = x.shape
    Lc = ctx.shape[1]
    depth = w_mod.shape[0]
    d_a = D // 2
    d_b = D // 4
    d_c = D - d_a - d_b
    h_a = d_a // DK_A
    n_ab = 4 * h_a
    kvb = 2 * KV_B * DH
    o_ab = 4 * d_a
    n_main = 4 * d_a + d_b + kvb + 3 * d_c

    cc = jnp.zeros((MOD_ROWS, D), F32).at[:B].set(c).at[B].set(c_ctx)
    mod = _mod_call(cc, w_mod, b_mod).reshape(depth, MOD_ROWS, N_MOD, D)
    rope = _rope_tables(L)

    o_qb = o_ab + n_ab
    o_kvb = o_qb + d_b
    w_main, w_ab = _winprep_call(w_in, (o_ab, o_qb, o_kvb, o_kvb + kvb))

    tn_in = n_main
    for cand in (1280, 1024, 768, 640, 512, 384, 256, 128):
        if n_main % cand == 0:
            tn_in = cand
            break

    x_l, x_c = x, ctx
    for l in range(depth):
        last = l == depth - 1
        ctx_out = not last
        mod_l = mod[l, :B]
        mod_c = mod[l, B:B + 1]
        ml = [mod_l[:, i:i + 1] for i in range(N_MOD)]
        mc = [mod_c[:, i:i + 1] for i in range(N_MOD)]
        gain_mix = norm_mix[l].reshape(1, D)
        p_l, ab_l = _inproj_call(x_l, ml[0], ml[1], gain_mix, w_main, w_ab, l, min(L, 1024), tn_in)
        p_c, ab_c = _inproj_call(x_c, mc[0], mc[1], gain_mix, w_main, w_ab, l, min(Lc, 1024), tn_in)

        n_e, _, d_f = w_gate.shape[1:]
        src_d = _cast_chunks(w_down, l, B * h_a)
        src_u = _cast_chunks(w_up, l, B * (L // QBLK))
        src_g = _cast_chunks(w_gate, l, B * (L // GRID_W))
        oA_c, oA_l, *wd = _gdn_call(p_c, p_l, ab_c, ab_l, conv_a[l], a_log[l], dt_bias[l], gdn_norm[l], h_a, src_d)
        oB_l, *wu = _gqa_call(p_l, p_l, p_c, sink_b[l], rope, D, d_b, True, src_u)
        tbl = _na_bias_table(rpb_c[l])
        oC_l, *wg = _na_call(p_l, p_l, p_c, tbl, D, d_c, True, src_g)
        wd = wd[0].reshape(n_e, d_f, D) if wd else w_down[l].astype(BF16)
        wu = wu[0].reshape(n_e, D, d_f) if wu else w_up[l].astype(BF16)
        wg = wg[0].reshape(n_e, D, d_f) if wg else w_gate[l].astype(BF16)
        if ctx_out:
            oB_c, = _gqa_call(p_c, p_l, p_c, sink_b[l], rope, D, d_b, False)
            oC_c, = _na_call(p_c, p_l, p_c, tbl, D, d_c, False)

        w_o = w_out[l].astype(BF16)
        w_r = jnp.pad(jnp.concatenate([w_router_group[l], w_router_expert[l]], -1),
                      ((0, 0), (0, ROUTER_COLS - N_GROUPS - N_EXPERTS)))
        w_r_hi = w_r.astype(BF16)
        w_r = jnp.concatenate([w_r_hi, (w_r - w_r_hi.astype(F32)).astype(BF16)], -1)
        b_r = jnp.pad(jnp.concatenate([b_router_group[l], b_router_expert[l]], -1),
                      (0, ROUTER_COLS - N_GROUPS - N_EXPERTS)).reshape(1, ROUTER_COLS)
        gain_ffn = norm_ffn[l].reshape(1, D)
        x_l, f_l, lg_l = _outproj_call((oA_l, oB_l, oC_l), x_l, ml[2], ml[3], ml[4], gain_ffn, w_o, w_r, b_r,
                                       min(L, 256))
        if ctx_out:
            x_c, f_c, lg_c = _outproj_call((oA_c, oB_c, oC_c), x_c, mc[2], mc[3], mc[4], gain_ffn, w_o, w_r, b_r,
                                           min(Lc, 256))
            f_all = jnp.concatenate([f_c.reshape(B * Lc, D), f_l.reshape(B * L, D)], 0)
            lg_all = jnp.concatenate([lg_c.reshape(B * Lc, ROUTER_COLS), lg_l.reshape(B * L, ROUTER_COLS)], 0)
            y_c, y_l = _moe(f_all, lg_all, wg, wu, wd, B * Lc)
            y_c = [u.reshape(B, Lc, -1) for u in y_c]
            x_c = _resid_call(x_c, *y_c, mc[5], gain_ffn, min(Lc, 256), False)
        else:
            _, y_l = _moe(f_l.reshape(B * L, D), lg_l.reshape(B * L, ROUTER_COLS), wg, wu, wd)
        y_l = [u.reshape(B, L, -1) for u in y_l]
        x_l = _resid_call(x_l, *y_l, ml[5], norm_final.reshape(1, D), min(L, 256), last)
    return x_l
```

```python
import functools
import math

import jax
import jax.numpy as jnp
from jax import lax
from jax.experimental import pallas as pl
from jax.experimental.pallas import tpu as pltpu

F32 = jnp.float32
BF16 = jnp.bfloat16
HIGHEST = lax.Precision.HIGHEST

EPS = 1e-6
NEG_INF = -1e30
N_MOD = 6
GRID_W = 64
DH = 64
DK_A = 128
CONV_W = 5
CHUNK = 64
KV_B = 2
WIN = 128
QBLK = 128
ROPE_BASE = 10000.0
KH_MAX = 8
KW = 16
NA_QCOLS = 16
NA_KCOLS = 32
N_GROUPS = 4
E_PER_GROUP = 8
N_EXPERTS = N_GROUPS * E_PER_GROUP
TOP_K = 2
LANE = 128
SUBLANE = 8
MOD_ROWS = 16
ROUTER_COLS = LANE
MOE_TM = 256


def _params(sem, vmem_mb):
    return pltpu.CompilerParams(dimension_semantics=sem, vmem_limit_bytes=vmem_mb << 20)


def _mod_kernel(c_ref, w_ref, b_ref, o_ref):
    c = c_ref[...]
    s = c * jax.nn.sigmoid(c)
    o_ref[0] = jnp.dot(s, w_ref[0], preferred_element_type=F32, precision=HIGHEST) + b_ref[0]


def _mod_call(cc, w_mod, b_mod):
    depth, d, n = w_mod.shape
    tn = min(n, 1024)
    return pl.pallas_call(
        _mod_kernel,
        grid=(depth, n // tn),
        in_specs=[pl.BlockSpec((MOD_ROWS, d), lambda l, j: (0, 0)),
                  pl.BlockSpec((1, d, tn), lambda l, j: (l, 0, j)),
                  pl.BlockSpec((1, 1, tn), lambda l, j: (l, 0, j))],
        out_specs=pl.BlockSpec((1, MOD_ROWS, tn), lambda l, j: (l, 0, j)),
        out_shape=jax.ShapeDtypeStruct((depth, MOD_ROWS, n), F32),
        compiler_params=_params(("parallel", "parallel"), 48),
        name="mod_proj",
    )(cc, w_mod, b_mod.reshape(depth, 1, n))


def _winprep_kernel(w_ref, m_ref, ab_ref, *, bounds):
    o_ab, o_qb, o_kvb, o_qc = bounds
    x = w_ref[0]
    m_ref[0] = jnp.concatenate([x[:, :o_ab], x[:, o_qb:o_kvb], x[:, o_qc:], x[:, o_kvb:o_qc]], 1).astype(BF16)
    ab = x[:, o_ab:o_qb]
    ab_ref[0] = jnp.concatenate([ab, jnp.zeros((x.shape[0], LANE - ab.shape[1]), F32)], 1).astype(BF16)


def _winprep_call(w_in, bounds):
    depth, d, n = w_in.shape
    n_main = n - (bounds[1] - bounds[0])
    rt = min(d, 256)
    return pl.pallas_call(
        functools.partial(_winprep_kernel, bounds=bounds),
        grid=(depth, d // rt),
        in_specs=[pl.BlockSpec((1, rt, n), lambda l, i: (l, i, 0))],
        out_specs=[pl.BlockSpec((1, rt, n_main), lambda l, i: (l, i, 0)),
                   pl.BlockSpec((1, rt, LANE), lambda l, i: (l, i, 0))],
        out_shape=[jax.ShapeDtypeStruct((depth, d, n_main), BF16), jax.ShapeDtypeStruct((depth, d, LANE), BF16)],
        compiler_params=_params(("parallel", "parallel"), 48),
        name="w_in_prep",
    )(w_in)


def _cast_chunks(w, layer, n):
    depth, rows = w.shape[0], w.shape[1] * w.shape[2]
    if rows % n or (rows // n) % 16:
        return None
    return w.reshape(depth * n, rows // n, w.shape[3]), layer * n, n


def _cast_specs(cast, n_inner):
    chunks, first, n = cast
    blk = (1,) + chunks.shape[1:]
    src = pl.BlockSpec(blk, lambda b, j, *_: (first + b * n_inner + j, 0, 0))
    dst = pl.BlockSpec(blk, lambda b, j, *_: (b * n_inner + j, 0, 0))
    return src, dst, jax.ShapeDtypeStruct((n,) + chunks.shape[1:], BF16)


def _inproj_kernel(x_ref, shift_ref, scale_ref, g_ref, w_ref, wab_ref, o_ref, ab_ref, h_ref):
    @pl.when(pl.program_id(2) == 0)
    def _():
        x = x_ref[0]
        y = x * lax.rsqrt(jnp.mean(x * x, -1, keepdims=True) + EPS) * g_ref[...]
        h = (y * (1.0 + scale_ref[0]) + shift_ref[0]).astype(BF16)
        h_ref[...] = h
        ab_ref[0] = jnp.dot(h, wab_ref[0], preferred_element_type=F32)

    o_ref[0] = jnp.dot(h_ref[...], w_ref[0], preferred_element_type=F32).astype(o_ref.dtype)


def _inproj_call(x, shift, scale, gain, w_main, w_ab, layer, tm, tn):
    bn, seq, d = x.shape
    n = w_main.shape[2]
    per_batch = shift.shape[0] > 1
    mod_map = (lambda b, i, j: (b, 0, 0)) if per_batch else (lambda b, i, j: (0, 0, 0))
    return pl.pallas_call(
        _inproj_kernel,
        grid=(bn, seq // tm, n // tn),
        in_specs=[pl.BlockSpec((1, tm, d), lambda b, i, j: (b, i, 0)),
                  pl.BlockSpec((1, 1, d), mod_map),
                  pl.BlockSpec((1, 1, d), mod_map),
                  pl.BlockSpec((1, d), lambda b, i, j: (0, 0)),
                  pl.BlockSpec((1, d, tn), lambda b, i, j: (layer, 0, j)),
                  pl.BlockSpec((1, d, LANE), lambda b, i, j: (layer, 0, 0))],
        out_specs=[pl.BlockSpec((1, tm, tn), lambda b, i, j: (b, i, j)),
                   pl.BlockSpec((1, tm, LANE), lambda b, i, j: (b, i, 0))],
        out_shape=[jax.ShapeDtypeStruct((bn, seq, n), BF16),
                   jax.ShapeDtypeStruct((bn, seq, LANE), F32)],
        scratch_shapes=[pltpu.VMEM((tm, d), BF16)],
        compiler_params=_params(("parallel", "parallel", "arbitrary"), 56),
        name="in_proj",
    )(x, shift, scale, gain, w_main, w_ab)


def _outproj_kernel(oa_ref, ob_ref, oc_ref, x_ref, gate_ref, shift_ref, scale_ref, g_ref, w_ref, wr_ref, br_ref,
                    xn_ref, f_ref, lg_ref):
    acc, r0 = None, 0
    for o_ref in (oa_ref, ob_ref, oc_ref):
        r1 = r0 + o_ref.shape[2]
        part = jnp.dot(o_ref[0], w_ref[r0:r1, :], preferred_element_type=F32)
        acc = part if acc is None else acc + part
        r0 = r1
    xn = x_ref[0] + gate_ref[0] * acc
    xn_ref[0] = xn
    y = xn * lax.rsqrt(jnp.mean(xn * xn, -1, keepdims=True) + EPS) * g_ref[...]
    f = y * (1.0 + scale_ref[0]) + shift_ref[0]
    f_hi = f.astype(BF16)
    f_ref[0] = f_hi
    f_lo = (f - f_hi.astype(F32)).astype(BF16)
    hh_hl = jnp.dot(f_hi, wr_ref[...], preferred_element_type=F32)
    lh = jnp.dot(f_lo, wr_ref[:, :ROUTER_COLS], preferred_element_type=F32)
    lg_ref[0] = hh_hl[:, :ROUTER_COLS] + hh_hl[:, ROUTER_COLS:] + lh + br_ref[...]


def _outproj_call(os, x, gate, shift, scale, gain, w_out, w_r, b_r, tm):
    bn, seq, d = x.shape
    per_batch = gate.shape[0] > 1
    mod_map = (lambda b, i: (b, 0, 0)) if per_batch else (lambda b, i: (0, 0, 0))
    row = pl.BlockSpec((1, tm, d), lambda b, i: (b, i, 0))
    o_specs = [pl.BlockSpec((1, tm, o.shape[2]), lambda b, i: (b, i, 0)) for o in os]
    return pl.pallas_call(
        _outproj_kernel,
        grid=(bn, seq // tm),
        in_specs=[*o_specs, row,
                  pl.BlockSpec((1, 1, d), mod_map),
                  pl.BlockSpec((1, 1, d), mod_map),
                  pl.BlockSpec((1, 1, d), mod_map),
                  pl.BlockSpec((1, d), lambda b, i: (0, 0)),
                  pl.BlockSpec((d, d), lambda b, i: (0, 0)),
                  pl.BlockSpec((d, 2 * ROUTER_COLS), lambda b, i: (0, 0)),
                  pl.BlockSpec((1, ROUTER_COLS), lambda b, i: (0, 0))],
        out_specs=[row, row, pl.BlockSpec((1, tm, ROUTER_COLS), lambda b, i: (b, i, 0))],
        out_shape=[jax.ShapeDtypeStruct((bn, seq, d), F32),
                   jax.ShapeDtypeStruct((bn, seq, d), BF16),
                   jax.ShapeDtypeStruct((bn, seq, ROUTER_COLS), F32)],
        compiler_params=_params(("parallel", "parallel"), 56),
        name="out_proj",
    )(*os, x, gate, shift, scale, gain, w_out, w_r, b_r)


def _moe_kernel(te_ref, nu_ref, x_ref, wg_ref, wu_ref, wd_ref, o_ref):
    @pl.when(pl.program_id(0) < nu_ref[0])
    def _():
        x = x_ref[...]
        g = jnp.dot(x, wg_ref[0], preferred_element_type=F32)
        u = jnp.dot(x, wu_ref[0], preferred_element_type=F32)
        a = (g * jax.nn.sigmoid(g) * u).astype(BF16)
        o_ref[...] = jnp.dot(a, wd_ref[0], preferred_element_type=F32).astype(o_ref.dtype)


def _moe_call(tile_expert, n_used, buf, w_gate, w_up, w_down, tm):
    p, d = buf.shape
    f = w_gate.shape[-1]
    grid_spec = pltpu.PrefetchScalarGridSpec(
        num_scalar_prefetch=2,
        grid=(p // tm,),
        in_specs=[pl.BlockSpec((tm, d), lambda i, te, nu: (i, 0)),
                  pl.BlockSpec((1, d, f), lambda i, te, nu: (te[i], 0, 0)),
                  pl.BlockSpec((1, d, f), lambda i, te, nu: (te[i], 0, 0)),
                  pl.BlockSpec((1, f, d), lambda i, te, nu: (te[i], 0, 0))],
        out_specs=pl.BlockSpec((tm, d), lambda i, te, nu: (i, 0)),
    )
    return pl.pallas_call(
        _moe_kernel,
        grid_spec=grid_spec,
        out_shape=jax.ShapeDtypeStruct((p, d), BF16),
        compiler_params=_params(("arbitrary",), 60),
        name="moe_experts",
    )(tile_expert, n_used, buf, w_gate, w_up, w_down)


ROUTE_TM = 256


def _route_kernel(lg_ref, ids_ref, wts_ref, cnt_ref, carry_ref):
    @pl.when(pl.program_id(0) == 0)
    def _():
        carry_ref[...] = jnp.zeros_like(carry_ref)

    lg = lg_ref[...]
    tm = lg.shape[0]
    lane = lax.broadcasted_iota(jnp.int32, lg.shape, 1)
    big = jnp.int32(ROUTER_COLS)
    is_g = lane < N_GROUPS
    gmax = jnp.max(jnp.where(is_g, lg, -jnp.inf), -1, keepdims=True)
    g_sel = jnp.min(jnp.where(is_g & (lg == gmax), lane, big), -1, keepdims=True)
    p_g = 1.0 / jnp.sum(jnp.where(is_g, jnp.exp(lg - gmax), 0.0), -1, keepdims=True)
    lo = N_GROUPS + E_PER_GROUP * g_sel
    in_e = (lane >= lo) & (lane < lo + E_PER_GROUP)
    el = jnp.where(in_e, lg, -jnp.inf)
    v1 = jnp.max(el, -1, keepdims=True)
    i1 = jnp.min(jnp.where(el == v1, lane, big), -1, keepdims=True)
    el2 = jnp.where(lane == i1, -jnp.inf, el)
    v2 = jnp.max(el2, -1, keepdims=True)
    i2 = jnp.min(jnp.where(el2 == v2, lane, big), -1, keepdims=True)
    t = jnp.exp(v2 - v1)
    w1 = p_g / (1.0 + t)
    w2 = w1 * t
    e1 = i1 - N_GROUPS
    e2 = i2 - N_GROUPS
    hit1 = lane == e1
    hit2 = lane == e2
    onehot = jnp.where(hit1 | hit2, 1.0, 0.0)
    ri = lax.broadcasted_iota(jnp.int32, (tm, tm), 0)
    ci = lax.broadcasted_iota(jnp.int32, (tm, tm), 1)
    below = jnp.where(ci < ri, 1.0, 0.0).astype(BF16)
    before = carry_ref[...] + jnp.dot(below, onehot.astype(BF16), preferred_element_type=F32)
    r1 = jnp.sum(jnp.where(hit1, before, 0.0), -1, keepdims=True).astype(jnp.int32)
    r2 = jnp.sum(jnp.where(hit2, before, 0.0), -1, keepdims=True).astype(jnp.int32)
    ids_ref[...] = jnp.where(lane == 0, e1, jnp.where(lane == 1, e2, jnp.where(lane == 2, r1, jnp.where(lane == 3, r2, 0))))
    wts_ref[...] = jnp.where(lane == 0, w1, jnp.where(lane == 1, w2, 0.0))
    carry_ref[...] = carry_ref[...] + jnp.sum(onehot, 0, keepdims=True)
    cnt_ref[...] = carry_ref[...]


def _route_call(logits):
    t = logits.shape[0]
    tm = min(t, ROUTE_TM)
    row = pl.BlockSpec((tm, ROUTER_COLS), lambda i: (i, 0))
    return pl.pallas_call(
        _route_kernel,
        grid=(t // tm,),
        in_specs=[row],
        out_specs=[row, row, pl.BlockSpec((1, ROUTER_COLS), lambda i: (0, 0))],
        out_shape=[jax.ShapeDtypeStruct((t, ROUTER_COLS), jnp.int32),
                   jax.ShapeDtypeStruct((t, ROUTER_COLS), F32),
                   jax.ShapeDtypeStruct((1, ROUTER_COLS), F32)],
        scratch_shapes=[pltpu.VMEM((1, ROUTER_COLS), F32)],
        compiler_params=_params(("arbitrary",), 32),
        name="route",
    )(logits)


def _resid_kernel(x_ref, y0_ref, y1_ref, w_ref, gate_ref, g_ref, o_ref, *, final_norm):
    w = w_ref[0]
    y = y0_ref[0].astype(F32) * w[:, 0:1] + y1_ref[0].astype(F32) * w[:, 1:2]
    xn = x_ref[0] + gate_ref[0] * y
    if final_norm:
        xn = xn * lax.rsqrt(jnp.mean(xn * xn, -1, keepdims=True) + EPS) * g_ref[...]
    o_ref[0] = xn


def _resid_call(x, y0, y1, w, gate, gain, tm, final_norm):
    bn, seq, d = x.shape
    per_batch = gate.shape[0] > 1
    mod_map = (lambda b, i: (b, 0, 0)) if per_batch else (lambda b, i: (0, 0, 0))
    row = pl.BlockSpec((1, tm, d), lambda b, i: (b, i, 0))
    return pl.pallas_call(
        functools.partial(_resid_kernel, final_norm=final_norm),
        grid=(bn, seq // tm),
        in_specs=[row, row, row, pl.BlockSpec((1, tm, ROUTER_COLS), lambda b, i: (b, i, 0)),
                  pl.BlockSpec((1, 1, d), mod_map), pl.BlockSpec((1, d), lambda b, i: (0, 0))],
        out_specs=row,
        out_shape=jax.ShapeDtypeStruct((bn, seq, d), F32),
        compiler_params=_params(("parallel", "parallel"), 48),
        name="moe_residual",
    )(x, y0, y1, w, gate, gain)


def _softplus(x):
    return jnp.maximum(x, 0.0) + jnp.log1p(jnp.exp(-jnp.abs(x)))


def _gdn_prep(x_ref, w_ref, dst_ref, off, mode):
    x = x_ref[0].astype(F32)
    n = x.shape[0]
    rows = lax.broadcasted_iota(jnp.int32, (n, 1), 0)
    pad = CONV_W // 2
    acc = x * w_ref[pad:pad + 1, :]
    for o in range(-pad, pad + 1):
        if o == 0:
            continue
        xs = pltpu.roll(x, (-o) % n, 0)
        valid = (rows + o >= 0) if o < 0 else (rows + o < n)
        acc = acc + jnp.where(valid, xs, 0.0) * w_ref[o + pad:o + pad + 1, :]
    y = acc * jax.nn.sigmoid(acc)
    if mode != "v":
        y = y * lax.rsqrt(jnp.sum(y * y, -1, keepdims=True) + EPS)
    if mode == "q":
        y = y * DK_A ** -0.5
    dst_ref[off:off + n, :] = y


def _gdn_kernel(*refs, h_a, n_ctx, cast):
    if cast:
        refs = list(refs)
        cast_dst = refs.pop(19)
        cast_src = refs.pop(16)
        cast_dst[...] = cast_src[...].astype(BF16)
    (alog_ref, dtb_ref,
     qc_ref, kc_ref, vc_ref, zc_ref, ql_ref, kl_ref, vl_ref, zl_ref,
     wq_ref, wk_ref, wv_ref, grow_ref, bcol_ref, nw_ref,
     oc_ref, ol_ref,
     q_s, k_s, v_s, o_s, u_s, l1_s, l2_s, gl_s, gc_s) = refs
    C = CHUNK
    lc = qc_ref.shape[1]
    n_tot = u_s.shape[1]
    h = pl.program_id(1)

    _gdn_prep(qc_ref, wq_ref, q_s, 0, "q")
    _gdn_prep(kc_ref, wk_ref, k_s, 0, "k")
    _gdn_prep(vc_ref, wv_ref, v_s, 0, "v")
    _gdn_prep(ql_ref, wq_ref, q_s, lc, "q")
    _gdn_prep(kl_ref, wk_ref, k_s, lc, "k")
    _gdn_prep(vl_ref, wv_ref, v_s, lc, "v")

    NB = 4
    W = NB * C
    ii = lax.broadcasted_iota(jnp.int32, (C, W), 0)
    jl = lax.broadcasted_iota(jnp.int32, (C, W), 1)
    blk = jl // C
    jm = jl - blk * C
    eye4 = ii == jm
    tri4 = ((blk < 2) & (ii >= jm)) | ((blk >= 2) & (ii <= jm))
    eye4f = eye4.astype(F32)
    lane2 = lax.broadcasted_iota(jnp.int32, (C, 2 * C), 1)
    lo2 = lane2 < C

    i2 = lax.broadcasted_iota(jnp.int32, (2 * C, 2 * C), 0)
    j2 = lax.broadcasted_iota(jnp.int32, (2 * C, 2 * C), 1)
    same = (i2 // C) == (j2 // C)
    for d in range(2):
        a = jnp.exp(jnp.full((1, 2 * C), alog_ref[d * h_a + h], F32))
        g = -a * _softplus(grow_ref[0, 0, d] + dtb_ref[d * h_a + h])
        tri = same & ((i2 <= j2) if d == 0 else (i2 >= j2))
        gc_s[d] = jnp.dot(g, tri.astype(F32), preferred_element_type=F32, precision=HIGHEST)

    def block_diag(x):
        return jnp.concatenate([jnp.where(blk == b, x, 0.0) for b in range(NB)], 0).astype(BF16)

    def pair_setup(p):
        r0 = pl.multiple_of(p * 2 * C, 2 * C)
        q2 = q_s[pl.ds(r0, 2 * C), :]
        k2 = k_s[pl.ds(r0, 2 * C), :]
        v2 = v_s[pl.ds(r0, 2 * C), :]
        k16 = k2.astype(BF16)
        gk = lax.dot_general(k16, k16, _NT, preferred_element_type=F32)
        gq = lax.dot_general(q2.astype(BF16), k16, _NT, preferred_element_type=F32)
        kk2 = jnp.where(lo2, gk[:C], gk[C:])
        qk2 = jnp.where(lo2, gq[:C], gq[C:])
        kk4 = jnp.concatenate([kk2, kk2], 1)
        qk4 = jnp.concatenate([qk2, qk2], 1)
        grow = [gc_s[d, pl.ds(p, 1), :] for d in range(2)]
        rowb = jnp.concatenate([jnp.broadcast_to(g, (C, 2 * C)) for g in grow], 1)
        diag = jnp.where(eye4, rowb, 0.0)
        gcol, bcol, tot = [], [], []
        for b in range(NB):
            d, par = b // 2, b % 2
            half = diag[:, d * 2 * C:(d + 1) * 2 * C]
            gcol.append(jnp.sum(jnp.where(lo2 if par == 0 else ~lo2, half, 0.0), axis=1, keepdims=True))
            bcol.append(jax.nn.sigmoid(bcol_ref[0, 0, pl.ds(r0 + par * C, C), d:d + 1]))
            e = par * C + (C - 1 if d == 0 else 0)
            tot.append(grow[d][:, e:e + 1])

        def pack(cols):
            return jnp.where(blk == 0, cols[0], jnp.where(blk == 1, cols[1], jnp.where(blk == 2, cols[2], cols[3])))

        gcc4 = pack(gcol)
        dec4 = jnp.where(tri4, jnp.exp(jnp.where(tri4, gcc4 - rowb, 0.0)), 0.0)
        m0 = jnp.where(eye4, 0.0, -(kk4 * pack(bcol) * dec4))
        return m0, (p, q2, k2, v2, qk4 * dec4, gcol, bcol, tot)

    def pair_finish(tinv4, aux):
        p, q2, k2, v2, attn4, gcol, bcol, tot = aux
        eg = [jnp.exp(g) for g in gcol]
        ks = (k2[:C], k2[C:])
        vs = (v2[:C], v2[C:])
        qs = (q2[:C], q2[C:])
        rhs = jnp.concatenate([jnp.concatenate([vs[b % 2] * bcol[b], ks[b % 2] * (bcol[b] * eg[b])], 1)
                               for b in range(NB)], 0).astype(BF16)
        uw = jnp.dot(block_diag(tinv4), rhs, preferred_element_type=F32)
        for b in range(NB):
            d, par = b // 2, b % 2
            c = 2 * p + par
            u_s[d, c] = uw[b * C:(b + 1) * C, :DK_A]
            l1_s[d, c] = jnp.concatenate([uw[b * C:(b + 1) * C, DK_A:], qs[par] * eg[b]], 0).astype(BF16)
            gl_s[d, c] = jnp.broadcast_to(jnp.exp(tot[b]), (1, DK_A))
        for d in range(2):
            f = jnp.concatenate([jnp.exp(tot[2 * d + par] - gcol[2 * d + par]) for par in range(2)], 0)
            kdec_t = (k2 * f).T
            l2_s[d, p] = jnp.concatenate([attn4[:, d * 2 * C:(d + 1) * 2 * C], kdec_t], 0).astype(BF16)

    n_pairs = n_tot // 2
    group = max(g for g in (1, 2, 3, 6, 9) if n_pairs % g == 0)

    def group_terms(i, _):
        st = [pair_setup(i * group + g) for g in range(group)]
        m0s = [s[0] for s in st]
        rs = [jnp.dot(m0.astype(BF16), block_diag(m0), preferred_element_type=F32) for m0 in m0s]
        qms = [eye4f + m0 for m0 in m0s]
        for _ in range(4):
            qrs = [jnp.dot(jnp.concatenate([qm, r], 0).astype(BF16), block_diag(r), preferred_element_type=F32)
                   for qm, r in zip(qms, rs)]
            qms = [qm + qr[:C] for qm, qr in zip(qms, qrs)]
            rs = [qr[C:] for qr in qrs]
        tinvs = [qm + jnp.dot(qm.astype(BF16), block_diag(r), preferred_element_type=F32)
                 for qm, r in zip(qms, rs)]
        for tinv4, s in zip(tinvs, st):
            pair_finish(tinv4, s[1])
        return 0

    lax.fori_loop(0, n_pairs // group, group_terms, 0)

    o_s[...] = jnp.zeros_like(o_s)

    def scan_step(i, carry):
        c_b = jnp.where(i < n_ctx, n_ctx - 1 - i, n_ctx + n_tot - 1 - i)
        new = []
        row_half = lax.broadcasted_iota(jnp.int32, (2 * C, DK_A), 0) // C
        for d, c, s in ((0, i, carry[0]), (1, c_b, carry[1])):
            r1 = jnp.dot(l1_s[d, c], s.astype(BF16), preferred_element_type=F32)
            v_new = u_s[d, c] - r1[:C]
            keep = row_half == c % 2
            v_ext = jnp.where(keep, jnp.concatenate([v_new, v_new], 0), 0.0).astype(BF16)
            r2 = jnp.dot(l2_s[d, c // 2], v_ext, preferred_element_type=F32)
            rows = pl.ds(pl.multiple_of(c * C, C), C)
            o_s[rows, :] = o_s[rows, :] + r1[C:] + r2[:C]
            new.append(s * gl_s[d, c] + r2[C:])
        return tuple(new)

    zero = jnp.zeros((DK_A, DK_A), F32)
    lax.fori_loop(0, n_tot, scan_step, (zero, zero))

    def finish(z_ref, o_ref, off):
        n = z_ref.shape[1]
        o = o_s[off:off + n, :]
        o = o * lax.rsqrt(jnp.mean(o * o, -1, keepdims=True) + EPS) * nw_ref[...]
        z = z_ref[0].astype(F32)
        o_ref[0] = (o * (z * jax.nn.sigmoid(z))).astype(o_ref.dtype)

    finish(zc_ref, oc_ref, 0)
    finish(zl_ref, ol_ref, lc)


def _gdn_call(p_c, p_l, ab_c, ab_l, conv_w, a_log, dt_bias, norm_w, h_a, cast_src=None):
    B, L, _ = p_l.shape
    lc = p_c.shape[1]
    t = lc + L
    n_tot, n_ctx = t // CHUNK, lc // CHUNK
    d_a = h_a * DK_A
    ab = jnp.concatenate([ab_c, ab_l], 1)[..., :4 * h_a].reshape(B, t, 2, 2, h_a)
    assert n_tot % 2 == 0
    g_row = ab[:, :, :, 0].transpose(0, 3, 2, 1).reshape(B, h_a, 2, n_tot // 2, 2 * CHUNK)
    b_col = ab[:, :, :, 1].transpose(0, 3, 1, 2)

    def col(k, n):
        return pl.BlockSpec((1, n, DK_A), lambda b, h, *_: (b, 0, k * h_a + h))

    def tap(k):
        return pl.BlockSpec((CONV_W, DK_A), lambda b, h, *_: (0, k * h_a + h))

    in_specs = [col(0, lc), col(1, lc), col(2, lc), col(3, lc),
                col(0, L), col(1, L), col(2, L), col(3, L),
                tap(0), tap(1), tap(2),
                pl.BlockSpec((1, 1, 2, n_tot // 2, 2 * CHUNK), lambda b, h, *_: (b, h, 0, 0, 0)),
                pl.BlockSpec((1, 1, t, 2), lambda b, h, *_: (b, h, 0, 0)),
                pl.BlockSpec((1, DK_A), lambda b, h, *_: (0, 0))]
    out_specs = [pl.BlockSpec((1, lc, DK_A), lambda b, h, *_: (b, 0, h)),
                 pl.BlockSpec((1, L, DK_A), lambda b, h, *_: (b, 0, h))]
    out_shape = [jax.ShapeDtypeStruct((B, lc, d_a), BF16), jax.ShapeDtypeStruct((B, L, d_a), BF16)]
    extra = ()
    if cast_src is not None:
        src, dst, shape = _cast_specs(cast_src, h_a)
        in_specs.append(src)
        out_specs.append(dst)
        out_shape.append(shape)
        extra = (cast_src[0],)
    grid_spec = pltpu.PrefetchScalarGridSpec(
        num_scalar_prefetch=2,
        grid=(B, h_a),
        in_specs=in_specs,
        out_specs=out_specs,
        scratch_shapes=[pltpu.VMEM((t, DK_A), F32), pltpu.VMEM((t, DK_A), F32), pltpu.VMEM((t, DK_A), F32),
                        pltpu.VMEM((t, DK_A), F32),
                        pltpu.VMEM((2, n_tot, CHUNK, DK_A), F32),
                        pltpu.VMEM((2, n_tot, 2 * CHUNK, DK_A), BF16),
                        pltpu.VMEM((2, n_tot // 2, CHUNK + DK_A, 2 * CHUNK), BF16),
                        pltpu.VMEM((2, n_tot, 1, DK_A), F32),
                        pltpu.VMEM((2, n_tot // 2, 2 * CHUNK), F32)],
    )
    return pl.pallas_call(
        functools.partial(_gdn_kernel, h_a=h_a, n_ctx=n_ctx, cast=cast_src is not None),
        grid_spec=grid_spec,
        out_shape=out_shape,
        compiler_params=_params(("parallel", "parallel"), 56),
        name="gdn",
    )(a_log.reshape(-1), dt_bias.reshape(-1), p_c, p_c, p_c, p_c, p_l, p_l, p_l, p_l,
      conv_w, conv_w, conv_w, g_row, b_col, norm_w.reshape(1, DK_A), *extra)


GDN_HEADS = 2


def _gdn_prep_h(x_ref, w_ref, lanes, dst_ref, off, mode, xp_ref):
    n = x_ref.shape[1]
    pad = CONV_W // 2
    edge = jnp.zeros((SUBLANE, DK_A), F32)
    xp_ref[0:SUBLANE, :] = edge
    xp_ref[n + SUBLANE:n + 2 * SUBLANE, :] = edge
    xp_ref[SUBLANE:n + SUBLANE, :] = x_ref[0, :, lanes].astype(F32)
    acc = None
    for o in range(-pad, pad + 1):
        term = xp_ref[SUBLANE + o:SUBLANE + o + n, :] * w_ref[o + pad:o + pad + 1, lanes]
        acc = term if acc is None else acc + term
    y = acc * jax.nn.sigmoid(acc)
    if mode != "v":
        y = y * lax.rsqrt(jnp.sum(y * y, -1, keepdims=True) + EPS)
    if mode == "q":
        y = y * DK_A ** -0.5
    dst_ref[off:off + n, :] = y


def _gdn_kernel_h(alog_ref, dtb_ref,
                  qc_ref, kc_ref, vc_ref, zc_ref, ql_ref, kl_ref, vl_ref, zl_ref,
                  wq_ref, wk_ref, wv_ref, grow_ref, bcol_ref, nw_ref,
                  oc_ref, ol_ref,
                  q_s, k_s, v_s, o_s, u_s, l1_s, l2_s, gl_s, gc_s, s_s, xp_s, *, h_a, n_ctx):
    C = CHUNK
    hp = o_s.shape[0]
    lc = qc_ref.shape[1]
    n_tot = u_s.shape[2]
    n_pairs = n_tot // 2
    group = max(g for g in (1, 2, 3, 6, 9) if n_pairs % g == 0)

    NB = 4
    W = NB * C
    ii = lax.broadcasted_iota(jnp.int32, (C, W), 0)
    jl = lax.broadcasted_iota(jnp.int32, (C, W), 1)
    blk = jl // C
    jm = jl - blk * C
    eye4 = ii == jm
    tri4 = ((blk < 2) & (ii >= jm)) | ((blk >= 2) & (ii <= jm))
    eye4f = eye4.astype(F32)
    lane2 = lax.broadcasted_iota(jnp.int32, (C, 2 * C), 1)
    lo2 = lane2 < C
    i2 = lax.broadcasted_iota(jnp.int32, (2 * C, 2 * C), 0)
    j2 = lax.broadcasted_iota(jnp.int32, (2 * C, 2 * C), 1)
    same = (i2 // C) == (j2 // C)

    def block_diag(x):
        return jnp.concatenate([jnp.where(blk == b, x, 0.0) for b in range(NB)], 0).astype(BF16)

    def pack(cols):
        return jnp.where(blk == 0, cols[0], jnp.where(blk == 1, cols[1], jnp.where(blk == 2, cols[2], cols[3])))

    def head_terms(hh):
        head = pl.program_id(1) * hp + hh
        lanes = slice(hh * DK_A, (hh + 1) * DK_A)
        _gdn_prep_h(qc_ref, wq_ref, lanes, q_s, 0, "q", xp_s)
        _gdn_prep_h(kc_ref, wk_ref, lanes, k_s, 0, "k", xp_s)
        _gdn_prep_h(vc_ref, wv_ref, lanes, v_s, 0, "v", xp_s)
        _gdn_prep_h(ql_ref, wq_ref, lanes, q_s, lc, "q", xp_s)
        _gdn_prep_h(kl_ref, wk_ref, lanes, k_s, lc, "k", xp_s)
        _gdn_prep_h(vl_ref, wv_ref, lanes, v_s, lc, "v", xp_s)

        for d in range(2):
            a = jnp.exp(jnp.full((1, 2 * C), alog_ref[d * h_a + head], F32))
            g = -a * _softplus(grow_ref[0, hh, d] + dtb_ref[d * h_a + head])
            tri = same & ((i2 <= j2) if d == 0 else (i2 >= j2))
            gc_s[d] = jnp.dot(g, tri.astype(F32), preferred_element_type=F32, precision=HIGHEST)

        def pair_setup(p):
            r0 = pl.multiple_of(p * 2 * C, 2 * C)
            q2 = q_s[pl.ds(r0, 2 * C), :]
            k2 = k_s[pl.ds(r0, 2 * C), :]
            v2 = v_s[pl.ds(r0, 2 * C), :]
            k16 = k2.astype(BF16)
            gk = lax.dot_general(k16, k16, _NT, preferred_element_type=F32)
            gq = lax.dot_general(q2.astype(BF16), k16, _NT, preferred_element_type=F32)
            kk2 = jnp.where(lo2, gk[:C], gk[C:])
            qk2 = jnp.where(lo2, gq[:C], gq[C:])
            kk4 = jnp.concatenate([kk2, kk2], 1)
            qk4 = jnp.concatenate([qk2, qk2], 1)
            grow = [gc_s[d, pl.ds(p, 1), :] for d in range(2)]
            rowb = jnp.concatenate([jnp.broadcast_to(g, (C, 2 * C)) for g in grow], 1)
            diag = jnp.where(eye4, rowb, 0.0)
            gcol, bcol, tot = [], [], []
            for b in range(NB):
                d, par = b // 2, b % 2
                half = diag[:, d * 2 * C:(d + 1) * 2 * C]
                gcol.append(jnp.sum(jnp.where(lo2 if par == 0 else ~lo2, half, 0.0), axis=1, keepdims=True))
                bcol.append(jax.nn.sigmoid(bcol_ref[0, hh, pl.ds(r0 + par * C, C), d:d + 1]))
                e = par * C + (C - 1 if d == 0 else 0)
                tot.append(grow[d][:, e:e + 1])
            gcc4 = pack(gcol)
            dec4 = jnp.where(tri4, jnp.exp(jnp.where(tri4, gcc4 - rowb, 0.0)), 0.0)
            m0 = jnp.where(eye4, 0.0, -(kk4 * pack(bcol) * dec4))
            return m0, (p, q2, k2, v2, qk4 * dec4, gcol, bcol, tot)

        def pair_finish(tinv4, aux):
            p, q2, k2, v2, attn4, gcol, bcol, tot = aux
            eg = [jnp.exp(g) for g in gcol]
            ks = (k2[:C], k2[C:])
            vs = (v2[:C], v2[C:])
            qs = (q2[:C], q2[C:])
            rhs = jnp.concatenate([jnp.concatenate([vs[b % 2] * bcol[b], ks[b % 2] * (bcol[b] * eg[b])], 1)
                                   for b in range(NB)], 0).astype(BF16)
            uw = jnp.dot(block_diag(tinv4), rhs, preferred_element_type=F32)
            for b in range(NB):
                d, par = b // 2, b % 2
                c = 2 * p + par
                u_s[hh, d, c] = uw[b * C:(b + 1) * C, :DK_A]
                l1_s[hh, d, c] = jnp.concatenate([uw[b * C:(b + 1) * C, DK_A:], qs[par] * eg[b]], 0).astype(BF16)
                gl_s[hh, d, c] = jnp.broadcast_to(jnp.exp(tot[b]), (1, DK_A))
            for d in range(2):
                f = jnp.concatenate([jnp.exp(tot[2 * d + par] - gcol[2 * d + par]) for par in range(2)], 0)
                kdec_t = (k2 * f).T
                l2_s[hh, d, p] = jnp.concatenate([attn4[:, d * 2 * C:(d + 1) * 2 * C], kdec_t], 0).astype(BF16)

        def group_terms(i, _):
            st = [pair_setup(i * group + g) for g in range(group)]
            m0s = [s[0] for s in st]
            rs = [jnp.dot(m0.astype(BF16), block_diag(m0), preferred_element_type=F32) for m0 in m0s]
            qms = [eye4f + m0 for m0 in m0s]
            for _ in range(4):
                qrs = [jnp.dot(jnp.concatenate([qm, r], 0).astype(BF16), block_diag(r), preferred_element_type=F32)
                       for qm, r in zip(qms, rs)]
                qms = [qm + qr[:C] for qm, qr in zip(qms, qrs)]
                rs = [qr[C:] for qr in qrs]
            tinvs = [qm + jnp.dot(qm.astype(BF16), block_diag(r), preferred_element_type=F32)
                     for qm, r in zip(qms, rs)]
            for tinv4, s in zip(tinvs, st):
                pair_finish(tinv4, s[1])
            return 0

        lax.fori_loop(0, n_pairs // group, group_terms, 0)

    for hh in range(hp):
        head_terms(hh)

    o_s[...] = jnp.zeros_like(o_s)
    s_s[...] = jnp.zeros_like(s_s)
    row_half = lax.broadcasted_iota(jnp.int32, (2 * C, DK_A), 0) // C

    def scan_step(i, _):
        c_b = jnp.where(i < n_ctx, n_ctx - 1 - i, n_ctx + n_tot - 1 - i)
        chains = [(hh, d, i if d == 0 else c_b) for hh in range(hp) for d in range(2)]
        r1s = [jnp.dot(l1_s[hh, d, c], s_s[hh, d].astype(BF16), preferred_element_type=F32)
               for hh, d, c in chains]
        v_exts = []
        for (hh, d, c), r1 in zip(chains, r1s):
            v_new = u_s[hh, d, c] - r1[:C]
            v_exts.append(jnp.where(row_half == c % 2, jnp.concatenate([v_new, v_new], 0), 0.0).astype(BF16))
        r2s = [jnp.dot(l2_s[hh, d, c // 2], v_ext, preferred_element_type=F32)
               for (hh, d, c), v_ext in zip(chains, v_exts)]
        for (hh, d, c), r1, r2 in zip(chains, r1s, r2s):
            rows = pl.ds(pl.multiple_of(c * C, C), C)
            o_s[hh, rows, :] = o_s[hh, rows, :] + r1[C:] + r2[:C]
            s_s[hh, d] = s_s[hh, d] * gl_s[hh, d, c] + r2[C:]
        return 0

    lax.fori_loop(0, n_tot, scan_step, 0)

    for hh in range(hp):
        lanes = slice(hh * DK_A, (hh + 1) * DK_A)
        for z_ref, o_ref, off in ((zc_ref, oc_ref, 0), (zl_ref, ol_ref, lc)):
            n = z_ref.shape[1]
            o = o_s[hh, off:off + n, :]
            o = o * lax.rsqrt(jnp.mean(o * o, -1, keepdims=True) + EPS) * nw_ref[...]
            z = z_ref[0, :, lanes].astype(F32)
            o_ref[0, :, lanes] = (o * (z * jax.nn.sigmoid(z))).astype(o_ref.dtype)


def _gdn_call_h(p_c, p_l, ab_c, ab_l, conv_w, a_log, dt_bias, norm_w, h_a):
    B, L, _ = p_l.shape
    lc = p_c.shape[1]
    t = lc + L
    n_tot, n_ctx = t // CHUNK, lc // CHUNK
    hp = GDN_HEADS if h_a % GDN_HEADS == 0 else 1
    n_hg = h_a // hp
    wd = hp * DK_A
    ab = jnp.concatenate([ab_c, ab_l], 1)[..., :4 * h_a].reshape(B, t, 2, 2, h_a)
    assert n_tot % 2 == 0
    g_row = ab[:, :, :, 0].transpose(0, 3, 2, 1).reshape(B, h_a, 2, n_tot // 2, 2 * CHUNK)
    b_col = ab[:, :, :, 1].transpose(0, 3, 1, 2)

    def col(k, n):
        return pl.BlockSpec((1, n, wd), lambda b, j, *_: (b, 0, k * n_hg + j))

    def tap(k):
        return pl.BlockSpec((CONV_W, wd), lambda b, j, *_: (0, k * n_hg + j))

    grid_spec = pltpu.PrefetchScalarGridSpec(
        num_scalar_prefetch=2,
        grid=(B, n_hg),
        in_specs=[col(0, lc), col(1, lc), col(2, lc), col(3, lc),
                  col(0, L), col(1, L), col(2, L), col(3, L),
                  tap(0), tap(1), tap(2),
                  pl.BlockSpec((1, hp, 2, n_tot // 2, 2 * CHUNK), lambda b, j, *_: (b, j, 0, 0, 0)),
                  pl.BlockSpec((1, hp, t, 2), lambda b, j, *_: (b, j, 0, 0)),
                  pl.BlockSpec((1, DK_A), lambda b, j, *_: (0, 0))],
        out_specs=[pl.BlockSpec((1, lc, wd), lambda b, j, *_: (b, 0, j)),
                   pl.BlockSpec((1, L, wd), lambda b, j, *_: (b, 0, j))],
        scratch_shapes=[pltpu.VMEM((t, DK_A), F32), pltpu.VMEM((t, DK_A), F32), pltpu.VMEM((t, DK_A), F32),
                        pltpu.VMEM((hp, t, DK_A), F32),
                        pltpu.VMEM((hp, 2, n_tot, CHUNK, DK_A), F32),
                        pltpu.VMEM((hp, 2, n_tot, 2 * CHUNK, DK_A), BF16),
                        pltpu.VMEM((hp, 2, n_tot // 2, CHUNK + DK_A, 2 * CHUNK), BF16),
                        pltpu.VMEM((hp, 2, n_tot, 1, DK_A), F32),
                        pltpu.VMEM((2, n_tot // 2, 2 * CHUNK), F32),
                        pltpu.VMEM((hp, 2, DK_A, DK_A), F32),
                        pltpu.VMEM((max(L, lc) + 2 * SUBLANE, DK_A), F32)],
    )
    return pl.pallas_call(
        functools.partial(_gdn_kernel_h, h_a=h_a, n_ctx=n_ctx),
        grid_spec=grid_spec,
        out_shape=[jax.ShapeDtypeStruct((B, lc, h_a * DK_A), BF16), jax.ShapeDtypeStruct((B, L, h_a * DK_A), BF16)],
        compiler_params=_params(("parallel", "parallel"), 56),
        name="gdn",
    )(a_log.reshape(-1), dt_bias.reshape(-1), p_c, p_c, p_c, p_c, p_l, p_l, p_l, p_l,
      conv_w, conv_w, conv_w, g_row, b_col, norm_w.reshape(1, DK_A))


_NT = (((1,), (1,)), ((), ()))


def _softmax_rows(s, sink=None):
    m = jnp.max(s, -1, keepdims=True)
    if sink is not None:
        m = jnp.maximum(m, sink)
    e = jnp.exp(s - m)
    den = jnp.sum(e, -1, keepdims=True)
    if sink is not None:
        den = den + jnp.exp(sink - m)
    return (e * (1.0 / den)).astype(BF16)


def _gqa_kernel(sink_ref, q_ref, kvl_ref, kvc_ref, cos_ref, sa_ref, sb_ref, *rest, n_heads, latent):
    if len(rest) == 3:
        cast_src, o_ref, cast_dst = rest
        cast_dst[...] = cast_src[...].astype(BF16)
    else:
        o_ref, = rest
    G = n_heads // KV_B
    P = 2 * DH
    n = pl.program_id(1)
    seq = kvl_ref.shape[1]
    lc = kvc_ref.shape[1]

    def rope(x, r0):
        reps = x.shape[1] // P
        rows = pl.ds(pl.multiple_of(r0, QBLK), QBLK)
        c, a, b = [jnp.concatenate([t[rows, :]] * reps, 1) if reps > 1 else t[rows, :]
                   for t in (cos_ref, sa_ref, sb_ref)]
        w = x.shape[1]
        return x * c + pltpu.roll(x, w - DH // 4, 1) * a + pltpu.roll(x, DH // 4, 1) * b

    q = q_ref[0].astype(F32) * DH ** -0.5
    kc = kvc_ref[0][:, :P].astype(F32)
    vc = kvc_ref[0][:, P:].astype(F32)
    if latent:
        q = rope(q, n * QBLK)
        ks, vs = [], []
        for o in (-1, 0, 1):
            s0 = jnp.clip((n + o) * QBLK, 0, seq - QBLK)
            kv = kvl_ref[0, pl.ds(pl.multiple_of(s0, QBLK), QBLK), :].astype(F32)
            ks.append(rope(kv[:, :P], s0))
            vs.append(kv[:, P:])
        k = jnp.concatenate(ks + [kc], 0)
        v = jnp.concatenate(vs + [vc], 0)
        nk = 3 * QBLK + lc
        qi = lax.broadcasted_iota(jnp.int32, (QBLK, nk), 0)
        kj = lax.broadcasted_iota(jnp.int32, (QBLK, nk), 1)
        kpos = n * QBLK - QBLK + kj
        rel = kj - QBLK - qi
        valid = ((rel >= -WIN) & (rel <= WIN) & (kpos >= 0) & (kpos < seq)) | (kj >= 3 * QBLK)
    else:
        k, v = kc, vc
        valid = None
    q = q.astype(BF16)
    k_sw = pltpu.roll(k, DH, 1).astype(BF16)
    v_sw = pltpu.roll(v, DH, 1).astype(BF16)
    k = k.astype(BF16)
    v = v.astype(BF16)
    lane = lax.broadcasted_iota(jnp.int32, (QBLK, P), 1)
    zero = jnp.zeros((QBLK, P), BF16)
    scores = []
    for h in range(n_heads):
        qp = q[:, (h // 2) * P:(h // 2 + 1) * P]
        qh = jnp.where(lane < DH, qp, zero) if h % 2 == 0 else jnp.where(lane >= DH, qp, zero)
        straight = (h // G == h % 2)
        scores.append(lax.dot_general(qh, k if straight else k_sw, _NT, preferred_element_type=F32))
    probs = []
    for h, s in enumerate(scores):
        if valid is not None:
            s = jnp.where(valid, s, NEG_INF)
        probs.append(_softmax_rows(s, sink_ref[h]))
    outs = [jnp.dot(pr, v if (h // G == h % 2) else v_sw, preferred_element_type=F32) for h, pr in enumerate(probs)]
    for p in range(n_heads // 2):
        o_ref[0, :, p * P:(p + 1) * P] = jnp.where(lane < DH, outs[2 * p], outs[2 * p + 1]).astype(o_ref.dtype)


def _gqa_call(p_q, p_l, p_c, sink, tables, d_model, d_b, latent, cast_src=None):
    B, nq, _ = p_q.shape
    L, lc = p_l.shape[1], p_c.shape[1]
    kvw = 2 * KV_B * DH
    n_heads = d_b // DH
    assert (n_heads // KV_B) % 2 == 0
    in_specs = [pl.BlockSpec((1, QBLK, d_b), lambda b, n, *_: (b, n, 2 * d_model // d_b)),
                pl.BlockSpec((1, L, kvw), lambda b, n, *_: (b, 0, 3 * d_model // kvw)),
                pl.BlockSpec((1, lc, kvw), lambda b, n, *_: (b, 0, 3 * d_model // kvw)),
                pl.BlockSpec((L, 2 * DH), lambda b, n, *_: (0, 0)),
                pl.BlockSpec((L, 2 * DH), lambda b, n, *_: (0, 0)),
                pl.BlockSpec((L, 2 * DH), lambda b, n, *_: (0, 0))]
    out_specs = [pl.BlockSpec((1, QBLK, d_b), lambda b, n, *_: (b, n, 0))]
    out_shape = [jax.ShapeDtypeStruct((B, nq, d_b), BF16)]
    extra = ()
    if cast_src is not None:
        src, dst, shape = _cast_specs(cast_src, nq // QBLK)
        in_specs.append(src)
        out_specs.append(dst)
        out_shape.append(shape)
        extra = (cast_src[0],)
    grid_spec = pltpu.PrefetchScalarGridSpec(
        num_scalar_prefetch=1, grid=(B, nq // QBLK), in_specs=in_specs, out_specs=out_specs)
    return pl.pallas_call(
        functools.partial(_gqa_kernel, n_heads=n_heads, latent=latent),
        grid_spec=grid_spec,
        out_shape=out_shape,
        compiler_params=_params(("parallel", "parallel"), 48),
        name="window_gqa" if latent else "ctx_gqa",
    )(sink, p_q, p_l, p_c, *tables, *extra)


def _rope_tables(n_tokens):
    t = jnp.arange(n_tokens)
    row = (t // GRID_W).astype(F32)
    col = (t % GRID_W).astype(F32)
    half = DH // 2
    inv = ROPE_BASE ** (-jnp.arange(0, half, 2, dtype=F32) / half)
    ang_r = row[:, None] * inv[None]
    ang_c = col[:, None] * inv[None]
    ang = jnp.concatenate([ang_r, ang_r, ang_c, ang_c], -1)
    cos, sin = jnp.cos(ang), jnp.sin(ang)
    first = (jnp.arange(DH) % half) < half // 2
    sa = jnp.where(first, -sin, 0.0)
    sb = jnp.where(first, 0.0, sin)
    return tuple(jnp.concatenate([u, u], -1) for u in (cos, sa, sb))


def _na_kernel(q_ref, k_ref, v_ref, kc_ref, vc_ref, tbl_ref, *rest, n_heads, rows, latent):
    n_cast = len(rest) // 2
    o_ref = rest[n_cast]
    for cast_src, cast_dst in zip(rest[:n_cast], rest[n_cast + 1:]):
        cast_dst[...] = cast_src[...].astype(BF16)
    P = 2 * DH
    r = pl.program_id(1)
    nq = q_ref.shape[1]
    nb = KH_MAX * GRID_W
    q = (q_ref[0].astype(F32) * DH ** -0.5).astype(BF16)
    lane = lax.broadcasted_iota(jnp.int32, (nq, P), 1)
    zero = jnp.zeros((nq, P), BF16)
    if latent:
        r0 = jnp.clip(r - KH_MAX // 2, 0, rows - KH_MAX)
        band = pl.ds(pl.multiple_of(r0 * GRID_W, GRID_W), nb)
        d0 = r0 - r + (KH_MAX - 1)
    scores, vals = [], []
    for p in range(n_heads // 2):
        cols = slice(p * P, (p + 1) * P)
        qp = q[:, cols]
        kc = kc_ref[0, :, cols]
        vc = vc_ref[0, :, cols]
        if latent:
            kk = jnp.concatenate([k_ref[0, band, cols], kc], 0)
            vals.append(jnp.concatenate([v_ref[0, band, cols], vc], 0))
        else:
            kk = kc
            vals.append(vc)
        for half in range(2):
            qh = jnp.where(lane < DH, qp, zero) if half == 0 else jnp.where(lane >= DH, qp, zero)
            scores.append(lax.dot_general(qh, kk, _NT, preferred_element_type=F32))
    probs = []
    for h, s in enumerate(scores):
        if latent:
            bias = jnp.concatenate([tbl_ref[h, d0 + 2 * t] for t in range(KH_MAX // 2)], 1)
            s = jnp.concatenate([s[:, :nb] + bias, s[:, nb:]], 1)
        probs.append(_softmax_rows(s))
    outs = [jnp.dot(pr, vals[h // 2], preferred_element_type=F32) for h, pr in enumerate(probs)]
    for p in range(n_heads // 2):
        o_ref[0, :, p * P:(p + 1) * P] = jnp.where(lane < DH, outs[2 * p], outs[2 * p + 1]).astype(o_ref.dtype)


def _na_bias_table(rpb):
    qc = jnp.arange(GRID_W)[:, None]
    kc = jnp.arange(GRID_W)[None, :]
    dc = jnp.clip(kc - qc + (KW - 1), 0, 2 * KW - 2)
    q_cs = jnp.clip(qc - KW // 2, 0, GRID_W - KW)
    in_win = (kc >= q_cs) & (kc < q_cs + KW)
    t = jnp.where(in_win[None, None], rpb[:, :, dc], NEG_INF)
    return jnp.concatenate([t[:, :-1], t[:, 1:]], -1)


def _na_call(p_q, p_l, p_c, tbl, d_model, d_c, latent, cast_src=None):
    B, nq, _ = p_q.shape
    L, lc = p_l.shape[1], p_c.shape[1]
    n_heads = d_c // DH
    rows = L // GRID_W
    assert rows >= KH_MAX and n_heads % 2 == 0
    cb = 2 * d_model // d_c
    qb = GRID_W
    in_specs = [pl.BlockSpec((1, qb, d_c), lambda b, r: (b, r, cb + 1)),
                pl.BlockSpec((1, L, d_c), lambda b, r: (b, 0, cb + 2)),
                pl.BlockSpec((1, L, d_c), lambda b, r: (b, 0, cb + 3)),
                pl.BlockSpec((1, lc, d_c), lambda b, r: (b, 0, cb + 2)),
                pl.BlockSpec((1, lc, d_c), lambda b, r: (b, 0, cb + 3)),
                pl.BlockSpec(tbl.shape, lambda b, r: (0, 0, 0, 0))]
    out_specs = [pl.BlockSpec((1, qb, d_c), lambda b, r: (b, r, 0))]
    out_shape = [jax.ShapeDtypeStruct((B, nq, d_c), BF16)]
    extra = ()
    for cast in cast_src or ():
        src, dst, shape = _cast_specs(cast, nq // qb)
        in_specs.append(src)
        out_specs.append(dst)
        out_shape.append(shape)
        extra += (cast[0],)
    return pl.pallas_call(
        functools.partial(_na_kernel, n_heads=n_heads, rows=rows, latent=latent),
        grid=(B, nq // qb),
        in_specs=in_specs,
        out_specs=out_specs,
        out_shape=out_shape,
        compiler_params=_params(("parallel", "parallel"), 48),
        name="nbr_attn" if latent else "ctx_attn",
    )(p_q, p_l, p_l, p_c, p_c, tbl, *extra)


def _l2norm(u):
    return u * lax.rsqrt(jnp.sum(u * u, -1, keepdims=True) + EPS)


def _short_conv(u, w):
    pad = CONV_W // 2
    L = u.shape[1]
    up = jnp.pad(u, ((0, 0), (pad, pad), (0, 0)))
    return sum(up[:, i:i + L] * w[i] for i in range(CONV_W))


def _gdn_prepare(qkv, ab, conv_w, a_log, dt_bias, h_a):
    B, L, _ = qkv.shape
    qkv = jax.nn.silu(_short_conv(qkv, conv_w))
    q, k, v = jnp.split(qkv, 3, -1)
    q = _l2norm(q.reshape(B, L, h_a, DK_A))
    k = _l2norm(k.reshape(B, L, h_a, DK_A))
    v = v.reshape(B, L, h_a, DK_A)
    ab = ab.reshape(B, L, 2, 2, h_a)
    g = -jnp.exp(a_log) * jax.nn.softplus(ab[:, :, :, 0] + dt_bias)
    beta = jax.nn.sigmoid(ab[:, :, :, 1])
    return q, k, v, g, beta


def _gdn_chunked(q, k, v, g, beta, s0):
    B, L, H, dk = k.shape
    dv = v.shape[-1]
    n = L // CHUNK

    def chunks(u):
        return u.reshape(B, n, CHUNK, H, -1).transpose(0, 3, 1, 2, 4)

    qc, kc, vc = chunks(q * dk ** -0.5), chunks(k), chunks(v)
    gc = g.reshape(B, n, CHUNK, H).transpose(0, 3, 1, 2)
    bc = beta.reshape(B, n, CHUNK, H).transpose(0, 3, 1, 2)
    gcum = jnp.cumsum(gc, -1)
    tril = jnp.tril(jnp.ones((CHUNK, CHUNK), bool))
    strict = jnp.tril(jnp.ones((CHUNK, CHUNK), bool), -1)
    diff = gcum[..., :, None] - gcum[..., None, :]
    decay = jnp.where(tril, jnp.exp(jnp.where(tril, diff, 0.0)), 0.0)
    kb = kc * bc[..., None]
    vb = vc * bc[..., None]
    a = jnp.where(strict, jnp.einsum('bhnid,bhnjd->bhnij', kb, kc) * decay, 0.0)
    eye = jnp.eye(CHUNK, dtype=F32)
    tinv = lax.linalg.triangular_solve(eye + a, jnp.broadcast_to(eye, a.shape),
                                       left_side=True, lower=True, unit_diagonal=True)
    u = tinv @ vb
    w = tinv @ (kb * jnp.exp(gcum)[..., None])
    attn = jnp.einsum('bhnid,bhnjd->bhnij', qc, kc) * decay
    qg = qc * jnp.exp(gcum)[..., None]
    kdec = kc * jnp.exp(gcum[..., -1:] - gcum)[..., None]
    glast = jnp.exp(gcum[..., -1])
    xs = tuple(jnp.moveaxis(t, 2, 0) for t in (u, w, attn, qg, kdec, glast))

    def step(s, inp):
        u_i, w_i, attn_i, qg_i, kdec_i, gl_i = inp
        v_new = u_i - w_i @ s
        o_i = qg_i @ s + attn_i @ v_new
        s = s * gl_i[..., None, None] + jnp.einsum('bhcd,bhce->bhde', kdec_i, v_new)
        return s, o_i

    s_final, o = lax.scan(step, s0, xs)
    o = jnp.moveaxis(o, 0, 2).transpose(0, 2, 3, 1, 4).reshape(B, L, H, dv)
    return o, s_final


def _gdn_output(o, z, norm_w, h_a):
    B, L = z.shape[:2]
    o = o * lax.rsqrt(jnp.mean(o * o, -1, keepdims=True) + EPS) * norm_w
    zg = jax.nn.silu(z.reshape(B, L, h_a, DK_A))
    return (o * zg).reshape(B, L, h_a * DK_A)


def _gated_deltanet(qkv_c, ab_c, z_c, qkv_l, ab_l, z_l, conv_w, a_log, dt_bias, norm_w, ctx_out, h_a):
    qc, kc, vc, gc, bc = _gdn_prepare(qkv_c, ab_c, conv_w, a_log, dt_bias, h_a)
    ql, kl, vl, gl, bl = _gdn_prepare(qkv_l, ab_l, conv_w, a_log, dt_bias, h_a)
    B = ql.shape[0]
    s0 = jnp.zeros((B, h_a, DK_A, DK_A), F32)
    o_c_sum, o_l_sum = 0.0, 0.0
    for d in range(2):
        seq_c = (qc, kc, vc, gc[:, :, d], bc[:, :, d])
        seq_l = (ql, kl, vl, gl[:, :, d], bl[:, :, d])
        if d == 1:
            seq_c = tuple(jnp.flip(t, 1) for t in seq_c)
            seq_l = tuple(jnp.flip(t, 1) for t in seq_l)
        o_c, s_c = _gdn_chunked(*seq_c, s0)
        o_l, _ = _gdn_chunked(*seq_l, s_c)
        if d == 1:
            o_c, o_l = jnp.flip(o_c, 1), jnp.flip(o_l, 1)
        o_c_sum = o_c_sum + o_c
        o_l_sum = o_l_sum + o_l
    o_lat = _gdn_output(o_l_sum, z_l, norm_w, h_a)
    o_ctx = _gdn_output(o_c_sum, z_c, norm_w, h_a) if ctx_out else None
    return o_ctx, o_lat


def _rope_tables_ref(n_tokens, head_dim):
    t = jnp.arange(n_tokens)
    row = (t // GRID_W).astype(F32)
    col = (t % GRID_W).astype(F32)
    half = head_dim // 2
    inv = ROPE_BASE ** (-jnp.arange(0, half, 2, dtype=F32) / half)
    ang_r = row[:, None] * inv[None]
    ang_c = col[:, None] * inv[None]
    ang = jnp.concatenate([ang_r, ang_r, ang_c, ang_c], -1)
    return jnp.cos(ang), jnp.sin(ang)


def _apply_rope(x, cos, sin):
    half = x.shape[-1] // 2

    def rot(u):
        u1, u2 = jnp.split(u, 2, -1)
        return jnp.concatenate([-u2, u1], -1)

    xr = jnp.concatenate([rot(x[..., :half]), rot(x[..., half:])], -1)
    return x * cos[None, :, None, :] + xr * sin[None, :, None, :]


def _softmax_with_sink(s, sink):
    sk = jnp.broadcast_to(sink, s.shape[:-1] + (1,))
    p = jax.nn.softmax(jnp.concatenate([s, sk], -1), -1)
    return p[..., :-1]


def _window_gqa(q_c, kv_c, q_l, kv_l, sink, cos, sin, ctx_out):
    B, L = q_l.shape[:2]
    Lc = q_c.shape[1]
    h_b = q_l.shape[-1] // DH
    G = h_b // KV_B
    n = L // QBLK
    J = 3 * QBLK
    scale = DH ** -0.5
    k_l, v_l = [u.reshape(B, L, KV_B, DH) for u in jnp.split(kv_l, 2, -1)]
    k_c, v_c = [u.reshape(B, Lc, KV_B, DH) for u in jnp.split(kv_c, 2, -1)]
    q_l = _apply_rope(q_l.reshape(B, L, h_b, DH), cos, sin)
    k_l = _apply_rope(k_l, cos, sin)
    sink = sink.reshape(KV_B, G, 1, 1)

    def band(u):
        up = jnp.pad(u, ((0, 0), (QBLK, QBLK), (0, 0), (0, 0))).reshape(B, n + 2, QBLK, KV_B, DH)
        return jnp.concatenate([up[:, :-2], up[:, 1:-1], up[:, 2:]], axis=2)

    k_w, v_w = band(k_l), band(v_l)
    qb = q_l.reshape(B, n, QBLK, KV_B, G, DH)
    qi = jnp.arange(QBLK)[:, None]
    kj = jnp.arange(J)[None]
    kpos = (jnp.arange(n) * QBLK)[:, None, None] - QBLK + kj[None]
    in_win = (jnp.abs(kj - QBLK - qi)[None] <= WIN) & (kpos >= 0) & (kpos < L)
    s_w = jnp.einsum('bnqkgd,bnjkd->bnkgqj', qb, k_w) * scale
    s_w = jnp.where(in_win[None, :, None, None], s_w, NEG_INF)
    s_x = jnp.einsum('bnqkgd,bckd->bnkgqc', qb, k_c) * scale
    p = _softmax_with_sink(jnp.concatenate([s_w, s_x], -1), sink)
    o = (jnp.einsum('bnkgqj,bnjkd->bnqkgd', p[..., :J], v_w)
         + jnp.einsum('bnkgqc,bckd->bnqkgd', p[..., J:], v_c))
    o_lat = o.reshape(B, L, h_b * DH)
    o_ctx = None
    if ctx_out:
        qc = q_c.reshape(B, Lc, KV_B, G, DH)
        s_c = jnp.einsum('bqkgd,bckd->bkgqc', qc, k_c) * scale
        p_c = _softmax_with_sink(s_c, sink)
        o_ctx = jnp.einsum('bkgqc,bckd->bqkgd', p_c, v_c).reshape(B, Lc, h_b * DH)
    return o_ctx, o_lat


def _neighbourhood_attn(q_c, kv_c, q_l, kv_l, rpb, ctx_out):
    B, L = q_l.shape[:2]
    Lc = q_c.shape[1]
    h_c = q_l.shape[-1] // DH
    rows = L // GRID_W
    kh = min(KH_MAX, rows)
    ncb = GRID_W // NA_QCOLS
    m = kh * NA_KCOLS
    scale = DH ** -0.5
    k_l, v_l = [u.reshape(B, L, h_c, DH) for u in jnp.split(kv_l, 2, -1)]
    k_c, v_c = [u.reshape(B, Lc, h_c, DH) for u in jnp.split(kv_c, 2, -1)]
    r = jnp.arange(rows)
    row_idx = jnp.clip(r - kh // 2, 0, rows - kh)[:, None] + jnp.arange(kh)[None]
    cb = jnp.arange(ncb)
    col_idx = (jnp.clip(cb * NA_QCOLS - KW // 2, 0, GRID_W - NA_KCOLS)[:, None]
               + jnp.arange(NA_KCOLS)[None])

    def gather_band(u):
        u = u.reshape(B, rows, GRID_W, h_c, DH)[:, row_idx]
        u = u[:, :, :, col_idx]
        return u.transpose(0, 1, 3, 2, 4, 5, 6).reshape(B, rows, ncb, m, h_c, DH)

    k_n, v_n = gather_band(k_l), gather_band(v_l)
    qn = q_l.reshape(B, rows, ncb, NA_QCOLS, h_c, DH)
    q_col = cb[:, None] * NA_QCOLS + jnp.arange(NA_QCOLS)[None]
    q_cs = jnp.clip(q_col - KW // 2, 0, GRID_W - KW)
    key_col = jnp.tile(col_idx, (1, kh))
    key_row = jnp.repeat(row_idx, NA_KCOLS, axis=1)
    in_win = ((key_col[:, None, :] >= q_cs[:, :, None])
              & (key_col[:, None, :] < q_cs[:, :, None] + KW))
    dr = key_row - r[:, None] + (KH_MAX - 1)
    dc = jnp.clip(key_col[:, None, :] - q_col[:, :, None] + (KW - 1), 0, 2 * KW - 2)
    bias = rpb[:, dr[:, None, None, :], dc[None]]
    s_nb = jnp.einsum('brcqhd,brcmhd->bhrcqm', qn, k_n) * scale + bias
    s_nb = jnp.where(in_win, s_nb, NEG_INF)
    s_x = jnp.einsum('brcqhd,bkhd->bhrcqk', qn, k_c) * scale
    p = jax.nn.softmax(jnp.concatenate([s_nb, s_x], -1), -1)
    o = (jnp.einsum('bhrcqm,brcmhd->brcqhd', p[..., :m], v_n)
         + jnp.einsum('bhrcqk,bkhd->brcqhd', p[..., m:], v_c))
    o_lat = o.reshape(B, L, h_c * DH)
    o_ctx = None
    if ctx_out:
        qc = q_c.reshape(B, Lc, h_c, DH)
        s_c = jnp.einsum('bqhd,bkhd->bhqk', qc, k_c) * scale
        p_c = jax.nn.softmax(s_c, -1)
        o_ctx = jnp.einsum('bhqk,bkhd->bqhd', p_c, v_c).reshape(B, Lc, h_c * DH)
    return o_ctx, o_lat


def _route(logits):
    T = logits.shape[0]
    g_logits = logits[:, :N_GROUPS]
    g_prob = jax.nn.softmax(g_logits, -1)
    g_sel = jnp.argmax(g_logits, -1)
    e_logits = logits[:, N_GROUPS:N_GROUPS + N_EXPERTS].reshape(T, N_GROUPS, E_PER_GROUP)
    e_in = jnp.take_along_axis(e_logits, g_sel[:, None, None], 1)[:, 0]
    top_v, top_i = lax.top_k(e_in, TOP_K)
    wts = jax.nn.softmax(top_v, -1) * jnp.take_along_axis(g_prob, g_sel[:, None], 1)
    expert = g_sel[:, None] * E_PER_GROUP + top_i
    return expert.astype(jnp.int32), wts


def _dispatch(expert, tm):
    T = expert.shape[0]
    S = T * TOP_K
    e_flat = expert.reshape(S)
    order = jnp.argsort(e_flat)
    e_sorted = e_flat[order]
    tok_sorted = order // TOP_K
    counts = jnp.bincount(e_flat, length=N_EXPERTS)
    starts = jnp.cumsum(counts) - counts
    padded = (counts + tm - 1) // tm * tm
    pends = jnp.cumsum(padded)
    pstarts = pends - padded
    dest = pstarts[e_sorted] + jnp.arange(S) - starts[e_sorted]
    n_tiles = -(-S // tm) + N_EXPERTS
    tile_expert = jnp.minimum(jnp.searchsorted(pends, jnp.arange(n_tiles) * tm, side='right'),
                              N_EXPERTS - 1).astype(jnp.int32)
    n_used = (pends[-1] // tm).astype(jnp.int32).reshape(1)
    pos = jnp.zeros((S,), jnp.int32).at[order].set(dest.astype(jnp.int32)).reshape(T, TOP_K)
    src_tok = jnp.zeros((n_tiles * tm,), jnp.int32).at[dest].set(tok_sorted.astype(jnp.int32))
    return tile_expert, n_used, pos, src_tok


def _moe(f, logits, w_gate, w_up, w_down, split=0):
    T, d = f.shape
    tm = MOE_TM
    ids, wts, cnt = _route_call(logits)
    expert, rank = ids[:, :TOP_K], ids[:, TOP_K:2 * TOP_K]
    counts = cnt[0, :N_EXPERTS].astype(jnp.int32)
    padded = (counts + tm - 1) // tm * tm
    pends = jnp.cumsum(padded)
    pstarts = pends - padded
    start = jnp.sum(jnp.where(expert[..., None] == jnp.arange(N_EXPERTS), pstarts, 0), -1)
    dest = start + rank
    n_tiles = -(-T * TOP_K // tm) + N_EXPERTS
    tile_expert = jnp.minimum(jnp.sum(pends[None, :] <= (jnp.arange(n_tiles) * tm)[:, None], -1),
                              N_EXPERTS - 1).astype(jnp.int32)
    n_used = (pends[-1] // tm).astype(jnp.int32).reshape(1)
    tok = jnp.broadcast_to(jnp.arange(T, dtype=jnp.int32)[:, None], (T, TOP_K))
    src_tok = (jnp.arange(n_tiles * tm, dtype=jnp.int32) % T).at[dest.reshape(-1)].set(tok.reshape(-1))
    out = _moe_call(tile_expert, n_used, f[src_tok], w_gate, w_up, w_down, tm)
    parts = [(dest[:split], wts[:split]), (dest[split:], wts[split:])]
    return [(out[d_[:, 0]], out[d_[:, 1]], w_) for d_, w_ in parts]


def kernel(x, c, ctx, c_ctx, w_mod, b_mod, norm_mix, norm_ffn, w_in, conv_a, a_log, dt_bias, gdn_norm, sink_b, rpb_c, w_out, w_router_group, b_router_group, w_router_expert, b_router_expert, w_gate, w_up, w_down, norm_final):
    B, L, D = x.shape
    Lc = ctx.shape[1]
    depth = w_mod.shape[0]
    d_a = D // 2
    d_b = D // 4
    d_c = D - d_a - d_b
    h_a = d_a // DK_A
    n_ab = 4 * h_a
    kvb = 2 * KV_B * DH
    o_ab = 4 * d_a
    n_main = 4 * d_a + d_b + kvb + 3 * d_c

    cc = jnp.zeros((MOD_ROWS, D), F32).at[:B].set(c).at[B].set(c_ctx)
    mod = _mod_call(cc, w_mod, b_mod).reshape(depth, MOD_ROWS, N_MOD, D)
    rope = _rope_tables(L)

    o_qb = o_ab + n_ab
    o_kvb = o_qb + d_b
    w_main, w_ab = _winprep_call(w_in, (o_ab, o_qb, o_kvb, o_kvb + kvb))

    tn_in = n_main
    for cand in (1280, 1024, 768, 640, 512, 384, 256, 128):
        if n_main % cand == 0:
            tn_in = cand
            break

    x_l, x_c = x, ctx
    for l in range(depth):
        last = l == depth - 1
        ctx_out = not last
        mod_l = mod[l, :B]
        mod_c = mod[l, B:B + 1]
        ml = [mod_l[:, i:i + 1] for i in range(N_MOD)]
        mc = [mod_c[:, i:i + 1] for i in range(N_MOD)]
        gain_mix = norm_mix[l].reshape(1, D)
        p_l, ab_l = _inproj_call(x_l, ml[0], ml[1], gain_mix, w_main, w_ab, l, min(L, 1024), tn_in)
        p_c, ab_c = _inproj_call(x_c, mc[0], mc[1], gain_mix, w_main, w_ab, l, min(Lc, 1024), tn_in)

        n_e, _, d_f = w_gate.shape[1:]
        src_u = _cast_chunks(w_up, l, B * (L // QBLK))
        src_gd = [_cast_chunks(w, l, B * (L // GRID_W)) for w in (w_gate, w_down)]
        oA_c, oA_l = _gdn_call_h(p_c, p_l, ab_c, ab_l, conv_a[l], a_log[l], dt_bias[l], gdn_norm[l], h_a)
        oB_l, *wu = _gqa_call(p_l, p_l, p_c, sink_b[l], rope, D, d_b, True, src_u)
        tbl = _na_bias_table(rpb_c[l])
        oC_l, *wgd = _na_call(p_l, p_l, p_c, tbl, D, d_c, True, src_gd if all(src_gd) else None)
        wu = wu[0].reshape(n_e, D, d_f) if wu else w_up[l].astype(BF16)
        wg = wgd[0].reshape(n_e, D, d_f) if wgd else w_gate[l].astype(BF16)
        wd = wgd[1].reshape(n_e, d_f, D) if wgd else w_down[l].astype(BF16)
        if ctx_out:
            oB_c, = _gqa_call(p_c, p_l, p_c, sink_b[l], rope, D, d_b, False)
            oC_c, = _na_call(p_c, p_l, p_c, tbl, D, d_c, False)

        w_o = w_out[l].astype(BF16)
        w_r = jnp.pad(jnp.concatenate([w_router_group[l], w_router_expert[l]], -1),
                      ((0, 0), (0, ROUTER_COLS - N_GROUPS - N_EXPERTS)))
        w_r_hi = w_r.astype(BF16)
        w_r = jnp.concatenate([w_r_hi, (w_r - w_r_hi.astype(F32)).astype(BF16)], -1)
        b_r = jnp.pad(jnp.concatenate([b_router_group[l], b_router_expert[l]], -1),
                      (0, ROUTER_COLS - N_GROUPS - N_EXPERTS)).reshape(1, ROUTER_COLS)
        gain_ffn = norm_ffn[l].reshape(1, D)
        x_l, f_l, lg_l = _outproj_call((oA_l, oB_l, oC_l), x_l, ml[2], ml[3], ml[4], gain_ffn, w_o, w_r, b_r,
                                       min(L, 256))
        if ctx_out:
            x_c, f_c, lg_c = _outproj_call((oA_c, oB_c, oC_c), x_c, mc[2], mc[3], mc[4], gain_ffn, w_o, w_r, b_r,
                                           min(Lc, 256))
            f_all = jnp.concatenate([f_c.reshape(B * Lc, D), f_l.reshape(B * L, D)], 0)
            lg_all = jnp.concatenate([lg_c.reshape(B * Lc, ROUTER_COLS), lg_l.reshape(B * L, ROUTER_COLS)], 0)
            y_c, y_l = _moe(f_all, lg_all, wg, wu, wd, B * Lc)
            y_c = [u.reshape(B, Lc, -1) for u in y_c]
            x_c = _resid_call(x_c, *y_c, mc[5], gain_ffn, min(Lc, 256), False)
        else:
            _, y_l = _moe(f_l.reshape(B * L, D), lg_l.reshape(B * L, ROUTER_COLS), wg, wu, wd)
        y_l = [u.reshape(B, L, -1) for u in y_l]
        x_l = _resid_call(x_l, *y_l, ml[5], norm_final.reshape(1, D), min(L, 256), last)
    return x_l
```

```python
import functools

import jax
import jax.numpy as jnp
from jax import lax
from jax.experimental import pallas as pl
from jax.experimental.pallas import tpu as pltpu

F32 = jnp.float32
BF16 = jnp.bfloat16
HIGHEST = lax.Precision.HIGHEST
_NT = (((1,), (1,)), ((), ()))

EPS = 1e-6
NEG_INF = -1e30
N_MOD = 6
GRID_W = 64
DH = 64
DK_A = 128
CONV_W = 5
CHUNK = 64
KV_B = 2
WIN = 128
QBLK = 128
ROPE_BASE = 10000.0
KH_MAX = 8
KW = 16
N_GROUPS = 4
E_PER_GROUP = 8
N_EXPERTS = N_GROUPS * E_PER_GROUP
TOP_K = 2

LANE = 128
SUBLANE = 8
BF16_ROWS = 16

MOD_ROWS = BF16_ROWS
MOD_TN = 1024
ROUTER_COLS = LANE
INPROJ_TM = 1024
INPROJ_TN = (1280, 1024, 768, 640, 512, 384, 256, 128)
WPREP_ROWS = 256
ROW_TM = 256
MOE_TM = 256
ROUTE_TM = 256
GDN_HEADS = 2
GDN_GROUPS = (1, 2, 3, 6, 9)


def _params(sem, vmem_mb):
    return pltpu.CompilerParams(dimension_semantics=sem, vmem_limit_bytes=vmem_mb << 20)


def _mod_kernel(c_ref, w_ref, b_ref, o_ref):
    c = c_ref[...]
    s = c * jax.nn.sigmoid(c)
    o_ref[0] = jnp.dot(s, w_ref[0], preferred_element_type=F32, precision=HIGHEST) + b_ref[0]


def _mod_call(cc, w_mod, b_mod):
    depth, d, n = w_mod.shape
    tn = min(n, MOD_TN)
    return pl.pallas_call(
        _mod_kernel,
        grid=(depth, n // tn),
        in_specs=[pl.BlockSpec((MOD_ROWS, d), lambda l, j: (0, 0)),
                  pl.BlockSpec((1, d, tn), lambda l, j: (l, 0, j)),
                  pl.BlockSpec((1, 1, tn), lambda l, j: (l, 0, j))],
        out_specs=pl.BlockSpec((1, MOD_ROWS, tn), lambda l, j: (l, 0, j)),
        out_shape=jax.ShapeDtypeStruct((depth, MOD_ROWS, n), F32),
        compiler_params=_params(("parallel", "parallel"), 48),
        name="mod_proj",
    )(cc, w_mod, b_mod.reshape(depth, 1, n))


def _winprep_kernel(w_ref, m_ref, ab_ref, *, bounds):
    o_ab, o_qb, o_kvb, o_qc = bounds
    x = w_ref[0]
    m_ref[0] = jnp.concatenate([x[:, :o_ab], x[:, o_qb:o_kvb], x[:, o_qc:], x[:, o_kvb:o_qc]], 1).astype(BF16)
    ab = x[:, o_ab:o_qb]
    ab_ref[0] = jnp.concatenate([ab, jnp.zeros((x.shape[0], LANE - ab.shape[1]), F32)], 1).astype(BF16)


def _winprep_call(w_in, bounds):
    depth, d, n = w_in.shape
    n_main = n - (bounds[1] - bounds[0])
    rt = min(d, WPREP_ROWS)
    return pl.pallas_call(
        functools.partial(_winprep_kernel, bounds=bounds),
        grid=(depth, d // rt),
        in_specs=[pl.BlockSpec((1, rt, n), lambda l, i: (l, i, 0))],
        out_specs=[pl.BlockSpec((1, rt, n_main), lambda l, i: (l, i, 0)),
                   pl.BlockSpec((1, rt, LANE), lambda l, i: (l, i, 0))],
        out_shape=[jax.ShapeDtypeStruct((depth, d, n_main), BF16), jax.ShapeDtypeStruct((depth, d, LANE), BF16)],
        compiler_params=_params(("parallel", "parallel"), 48),
        name="w_in_prep",
    )(w_in)


def _cast_chunks(w, layer, n):
    depth, rows = w.shape[0], w.shape[1] * w.shape[2]
    if rows % n or (rows // n) % BF16_ROWS:
        return None
    return w.reshape(depth * n, rows // n, w.shape[3]), layer * n, n


def _cast_specs(cast, n_inner):
    chunks, first, n = cast
    blk = (1,) + chunks.shape[1:]
    src = pl.BlockSpec(blk, lambda b, j, *_: (first + b * n_inner + j, 0, 0))
    dst = pl.BlockSpec(blk, lambda b, j, *_: (b * n_inner + j, 0, 0))
    return src, dst, jax.ShapeDtypeStruct((n,) + chunks.shape[1:], BF16)


def _add_casts(casts, n_inner, in_specs, out_specs, out_shape):
    extra = ()
    for cast in casts or ():
        src, dst, shape = _cast_specs(cast, n_inner)
        in_specs.append(src)
        out_specs.append(dst)
        out_shape.append(shape)
        extra += (cast[0],)
    return extra


def _do_casts(rest):
    n_cast = len(rest) // 2
    for cast_src, cast_dst in zip(rest[:n_cast], rest[n_cast + 1:]):
        cast_dst[...] = cast_src[...].astype(BF16)
    return rest[n_cast]


def _inproj_kernel(x_ref, shift_ref, scale_ref, g_ref, w_ref, wab_ref, o_ref, ab_ref, h_ref):
    @pl.when(pl.program_id(2) == 0)
    def _():
        x = x_ref[0]
        y = x * lax.rsqrt(jnp.mean(x * x, -1, keepdims=True) + EPS) * g_ref[...]
        h = (y * (1.0 + scale_ref[0]) + shift_ref[0]).astype(BF16)
        h_ref[...] = h
        ab_ref[0] = jnp.dot(h, wab_ref[0], preferred_element_type=F32)

    o_ref[0] = jnp.dot(h_ref[...], w_ref[0], preferred_element_type=F32).astype(o_ref.dtype)


def _inproj_call(x, shift, scale, gain, w_main, w_ab, layer):
    bn, seq, d = x.shape
    n = w_main.shape[2]
    tm = min(seq, INPROJ_TM)
    tn = next((t for t in INPROJ_TN if n % t == 0), n)
    per_batch = shift.shape[0] > 1
    mod_map = (lambda b, i, j: (b, 0, 0)) if per_batch else (lambda b, i, j: (0, 0, 0))
    return pl.pallas_call(
        _inproj_kernel,
        grid=(bn, seq // tm, n // tn),
        in_specs=[pl.BlockSpec((1, tm, d), lambda b, i, j: (b, i, 0)),
                  pl.BlockSpec((1, 1, d), mod_map),
                  pl.BlockSpec((1, 1, d), mod_map),
                  pl.BlockSpec((1, d), lambda b, i, j: (0, 0)),
                  pl.BlockSpec((1, d, tn), lambda b, i, j: (layer, 0, j)),
                  pl.BlockSpec((1, d, LANE), lambda b, i, j: (layer, 0, 0))],
        out_specs=[pl.BlockSpec((1, tm, tn), lambda b, i, j: (b, i, j)),
                   pl.BlockSpec((1, tm, LANE), lambda b, i, j: (b, i, 0))],
        out_shape=[jax.ShapeDtypeStruct((bn, seq, n), BF16),
                   jax.ShapeDtypeStruct((bn, seq, LANE), F32)],
        scratch_shapes=[pltpu.VMEM((tm, d), BF16)],
        compiler_params=_params(("parallel", "parallel", "arbitrary"), 56),
        name="in_proj",
    )(x, shift, scale, gain, w_main, w_ab)


def _outproj_kernel(oa_ref, ob_ref, oc_ref, x_ref, gate_ref, shift_ref, scale_ref, g_ref, w_ref, wr_ref, br_ref,
                    xn_ref, f_ref, lg_ref):
    acc, r0 = None, 0
    for o_ref in (oa_ref, ob_ref, oc_ref):
        r1 = r0 + o_ref.shape[2]
        part = jnp.dot(o_ref[0], w_ref[r0:r1, :], preferred_element_type=F32)
        acc = part if acc is None else acc + part
        r0 = r1
    xn = x_ref[0] + gate_ref[0] * acc
    xn_ref[0] = xn
    y = xn * lax.rsqrt(jnp.mean(xn * xn, -1, keepdims=True) + EPS) * g_ref[...]
    f = y * (1.0 + scale_ref[0]) + shift_ref[0]
    f_hi = f.astype(BF16)
    f_ref[0] = f_hi
    f_lo = (f - f_hi.astype(F32)).astype(BF16)
    hh_hl = jnp.dot(f_hi, wr_ref[...], preferred_element_type=F32)
    lh = jnp.dot(f_lo, wr_ref[:, :ROUTER_COLS], preferred_element_type=F32)
    lg_ref[0] = hh_hl[:, :ROUTER_COLS] + hh_hl[:, ROUTER_COLS:] + lh + br_ref[...]


def _outproj_call(os, x, gate, shift, scale, gain, w_out, w_r, b_r):
    bn, seq, d = x.shape
    tm = min(seq, ROW_TM)
    per_batch = gate.shape[0] > 1
    mod_map = (lambda b, i: (b, 0, 0)) if per_batch else (lambda b, i: (0, 0, 0))
    row = pl.BlockSpec((1, tm, d), lambda b, i: (b, i, 0))
    o_specs = [pl.BlockSpec((1, tm, o.shape[2]), lambda b, i: (b, i, 0)) for o in os]
    return pl.pallas_call(
        _outproj_kernel,
        grid=(bn, seq // tm),
        in_specs=[*o_specs, row,
                  pl.BlockSpec((1, 1, d), mod_map),
                  pl.BlockSpec((1, 1, d), mod_map),
                  pl.BlockSpec((1, 1, d), mod_map),
                  pl.BlockSpec((1, d), lambda b, i: (0, 0)),
                  pl.BlockSpec((d, d), lambda b, i: (0, 0)),
                  pl.BlockSpec((d, 2 * ROUTER_COLS), lambda b, i: (0, 0)),
                  pl.BlockSpec((1, ROUTER_COLS), lambda b, i: (0, 0))],
        out_specs=[row, row, pl.BlockSpec((1, tm, ROUTER_COLS), lambda b, i: (b, i, 0))],
        out_shape=[jax.ShapeDtypeStruct((bn, seq, d), F32),
                   jax.ShapeDtypeStruct((bn, seq, d), BF16),
                   jax.ShapeDtypeStruct((bn, seq, ROUTER_COLS), F32)],
        compiler_params=_params(("parallel", "parallel"), 56),
        name="out_proj",
    )(*os, x, gate, shift, scale, gain, w_out, w_r, b_r)


def _moe_kernel(te_ref, nu_ref, x_ref, wg_ref, wu_ref, wd_ref, o_ref):
    used = pl.program_id(0) < nu_ref[0]

    @pl.when(used)
    def _():
        x = x_ref[...]
        g = jnp.dot(x, wg_ref[0], preferred_element_type=F32)
        u = jnp.dot(x, wu_ref[0], preferred_element_type=F32)
        a = (g * jax.nn.sigmoid(g) * u).astype(BF16)
        o_ref[...] = jnp.dot(a, wd_ref[0], preferred_element_type=F32).astype(o_ref.dtype)

    @pl.when(jnp.logical_not(used))
    def _():
        o_ref[...] = jnp.zeros_like(o_ref)


def _moe_call(tile_expert, n_used, buf, w_gate, w_up, w_down, tm):
    p, d = buf.shape
    f = w_gate.shape[-1]
    grid_spec = pltpu.PrefetchScalarGridSpec(
        num_scalar_prefetch=2,
        grid=(p // tm,),
        in_specs=[pl.BlockSpec((tm, d), lambda i, te, nu: (i, 0)),
                  pl.BlockSpec((1, d, f), lambda i, te, nu: (te[i], 0, 0)),
                  pl.BlockSpec((1, d, f), lambda i, te, nu: (te[i], 0, 0)),
                  pl.BlockSpec((1, f, d), lambda i, te, nu: (te[i], 0, 0))],
        out_specs=pl.BlockSpec((tm, d), lambda i, te, nu: (i, 0)),
    )
    return pl.pallas_call(
        _moe_kernel,
        grid_spec=grid_spec,
        out_shape=jax.ShapeDtypeStruct((p, d), BF16),
        compiler_params=_params(("arbitrary",), 60),
        name="moe_experts",
    )(tile_expert, n_used, buf, w_gate, w_up, w_down)


def _route_kernel(lg_ref, ids_ref, wts_ref, cnt_ref, carry_ref):
    @pl.when(pl.program_id(0) == 0)
    def _():
        carry_ref[...] = jnp.zeros_like(carry_ref)

    lg = lg_ref[...]
    tm = lg.shape[0]
    lane = lax.broadcasted_iota(jnp.int32, lg.shape, 1)
    big = jnp.int32(ROUTER_COLS)
    is_g = lane < N_GROUPS
    gmax = jnp.max(jnp.where(is_g, lg, -jnp.inf), -1, keepdims=True)
    g_sel = jnp.min(jnp.where(is_g & (lg == gmax), lane, big), -1, keepdims=True)
    p_g = 1.0 / jnp.sum(jnp.where(is_g, jnp.exp(lg - gmax), 0.0), -1, keepdims=True)
    lo = N_GROUPS + E_PER_GROUP * g_sel
    in_e = (lane >= lo) & (lane < lo + E_PER_GROUP)
    el = jnp.where(in_e, lg, -jnp.inf)
    v1 = jnp.max(el, -1, keepdims=True)
    i1 = jnp.min(jnp.where(el == v1, lane, big), -1, keepdims=True)
    el2 = jnp.where(lane == i1, -jnp.inf, el)
    v2 = jnp.max(el2, -1, keepdims=True)
    i2 = jnp.min(jnp.where(el2 == v2, lane, big), -1, keepdims=True)
    t = jnp.exp(v2 - v1)
    w1 = p_g / (1.0 + t)
    w2 = w1 * t
    e1 = i1 - N_GROUPS
    e2 = i2 - N_GROUPS
    hit1 = lane == e1
    hit2 = lane == e2
    onehot = jnp.where(hit1 | hit2, 1.0, 0.0)
    ri = lax.broadcasted_iota(jnp.int32, (tm, tm), 0)
    ci = lax.broadcasted_iota(jnp.int32, (tm, tm), 1)
    below = jnp.where(ci < ri, 1.0, 0.0).astype(BF16)
    before = carry_ref[...] + jnp.dot(below, onehot.astype(BF16), preferred_element_type=F32)
    r1 = jnp.sum(jnp.where(hit1, before, 0.0), -1, keepdims=True).astype(jnp.int32)
    r2 = jnp.sum(jnp.where(hit2, before, 0.0), -1, keepdims=True).astype(jnp.int32)
    ids_ref[...] = jnp.where(lane == 0, e1, jnp.where(lane == 1, e2, jnp.where(lane == 2, r1, jnp.where(lane == 3, r2, 0))))
    wts_ref[...] = jnp.where(lane == 0, w1, jnp.where(lane == 1, w2, 0.0))
    carry_ref[...] = carry_ref[...] + jnp.sum(onehot, 0, keepdims=True)
    cnt_ref[...] = carry_ref[...]


def _route_call(logits):
    t = logits.shape[0]
    tm = min(t, ROUTE_TM)
    row = pl.BlockSpec((tm, ROUTER_COLS), lambda i: (i, 0))
    return pl.pallas_call(
        _route_kernel,
        grid=(t // tm,),
        in_specs=[row],
        out_specs=[row, row, pl.BlockSpec((1, ROUTER_COLS), lambda i: (0, 0))],
        out_shape=[jax.ShapeDtypeStruct((t, ROUTER_COLS), jnp.int32),
                   jax.ShapeDtypeStruct((t, ROUTER_COLS), F32),
                   jax.ShapeDtypeStruct((1, ROUTER_COLS), F32)],
        scratch_shapes=[pltpu.VMEM((1, ROUTER_COLS), F32)],
        compiler_params=_params(("arbitrary",), 32),
        name="route",
    )(logits)


def _resid_kernel(x_ref, y0_ref, y1_ref, w_ref, gate_ref, g_ref, o_ref, *, final_norm):
    w = w_ref[0]
    y = y0_ref[0].astype(F32) * w[:, 0:1] + y1_ref[0].astype(F32) * w[:, 1:2]
    xn = x_ref[0] + gate_ref[0] * y
    if final_norm:
        xn = xn * lax.rsqrt(jnp.mean(xn * xn, -1, keepdims=True) + EPS) * g_ref[...]
    o_ref[0] = xn


def _resid_call(x, y0, y1, w, gate, gain, final_norm):
    bn, seq, d = x.shape
    tm = min(seq, ROW_TM)
    per_batch = gate.shape[0] > 1
    mod_map = (lambda b, i: (b, 0, 0)) if per_batch else (lambda b, i: (0, 0, 0))
    row = pl.BlockSpec((1, tm, d), lambda b, i: (b, i, 0))
    return pl.pallas_call(
        functools.partial(_resid_kernel, final_norm=final_norm),
        grid=(bn, seq // tm),
        in_specs=[row, row, row, pl.BlockSpec((1, tm, ROUTER_COLS), lambda b, i: (b, i, 0)),
                  pl.BlockSpec((1, 1, d), mod_map), pl.BlockSpec((1, d), lambda b, i: (0, 0))],
        out_specs=row,
        out_shape=jax.ShapeDtypeStruct((bn, seq, d), F32),
        compiler_params=_params(("parallel", "parallel"), 48),
        name="moe_residual",
    )(x, y0, y1, w, gate, gain)


def _softplus(x):
    return jnp.maximum(x, 0.0) + jnp.log1p(jnp.exp(-jnp.abs(x)))


def _gdn_prep(x_ref, w_ref, lanes, dst_ref, off, mode, xp_ref):
    n = x_ref.shape[1]
    pad = CONV_W // 2
    edge = jnp.zeros((SUBLANE, DK_A), F32)
    xp_ref[0:SUBLANE, :] = edge
    xp_ref[n + SUBLANE:n + 2 * SUBLANE, :] = edge
    xp_ref[SUBLANE:n + SUBLANE, :] = x_ref[0, :, lanes].astype(F32)
    acc = None
    for o in range(-pad, pad + 1):
        term = xp_ref[SUBLANE + o:SUBLANE + o + n, :] * w_ref[o + pad:o + pad + 1, lanes]
        acc = term if acc is None else acc + term
    y = acc * jax.nn.sigmoid(acc)
    if mode != "v":
        y = y * lax.rsqrt(jnp.sum(y * y, -1, keepdims=True) + EPS)
    if mode == "q":
        y = y * DK_A ** -0.5
    dst_ref[off:off + n, :] = y


def _gdn_kernel(alog_ref, dtb_ref,
                qc_ref, kc_ref, vc_ref, zc_ref, ql_ref, kl_ref, vl_ref, zl_ref,
                wq_ref, wk_ref, wv_ref, grow_ref, bcol_ref, nw_ref,
                oc_ref, ol_ref,
                q_s, k_s, v_s, o_s, u_s, l1_s, l2_s, gl_s, gc_s, s_s, xp_s, *, h_a, n_ctx):
    C = CHUNK
    hp = o_s.shape[0]
    lc = qc_ref.shape[1]
    n_tot = u_s.shape[2]
    n_pairs = n_tot // 2
    group = max(g for g in GDN_GROUPS if n_pairs % g == 0)

    NB = 4
    W = NB * C
    ii = lax.broadcasted_iota(jnp.int32, (C, W), 0)
    jl = lax.broadcasted_iota(jnp.int32, (C, W), 1)
    blk = jl // C
    jm = jl - blk * C
    eye4 = ii == jm
    tri4 = ((blk < 2) & (ii >= jm)) | ((blk >= 2) & (ii <= jm))
    eye4f = eye4.astype(F32)
    lane2 = lax.broadcasted_iota(jnp.int32, (C, 2 * C), 1)
    lo2 = lane2 < C
    i2 = lax.broadcasted_iota(jnp.int32, (2 * C, 2 * C), 0)
    j2 = lax.broadcasted_iota(jnp.int32, (2 * C, 2 * C), 1)
    same = (i2 // C) == (j2 // C)

    def block_diag(x):
        return jnp.concatenate([jnp.where(blk == b, x, 0.0) for b in range(NB)], 0).astype(BF16)

    def head_terms(hh):
        head = pl.program_id(1) * hp + hh
        lanes = slice(hh * DK_A, (hh + 1) * DK_A)
        _gdn_prep(qc_ref, wq_ref, lanes, q_s, 0, "q", xp_s)
        _gdn_prep(kc_ref, wk_ref, lanes, k_s, 0, "k", xp_s)
        _gdn_prep(vc_ref, wv_ref, lanes, v_s, 0, "v", xp_s)
        _gdn_prep(ql_ref, wq_ref, lanes, q_s, lc, "q", xp_s)
        _gdn_prep(kl_ref, wk_ref, lanes, k_s, lc, "k", xp_s)
        _gdn_prep(vl_ref, wv_ref, lanes, v_s, lc, "v", xp_s)

        for d in range(2):
            a = jnp.exp(jnp.full((1, 2 * C), alog_ref[d * h_a + head], F32))
            g = -a * _softplus(grow_ref[0, hh, d] + dtb_ref[d * h_a + head])
            tri = same & ((i2 <= j2) if d == 0 else (i2 >= j2))
            gc_s[d] = jnp.dot(g, tri.astype(F32), preferred_element_type=F32, precision=HIGHEST)

        def pair_setup(p):
            r0 = pl.multiple_of(p * 2 * C, 2 * C)
            q2 = q_s[pl.ds(r0, 2 * C), :]
            k2 = k_s[pl.ds(r0, 2 * C), :]
            v2 = v_s[pl.ds(r0, 2 * C), :]
            k16 = k2.astype(BF16)
            gk = lax.dot_general(k16, k16, _NT, preferred_element_type=F32)
            gq = lax.dot_general(q2.astype(BF16), k16, _NT, preferred_element_type=F32)
            kk2 = jnp.where(lo2, gk[:C], gk[C:])
            qk2 = jnp.where(lo2, gq[:C], gq[C:])
            kk4 = jnp.concatenate([kk2, kk2], 1)
            qk4 = jnp.concatenate([qk2, qk2], 1)
            grow = [gc_s[d, pl.ds(p, 1), :] for d in range(2)]
            rowb = jnp.concatenate([jnp.broadcast_to(g, (C, 2 * C)) for g in grow], 1)
            diag = jnp.where(eye4, rowb, 0.0)
            gcol, bcol, tot = [], [], []
            for b in range(NB):
                d, par = b // 2, b % 2
                half = diag[:, d * 2 * C:(d + 1) * 2 * C]
                gcol.append(jnp.sum(jnp.where(lo2 if par == 0 else ~lo2, half, 0.0), axis=1, keepdims=True))
                bcol.append(jax.nn.sigmoid(bcol_ref[0, hh, pl.ds(r0 + par * C, C), d:d + 1]))
                e = par * C + (C - 1 if d == 0 else 0)
                tot.append(grow[d][:, e:e + 1])
            gb = [jnp.broadcast_to(g, (C, 2 * C)) for g in gcol]
            bb = [jnp.broadcast_to(b_, (C, 2 * C)) for b_ in bcol]
            gcc4 = jnp.concatenate([jnp.where(lo2, gb[0], gb[1]), jnp.where(lo2, gb[2], gb[3])], 1)
            bcc4 = jnp.concatenate([jnp.where(lo2, bb[0], bb[1]), jnp.where(lo2, bb[2], bb[3])], 1)
            dec4 = jnp.where(tri4, jnp.exp(jnp.where(tri4, gcc4 - rowb, 0.0)), 0.0)
            m0 = jnp.where(eye4, 0.0, -(kk4 * bcc4 * dec4))
            return m0, (p, q2, k2, v2, qk4 * dec4, gb, bb, tot)

        def pair_finish(tinv4, aux):
            p, q2, k2, v2, attn4, gb, bb, tot = aux
            eb = [jnp.exp(g) for g in gb]
            ks = (k2[:C], k2[C:])
            vs = (v2[:C], v2[C:])
            qs = (q2[:C], q2[C:])
            rhs = jnp.concatenate([jnp.concatenate([vs[b % 2] * bb[b], ks[b % 2] * (bb[b] * eb[b])], 1)
                                   for b in range(NB)], 0).astype(BF16)
            uw = jnp.dot(block_diag(tinv4), rhs, preferred_element_type=F32)
            for b in range(NB):
                d, par = b // 2, b % 2
                c = 2 * p + par
                u_s[hh, d, c] = uw[b * C:(b + 1) * C, :DK_A]
                l1_s[hh, d, c] = jnp.concatenate([uw[b * C:(b + 1) * C, DK_A:], qs[par] * eb[b]], 0).astype(BF16)
                gl_s[hh, d, c] = jnp.broadcast_to(jnp.exp(tot[b]), (1, DK_A))
            for d in range(2):
                kdec = jnp.concatenate([ks[par] * jnp.exp(tot[2 * d + par] - gb[2 * d + par]) for par in range(2)], 0)
                l2_s[hh, d, p] = jnp.concatenate([attn4[:, d * 2 * C:(d + 1) * 2 * C], kdec.T], 0).astype(BF16)

        def group_terms(i, _):
            st = [pair_setup(i * group + g) for g in range(group)]
            m0s = [s[0] for s in st]
            rs = [jnp.dot(m0.astype(BF16), block_diag(m0), preferred_element_type=F32) for m0 in m0s]
            qms = [eye4f + m0 for m0 in m0s]
            for _ in range(4):
                qrs = [jnp.dot(jnp.concatenate([qm, r], 0).astype(BF16), block_diag(r), preferred_element_type=F32)
                       for qm, r in zip(qms, rs)]
                qms = [qm + qr[:C] for qm, qr in zip(qms, qrs)]
                rs = [qr[C:] for qr in qrs]
            tinvs = [qm + jnp.dot(qm.astype(BF16), block_diag(r), preferred_element_type=F32)
                     for qm, r in zip(qms, rs)]
            for tinv4, s in zip(tinvs, st):
                pair_finish(tinv4, s[1])
            return 0

        lax.fori_loop(0, n_pairs // group, group_terms, 0)

    for hh in range(hp):
        head_terms(hh)

    o_s[...] = jnp.zeros_like(o_s)
    s_s[...] = jnp.zeros_like(s_s)
    row_half = lax.broadcasted_iota(jnp.int32, (2 * C, DK_A), 0) // C

    def scan_step(i, _):
        c_b = jnp.where(i < n_ctx, n_ctx - 1 - i, n_ctx + n_tot - 1 - i)
        chains = [(hh, d, i if d == 0 else c_b) for hh in range(hp) for d in range(2)]
        r1s = [jnp.dot(l1_s[hh, d, c], s_s[hh, d].astype(BF16), preferred_element_type=F32)
               for hh, d, c in chains]
        v_exts = []
        for (hh, d, c), r1 in zip(chains, r1s):
            v_new = u_s[hh, d, c] - r1[:C]
            v_exts.append(jnp.where(row_half == c % 2, jnp.concatenate([v_new, v_new], 0), 0.0).astype(BF16))
        r2s = [jnp.dot(l2_s[hh, d, c // 2], v_ext, preferred_element_type=F32)
               for (hh, d, c), v_ext in zip(chains, v_exts)]
        for (hh, d, c), r1, r2 in zip(chains, r1s, r2s):
            rows = pl.ds(pl.multiple_of(c * C, C), C)
            o_s[hh, rows, :] = o_s[hh, rows, :] + r1[C:] + r2[:C]
            s_s[hh, d] = s_s[hh, d] * gl_s[hh, d, c] + r2[C:]
        return 0

    lax.fori_loop(0, n_tot, scan_step, 0)

    for hh in range(hp):
        lanes = slice(hh * DK_A, (hh + 1) * DK_A)
        for z_ref, o_ref, off in ((zc_ref, oc_ref, 0), (zl_ref, ol_ref, lc)):
            n = z_ref.shape[1]
            o = o_s[hh, off:off + n, :]
            o = o * lax.rsqrt(jnp.mean(o * o, -1, keepdims=True) + EPS) * nw_ref[...]
            z = z_ref[0, :, lanes].astype(F32)
            o_ref[0, :, lanes] = (o * (z * jax.nn.sigmoid(z))).astype(o_ref.dtype)


def _gdn_call(p_c, p_l, ab_c, ab_l, conv_w, a_log, dt_bias, norm_w, h_a):
    B, L, _ = p_l.shape
    lc = p_c.shape[1]
    t = lc + L
    n_tot, n_ctx = t // CHUNK, lc // CHUNK
    assert n_tot % 2 == 0
    hp = GDN_HEADS if h_a % GDN_HEADS == 0 else 1
    n_hg = h_a // hp
    wd = hp * DK_A
    ab = jnp.concatenate([ab_c, ab_l], 1)[..., :4 * h_a].reshape(B, t, 2, 2, h_a)
    g_row = ab[:, :, :, 0].transpose(0, 3, 2, 1).reshape(B, h_a, 2, n_tot // 2, 2 * CHUNK)
    b_col = ab[:, :, :, 1].transpose(0, 3, 1, 2)

    def col(k, n):
        return pl.BlockSpec((1, n, wd), lambda b, j, *_: (b, 0, k * n_hg + j))

    def tap(k):
        return pl.BlockSpec((CONV_W, wd), lambda b, j, *_: (0, k * n_hg + j))

    grid_spec = pltpu.PrefetchScalarGridSpec(
        num_scalar_prefetch=2,
        grid=(B, n_hg),
        in_specs=[col(0, lc), col(1, lc), col(2, lc), col(3, lc),
                  col(0, L), col(1, L), col(2, L), col(3, L),
                  tap(0), tap(1), tap(2),
                  pl.BlockSpec((1, hp, 2, n_tot // 2, 2 * CHUNK), lambda b, j, *_: (b, j, 0, 0, 0)),
                  pl.BlockSpec((1, hp, t, 2), lambda b, j, *_: (b, j, 0, 0)),
                  pl.BlockSpec((1, DK_A), lambda b, j, *_: (0, 0))],
        out_specs=[pl.BlockSpec((1, lc, wd), lambda b, j, *_: (b, 0, j)),
                   pl.BlockSpec((1, L, wd), lambda b, j, *_: (b, 0, j))],
        scratch_shapes=[pltpu.VMEM((t, DK_A), F32), pltpu.VMEM((t, DK_A), F32), pltpu.VMEM((t, DK_A), F32),
                        pltpu.VMEM((hp, t, DK_A), F32),
                        pltpu.VMEM((hp, 2, n_tot, CHUNK, DK_A), F32),
                        pltpu.VMEM((hp, 2, n_tot, 2 * CHUNK, DK_A), BF16),
                        pltpu.VMEM((hp, 2, n_tot // 2, CHUNK + DK_A, 2 * CHUNK), BF16),
                        pltpu.VMEM((hp, 2, n_tot, 1, DK_A), F32),
                        pltpu.VMEM((2, n_tot // 2, 2 * CHUNK), F32),
                        pltpu.VMEM((hp, 2, DK_A, DK_A), F32),
                        pltpu.VMEM((max(L, lc) + 2 * SUBLANE, DK_A), F32)],
    )
    return pl.pallas_call(
        functools.partial(_gdn_kernel, h_a=h_a, n_ctx=n_ctx),
        grid_spec=grid_spec,
        out_shape=[jax.ShapeDtypeStruct((B, lc, h_a * DK_A), BF16), jax.ShapeDtypeStruct((B, L, h_a * DK_A), BF16)],
        compiler_params=_params(("parallel", "parallel"), 56),
        name="gdn",
    )(a_log.reshape(-1), dt_bias.reshape(-1), p_c, p_c, p_c, p_c, p_l, p_l, p_l, p_l,
      conv_w, conv_w, conv_w, g_row, b_col, norm_w.reshape(1, DK_A))


def _softmax_rows(s, sink=None):
    m = jnp.max(s, -1, keepdims=True)
    if sink is not None:
        m = jnp.maximum(m, sink)
    e = jnp.exp(s - m)
    den = jnp.sum(e, -1, keepdims=True)
    if sink is not None:
        den = den + jnp.exp(sink - m)
    return (e * (1.0 / den)).astype(BF16)


def _gqa_kernel(sink_ref, q_ref, kvl_ref, kvc_ref, cos_ref, sa_ref, sb_ref, *rest, n_heads, latent):
    o_ref = _do_casts(rest)
    G = n_heads // KV_B
    P = 2 * DH
    n = pl.program_id(1)
    seq = kvl_ref.shape[1]
    lc = kvc_ref.shape[1]

    def rope(x, r0):
        reps = x.shape[1] // P
        rows = pl.ds(pl.multiple_of(r0, QBLK), QBLK)
        c, a, b = [jnp.concatenate([t[rows, :]] * reps, 1) if reps > 1 else t[rows, :]
                   for t in (cos_ref, sa_ref, sb_ref)]
        w = x.shape[1]
        return x * c + pltpu.roll(x, w - DH // 4, 1) * a + pltpu.roll(x, DH // 4, 1) * b

    q = q_ref[0].astype(F32) * DH ** -0.5
    kc = kvc_ref[0][:, :P].astype(F32)
    vc = kvc_ref[0][:, P:].astype(F32)
    if latent:
        q = rope(q, n * QBLK)
        ks, vs = [], []
        for o in (-1, 0, 1):
            s0 = jnp.clip((n + o) * QBLK, 0, seq - QBLK)
            kv = kvl_ref[0, pl.ds(pl.multiple_of(s0, QBLK), QBLK), :].astype(F32)
            ks.append(rope(kv[:, :P], s0))
            vs.append(kv[:, P:])
        k = jnp.concatenate(ks + [kc], 0)
        v = jnp.concatenate(vs + [vc], 0)
        nk = 3 * QBLK + lc
        qi = lax.broadcasted_iota(jnp.int32, (QBLK, nk), 0)
        kj = lax.broadcasted_iota(jnp.int32, (QBLK, nk), 1)
        kpos = n * QBLK - QBLK + kj
        rel = kj - QBLK - qi
        valid = ((rel >= -WIN) & (rel <= WIN) & (kpos >= 0) & (kpos < seq)) | (kj >= 3 * QBLK)
    else:
        k, v = kc, vc
        valid = None
    q = q.astype(BF16)
    k_sw = pltpu.roll(k, DH, 1).astype(BF16)
    v_sw = pltpu.roll(v, DH, 1).astype(BF16)
    k = k.astype(BF16)
    v = v.astype(BF16)
    lane = lax.broadcasted_iota(jnp.int32, (QBLK, P), 1)
    zero = jnp.zeros((QBLK, P), BF16)
    scores = []
    for h in range(n_heads):
        qp = q[:, (h // 2) * P:(h // 2 + 1) * P]
        qh = jnp.where(lane < DH, qp, zero) if h % 2 == 0 else jnp.where(lane >= DH, qp, zero)
        straight = (h // G == h % 2)
        scores.append(lax.dot_general(qh, k if straight else k_sw, _NT, preferred_element_type=F32))
    probs = []
    for h, s in enumerate(scores):
        if valid is not None:
            s = jnp.where(valid, s, NEG_INF)
        probs.append(_softmax_rows(s, sink_ref[h]))
    outs = [jnp.dot(pr, v if (h // G == h % 2) else v_sw, preferred_element_type=F32) for h, pr in enumerate(probs)]
    for p in range(n_heads // 2):
        o_ref[0, :, p * P:(p + 1) * P] = jnp.where(lane < DH, outs[2 * p], outs[2 * p + 1]).astype(o_ref.dtype)


def _gqa_call(p_q, p_l, p_c, sink, tables, d_model, d_b, latent, casts=None):
    B, nq, _ = p_q.shape
    L, lc = p_l.shape[1], p_c.shape[1]
    kvw = 2 * KV_B * DH
    n_heads = d_b // DH
    assert (n_heads // KV_B) % 2 == 0
    in_specs = [pl.BlockSpec((1, QBLK, d_b), lambda b, n, *_: (b, n, 2 * d_model // d_b)),
                pl.BlockSpec((1, L, kvw), lambda b, n, *_: (b, 0, 3 * d_model // kvw)),
                pl.BlockSpec((1, lc, kvw), lambda b, n, *_: (b, 0, 3 * d_model // kvw)),
                pl.BlockSpec((L, 2 * DH), lambda b, n, *_: (0, 0)),
                pl.BlockSpec((L, 2 * DH), lambda b, n, *_: (0, 0)),
                pl.BlockSpec((L, 2 * DH), lambda b, n, *_: (0, 0))]
    out_specs = [pl.BlockSpec((1, QBLK, d_b), lambda b, n, *_: (b, n, 0))]
    out_shape = [jax.ShapeDtypeStruct((B, nq, d_b), BF16)]
    extra = _add_casts(casts, nq // QBLK, in_specs, out_specs, out_shape)
    grid_spec = pltpu.PrefetchScalarGridSpec(
        num_scalar_prefetch=1, grid=(B, nq // QBLK), in_specs=in_specs, out_specs=out_specs)
    return pl.pallas_call(
        functools.partial(_gqa_kernel, n_heads=n_heads, latent=latent),
        grid_spec=grid_spec,
        out_shape=out_shape,
        compiler_params=_params(("parallel", "parallel"), 48),
        name="window_gqa" if latent else "ctx_gqa",
    )(sink, p_q, p_l, p_c, *tables, *extra)


def _rope_tables(n_tokens):
    t = jnp.arange(n_tokens)
    row = (t // GRID_W).astype(F32)
    col = (t % GRID_W).astype(F32)
    half = DH // 2
    inv = ROPE_BASE ** (-jnp.arange(0, half, 2, dtype=F32) / half)
    ang_r = row[:, None] * inv[None]
    ang_c = col[:, None] * inv[None]
    ang = jnp.concatenate([ang_r, ang_r, ang_c, ang_c], -1)
    cos, sin = jnp.cos(ang), jnp.sin(ang)
    first = (jnp.arange(DH) % half) < half // 2
    sa = jnp.where(first, -sin, 0.0)
    sb = jnp.where(first, 0.0, sin)
    return tuple(jnp.concatenate([u, u], -1) for u in (cos, sa, sb))


def _na_kernel(q_ref, k_ref, v_ref, kc_ref, vc_ref, tbl_ref, *rest, n_heads, rows, latent):
    o_ref = _do_casts(rest)
    P = 2 * DH
    r = pl.program_id(1)
    nq = q_ref.shape[1]
    nb = KH_MAX * GRID_W
    q = (q_ref[0].astype(F32) * DH ** -0.5).astype(BF16)
    lane = lax.broadcasted_iota(jnp.int32, (nq, P), 1)
    zero = jnp.zeros((nq, P), BF16)
    if latent:
        r0 = jnp.clip(r - KH_MAX // 2, 0, rows - KH_MAX)
        band = pl.ds(pl.multiple_of(r0 * GRID_W, GRID_W), nb)
        d0 = r0 - r + (KH_MAX - 1)
    scores, vals = [], []
    for p in range(n_heads // 2):
        cols = slice(p * P, (p + 1) * P)
        qp = q[:, cols]
        kc = kc_ref[0, :, cols]
        vc = vc_ref[0, :, cols]
        if latent:
            kk = jnp.concatenate([k_ref[0, band, cols], kc], 0)
            vals.append(jnp.concatenate([v_ref[0, band, cols], vc], 0))
        else:
            kk = kc
            vals.append(vc)
        for half in range(2):
            qh = jnp.where(lane < DH, qp, zero) if half == 0 else jnp.where(lane >= DH, qp, zero)
            scores.append(lax.dot_general(qh, kk, _NT, preferred_element_type=F32))
    probs = []
    for h, s in enumerate(scores):
        if latent:
            bias = jnp.concatenate([tbl_ref[h, d0 + 2 * t] for t in range(KH_MAX // 2)], 1)
            s = jnp.concatenate([s[:, :nb] + bias, s[:, nb:]], 1)
        probs.append(_softmax_rows(s))
    outs = [jnp.dot(pr, vals[h // 2], preferred_element_type=F32) for h, pr in enumerate(probs)]
    for p in range(n_heads // 2):
        o_ref[0, :, p * P:(p + 1) * P] = jnp.where(lane < DH, outs[2 * p], outs[2 * p + 1]).astype(o_ref.dtype)


def _na_bias_table(rpb):
    qc = jnp.arange(GRID_W)[:, None]
    kc = jnp.arange(GRID_W)[None, :]
    dc = jnp.clip(kc - qc + (KW - 1), 0, 2 * KW - 2)
    q_cs = jnp.clip(qc - KW // 2, 0, GRID_W - KW)
    in_win = (kc >= q_cs) & (kc < q_cs + KW)
    t = jnp.where(in_win[None, None], rpb[:, :, dc], NEG_INF)
    return jnp.concatenate([t[:, :-1], t[:, 1:]], -1)


def _na_call(p_q, p_l, p_c, tbl, d_model, d_c, latent, casts=None):
    B, nq, _ = p_q.shape
    L, lc = p_l.shape[1], p_c.shape[1]
    n_heads = d_c // DH
    rows = L // GRID_W
    assert rows >= KH_MAX and n_heads % 2 == 0
    cb = 2 * d_model // d_c
    qb = GRID_W
    in_specs = [pl.BlockSpec((1, qb, d_c), lambda b, r: (b, r, cb + 1)),
                pl.BlockSpec((1, L, d_c), lambda b, r: (b, 0, cb + 2)),
                pl.BlockSpec((1, L, d_c), lambda b, r: (b, 0, cb + 3)),
                pl.BlockSpec((1, lc, d_c), lambda b, r: (b, 0, cb + 2)),
                pl.BlockSpec((1, lc, d_c), lambda b, r: (b, 0, cb + 3)),
                pl.BlockSpec(tbl.shape, lambda b, r: (0, 0, 0, 0))]
    out_specs = [pl.BlockSpec((1, qb, d_c), lambda b, r: (b, r, 0))]
    out_shape = [jax.ShapeDtypeStruct((B, nq, d_c), BF16)]
    extra = _add_casts(casts, nq // qb, in_specs, out_specs, out_shape)
    return pl.pallas_call(
        functools.partial(_na_kernel, n_heads=n_heads, rows=rows, latent=latent),
        grid=(B, nq // qb),
        in_specs=in_specs,
        out_specs=out_specs,
        out_shape=out_shape,
        compiler_params=_params(("parallel", "parallel"), 48),
        name="nbr_attn" if latent else "ctx_attn",
    )(p_q, p_l, p_l, p_c, p_c, tbl, *extra)


def _moe(f, logits, w_gate, w_up, w_down, split=0):
    T, d = f.shape
    tm = MOE_TM
    ids, wts, cnt = _route_call(logits)
    expert, rank = ids[:, :TOP_K], ids[:, TOP_K:2 * TOP_K]
    counts = cnt[0, :N_EXPERTS].astype(jnp.int32)
    padded = (counts + tm - 1) // tm * tm
    pends = jnp.cumsum(padded)
    pstarts = pends - padded
    start = jnp.sum(jnp.where(expert[..., None] == jnp.arange(N_EXPERTS), pstarts, 0), -1)
    dest = start + rank
    n_tiles = -(-T * TOP_K // tm) + N_EXPERTS
    tile_expert = jnp.minimum(jnp.sum(pends[None, :] <= (jnp.arange(n_tiles) * tm)[:, None], -1),
                              N_EXPERTS - 1).astype(jnp.int32)
    n_used = (pends[-1] // tm).astype(jnp.int32).reshape(1)
    tok = jnp.broadcast_to(jnp.arange(T, dtype=jnp.int32)[:, None], (T, TOP_K))
    src_tok = (jnp.arange(n_tiles * tm, dtype=jnp.int32) % T).at[dest.reshape(-1)].set(tok.reshape(-1))
    out = _moe_call(tile_expert, n_used, f[src_tok], w_gate, w_up, w_down, tm)
    parts = [(dest[:split], wts[:split]), (dest[split:], wts[split:])]
    return [(out[d_[:, 0]], out[d_[:, 1]], w_) for d_, w_ in parts]


def kernel(x, c, ctx, c_ctx, w_mod, b_mod, norm_mix, norm_ffn, w_in, conv_a, a_log, dt_bias, gdn_norm, sink_b, rpb_c, w_out, w_router_group, b_router_group, w_router_expert, b_router_expert, w_gate, w_up, w_down, norm_final):
    B, L, D = x.shape
    Lc = ctx.shape[1]
    depth = w_mod.shape[0]
    d_a = D // 2
    d_b = D // 4
    d_c = D - d_a - d_b
    h_a = d_a // DK_A
    n_ab = 4 * h_a
    kvb = 2 * KV_B * DH
    n_e, _, d_f = w_gate.shape[1:]
    o_ab = 4 * d_a
    o_qb = o_ab + n_ab
    o_kvb = o_qb + d_b

    cc = jnp.zeros((MOD_ROWS, D), F32).at[:B].set(c).at[B].set(c_ctx)
    mod = _mod_call(cc, w_mod, b_mod).reshape(depth, MOD_ROWS, N_MOD, D)
    rope = _rope_tables(L)
    w_main, w_ab = _winprep_call(w_in, (o_ab, o_qb, o_kvb, o_kvb + kvb))

    x_l, x_c = x, ctx
    for l in range(depth):
        last = l == depth - 1
        ctx_out = not last
        mod_l = mod[l, :B]
        mod_c = mod[l, B:B + 1]
        ml = [mod_l[:, i:i + 1] for i in range(N_MOD)]
        mc = [mod_c[:, i:i + 1] for i in range(N_MOD)]
        gain_mix = norm_mix[l].reshape(1, D)
        p_l, ab_l = _inproj_call(x_l, ml[0], ml[1], gain_mix, w_main, w_ab, l)
        p_c, ab_c = _inproj_call(x_c.reshape(1, B * Lc, D), mc[0], mc[1], gain_mix, w_main, w_ab, l)
        p_c, ab_c = p_c.reshape(B, Lc, -1), ab_c.reshape(B, Lc, -1)

        cast_u = _cast_chunks(w_up, l, B * (L // QBLK))
        cast_gd = [_cast_chunks(w, l, B * (L // GRID_W)) for w in (w_gate, w_down)]
        oA_c, oA_l = _gdn_call(p_c, p_l, ab_c, ab_l, conv_a[l], a_log[l], dt_bias[l], gdn_norm[l], h_a)
        oB_l, *wu = _gqa_call(p_l, p_l, p_c, sink_b[l], rope, D, d_b, True, [cast_u] if cast_u else None)
        tbl = _na_bias_table(rpb_c[l])
        oC_l, *wgd = _na_call(p_l, p_l, p_c, tbl, D, d_c, True, cast_gd if all(cast_gd) else None)
        wu = wu[0].reshape(n_e, D, d_f) if wu else w_up[l].astype(BF16)
        wg = wgd[0].reshape(n_e, D, d_f) if wgd else w_gate[l].astype(BF16)
        wd = wgd[1].reshape(n_e, d_f, D) if wgd else w_down[l].astype(BF16)
        if ctx_out:
            oB_c, = _gqa_call(p_c, p_l, p_c, sink_b[l], rope, D, d_b, False)
            oC_c, = _na_call(p_c, p_l, p_c, tbl, D, d_c, False)

        w_o = w_out[l].astype(BF16)
        w_r = jnp.pad(jnp.concatenate([w_router_group[l], w_router_expert[l]], -1),
                      ((0, 0), (0, ROUTER_COLS - N_GROUPS - N_EXPERTS)))
        w_r_hi = w_r.astype(BF16)
        w_r = jnp.concatenate([w_r_hi, (w_r - w_r_hi.astype(F32)).astype(BF16)], -1)
        b_r = jnp.pad(jnp.concatenate([b_router_group[l], b_router_expert[l]], -1),
                      (0, ROUTER_COLS - N_GROUPS - N_EXPERTS)).reshape(1, ROUTER_COLS)
        gain_ffn = norm_ffn[l].reshape(1, D)
        x_l, f_l, lg_l = _outproj_call((oA_l, oB_l, oC_l), x_l, ml[2], ml[3], ml[4], gain_ffn, w_o, w_r, b_r)
        if ctx_out:
            x_c, f_c, lg_c = _outproj_call((oA_c, oB_c, oC_c), x_c, mc[2], mc[3], mc[4], gain_ffn, w_o, w_r, b_r)
            f_all = jnp.concatenate([f_c.reshape(B * Lc, D), f_l.reshape(B * L, D)], 0)
            lg_all = jnp.concatenate([lg_c.reshape(B * Lc, ROUTER_COLS), lg_l.reshape(B * L, ROUTER_COLS)], 0)
            y_c, y_l = _moe(f_all, lg_all, wg, wu, wd, B * Lc)
            y_c = [u.reshape(B, Lc, -1) for u in y_c]
            x_c = _resid_call(x_c, *y_c, mc[5], gain_ffn, False)
        else:
            _, y_l = _moe(f_l.reshape(B * L, D), lg_l.reshape(B * L, ROUTER_COLS), wg, wu, wd)
        y_l = [u.reshape(B, L, -1) for u in y_l]
        x_l = _resid_call(x_l, *y_l, ml[5], norm_final.reshape(1, D), last)
    return x_l
```

```python
import functools

import jax
import jax.numpy as jnp
from jax import lax
from jax.experimental import pallas as pl
from jax.experimental.pallas import tpu as pltpu

F32 = jnp.float32
BF16 = jnp.bfloat16
HIGHEST = lax.Precision.HIGHEST
_NT = (((1,), (1,)), ((), ()))

EPS = 1e-6
NEG_INF = -1e30
N_MOD = 6
GRID_W = 64
DH = 64
DK_A = 128
CONV_W = 5
CHUNK = 64
KV_B = 2
WIN = 128
QBLK = 128
ROPE_BASE = 10000.0
KH_MAX = 8
KW = 16
N_GROUPS = 4
E_PER_GROUP = 8
N_EXPERTS = N_GROUPS * E_PER_GROUP
TOP_K = 2

LANE = 128
SUBLANE = 8
BF16_ROWS = 16

MOD_ROWS = BF16_ROWS
MOD_TN = 1024
ROUTER_COLS = LANE
INPROJ_TM = 1024
INPROJ_TN = (1280, 1024, 768, 640, 512, 384, 256, 128)
WPREP_ROWS = 256
ROW_TM = 256
MOE_TM = 256
ROUTE_TM = 256
GDN_HEADS = 2
GDN_GROUPS = (1, 2, 3, 6, 9)
INV_BLK = 8
INV_DOUBLINGS = INV_BLK.bit_length() - 1


def _params(sem, vmem_mb):
    return pltpu.CompilerParams(dimension_semantics=sem, vmem_limit_bytes=vmem_mb << 20)


def _mod_kernel(c_ref, w_ref, b_ref, o_ref):
    c = c_ref[...]
    s = c * jax.nn.sigmoid(c)
    o_ref[0] = jnp.dot(s, w_ref[0], preferred_element_type=F32, precision=HIGHEST) + b_ref[0]


def _mod_call(cc, w_mod, b_mod):
    depth, d, n = w_mod.shape
    tn = min(n, MOD_TN)
    return pl.pallas_call(
        _mod_kernel,
        grid=(depth, n // tn),
        in_specs=[pl.BlockSpec((MOD_ROWS, d), lambda l, j: (0, 0)),
                  pl.BlockSpec((1, d, tn), lambda l, j: (l, 0, j)),
                  pl.BlockSpec((1, 1, tn), lambda l, j: (l, 0, j))],
        out_specs=pl.BlockSpec((1, MOD_ROWS, tn), lambda l, j: (l, 0, j)),
        out_shape=jax.ShapeDtypeStruct((depth, MOD_ROWS, n), F32),
        compiler_params=_params(("parallel", "parallel"), 48),
        name="mod_proj",
    )(cc, w_mod, b_mod.reshape(depth, 1, n))


def _winprep_kernel(w_ref, m_ref, ab_ref, *, bounds):
    o_ab, o_qb, o_kvb, o_qc = bounds
    x = w_ref[0]
    m_ref[0] = jnp.concatenate([x[:, :o_ab], x[:, o_qb:o_kvb], x[:, o_qc:], x[:, o_kvb:o_qc]], 1).astype(BF16)
    ab = x[:, o_ab:o_qb]
    ab_ref[0] = jnp.concatenate([ab, jnp.zeros((x.shape[0], LANE - ab.shape[1]), F32)], 1).astype(BF16)


def _winprep_call(w_in, bounds):
    depth, d, n = w_in.shape
    n_main = n - (bounds[1] - bounds[0])
    rt = min(d, WPREP_ROWS)
    return pl.pallas_call(
        functools.partial(_winprep_kernel, bounds=bounds),
        grid=(depth, d // rt),
        in_specs=[pl.BlockSpec((1, rt, n), lambda l, i: (l, i, 0))],
        out_specs=[pl.BlockSpec((1, rt, n_main), lambda l, i: (l, i, 0)),
                   pl.BlockSpec((1, rt, LANE), lambda l, i: (l, i, 0))],
        out_shape=[jax.ShapeDtypeStruct((depth, d, n_main), BF16), jax.ShapeDtypeStruct((depth, d, LANE), BF16)],
        compiler_params=_params(("parallel", "parallel"), 48),
        name="w_in_prep",
    )(w_in)


def _cast_chunks(w, layer, n):
    depth, rows = w.shape[0], w.shape[1] * w.shape[2]
    if rows % n or (rows // n) % BF16_ROWS:
        return None
    return w.reshape(depth * n, rows // n, w.shape[3]), layer * n, n


def _cast_specs(cast, n_inner):
    chunks, first, n = cast
    blk = (1,) + chunks.shape[1:]
    src = pl.BlockSpec(blk, lambda b, j, *_: (first + b * n_inner + j, 0, 0))
    dst = pl.BlockSpec(blk, lambda b, j, *_: (b * n_inner + j, 0, 0))
    return src, dst, jax.ShapeDtypeStruct((n,) + chunks.shape[1:], BF16)


def _add_casts(casts, n_inner, in_specs, out_specs, out_shape):
    extra = ()
    for cast in casts or ():
        src, dst, shape = _cast_specs(cast, n_inner)
        in_specs.append(src)
        out_specs.append(dst)
        out_shape.append(shape)
        extra += (cast[0],)
    return extra


def _do_casts(rest):
    n_cast = len(rest) // 2
    for cast_src, cast_dst in zip(rest[:n_cast], rest[n_cast + 1:]):
        cast_dst[...] = cast_src[...].astype(BF16)
    return rest[n_cast]


def _inproj_kernel(x_ref, shift_ref, scale_ref, g_ref, w_ref, wab_ref, o_ref, ab_ref, h_ref):
    @pl.when(pl.program_id(2) == 0)
    def _():
        x = x_ref[0]
        y = x * lax.rsqrt(jnp.mean(x * x, -1, keepdims=True) + EPS) * g_ref[...]
        h = (y * (1.0 + scale_ref[0]) + shift_ref[0]).astype(BF16)
        h_ref[...] = h
        ab_ref[0] = jnp.dot(h, wab_ref[0], preferred_element_type=F32)

    o_ref[0] = jnp.dot(h_ref[...], w_ref[0], preferred_element_type=F32).astype(o_ref.dtype)


def _inproj_call(x, shift, scale, gain, w_main, w_ab, layer):
    bn, seq, d = x.shape
    n = w_main.shape[2]
    tm = min(seq, INPROJ_TM)
    tn = next((t for t in INPROJ_TN if n % t == 0), n)
    per_batch = shift.shape[0] > 1
    mod_map = (lambda b, i, j: (b, 0, 0)) if per_batch else (lambda b, i, j: (0, 0, 0))
    return pl.pallas_call(
        _inproj_kernel,
        grid=(bn, seq // tm, n // tn),
        in_specs=[pl.BlockSpec((1, tm, d), lambda b, i, j: (b, i, 0)),
                  pl.BlockSpec((1, 1, d), mod_map),
                  pl.BlockSpec((1, 1, d), mod_map),
                  pl.BlockSpec((1, d), lambda b, i, j: (0, 0)),
                  pl.BlockSpec((1, d, tn), lambda b, i, j: (layer, 0, j)),
                  pl.BlockSpec((1, d, LANE), lambda b, i, j: (layer, 0, 0))],
        out_specs=[pl.BlockSpec((1, tm, tn), lambda b, i, j: (b, i, j)),
                   pl.BlockSpec((1, tm, LANE), lambda b, i, j: (b, i, 0))],
        out_shape=[jax.ShapeDtypeStruct((bn, seq, n), BF16),
                   jax.ShapeDtypeStruct((bn, seq, LANE), F32)],
        scratch_shapes=[pltpu.VMEM((tm, d), BF16)],
        compiler_params=_params(("parallel", "parallel", "arbitrary"), 56),
        name="in_proj",
    )(x, shift, scale, gain, w_main, w_ab)


def _outproj_kernel(oa_ref, ob_ref, oc_ref, x_ref, gate_ref, shift_ref, scale_ref, g_ref, w_ref, wr_ref, br_ref,
                    xn_ref, f_ref, lg_ref):
    acc, r0 = None, 0
    for o_ref in (oa_ref, ob_ref, oc_ref):
        r1 = r0 + o_ref.shape[2]
        part = jnp.dot(o_ref[0], w_ref[r0:r1, :], preferred_element_type=F32)
        acc = part if acc is None else acc + part
        r0 = r1
    xn = x_ref[0] + gate_ref[0] * acc
    xn_ref[0] = xn
    y = xn * lax.rsqrt(jnp.mean(xn * xn, -1, keepdims=True) + EPS) * g_ref[...]
    f = y * (1.0 + scale_ref[0]) + shift_ref[0]
    f_hi = f.astype(BF16)
    f_ref[0] = f_hi
    f_lo = (f - f_hi.astype(F32)).astype(BF16)
    hh_hl = jnp.dot(f_hi, wr_ref[...], preferred_element_type=F32)
    lh = jnp.dot(f_lo, wr_ref[:, :ROUTER_COLS], preferred_element_type=F32)
    lg_ref[0] = hh_hl[:, :ROUTER_COLS] + hh_hl[:, ROUTER_COLS:] + lh + br_ref[...]


def _outproj_call(os, x, gate, shift, scale, gain, w_out, w_r, b_r):
    bn, seq, d = x.shape
    tm = min(seq, ROW_TM)
    per_batch = gate.shape[0] > 1
    mod_map = (lambda b, i: (b, 0, 0)) if per_batch else (lambda b, i: (0, 0, 0))
    row = pl.BlockSpec((1, tm, d), lambda b, i: (b, i, 0))
    o_specs = [pl.BlockSpec((1, tm, o.shape[2]), lambda b, i: (b, i, 0)) for o in os]
    return pl.pallas_call(
        _outproj_kernel,
        grid=(bn, seq // tm),
        in_specs=[*o_specs, row,
                  pl.BlockSpec((1, 1, d), mod_map),
                  pl.BlockSpec((1, 1, d), mod_map),
                  pl.BlockSpec((1, 1, d), mod_map),
                  pl.BlockSpec((1, d), lambda b, i: (0, 0)),
                  pl.BlockSpec((d, d), lambda b, i: (0, 0)),
                  pl.BlockSpec((d, 2 * ROUTER_COLS), lambda b, i: (0, 0)),
                  pl.BlockSpec((1, ROUTER_COLS), lambda b, i: (0, 0))],
        out_specs=[row, row, pl.BlockSpec((1, tm, ROUTER_COLS), lambda b, i: (b, i, 0))],
        out_shape=[jax.ShapeDtypeStruct((bn, seq, d), F32),
                   jax.ShapeDtypeStruct((bn, seq, d), BF16),
                   jax.ShapeDtypeStruct((bn, seq, ROUTER_COLS), F32)],
        compiler_params=_params(("parallel", "parallel"), 56),
        name="out_proj",
    )(*os, x, gate, shift, scale, gain, w_out, w_r, b_r)


def _moe_kernel(te_ref, nu_ref, x_ref, wg_ref, wu_ref, wd_ref, o_ref):
    used = pl.program_id(0) < nu_ref[0]

    @pl.when(used)
    def _():
        x = x_ref[...]
        g = jnp.dot(x, wg_ref[0], preferred_element_type=F32)
        u = jnp.dot(x, wu_ref[0], preferred_element_type=F32)
        a = (g * jax.nn.sigmoid(g) * u).astype(BF16)
        o_ref[...] = jnp.dot(a, wd_ref[0], preferred_element_type=F32).astype(o_ref.dtype)

    @pl.when(jnp.logical_not(used))
    def _():
        o_ref[...] = jnp.zeros_like(o_ref)


def _moe_call(tile_expert, n_used, buf, w_gate, w_up, w_down, tm):
    p, d = buf.shape
    f = w_gate.shape[-1]
    grid_spec = pltpu.PrefetchScalarGridSpec(
        num_scalar_prefetch=2,
        grid=(p // tm,),
        in_specs=[pl.BlockSpec((tm, d), lambda i, te, nu: (i, 0)),
                  pl.BlockSpec((1, d, f), lambda i, te, nu: (te[i], 0, 0)),
                  pl.BlockSpec((1, d, f), lambda i, te, nu: (te[i], 0, 0)),
                  pl.BlockSpec((1, f, d), lambda i, te, nu: (te[i], 0, 0))],
        out_specs=pl.BlockSpec((tm, d), lambda i, te, nu: (i, 0)),
    )
    return pl.pallas_call(
        _moe_kernel,
        grid_spec=grid_spec,
        out_shape=jax.ShapeDtypeStruct((p, d), BF16),
        compiler_params=_params(("arbitrary",), 60),
        name="moe_experts",
    )(tile_expert, n_used, buf, w_gate, w_up, w_down)


def _route_kernel(lg_ref, ids_ref, wts_ref, cnt_ref, carry_ref):
    @pl.when(pl.program_id(0) == 0)
    def _():
        carry_ref[...] = jnp.zeros_like(carry_ref)

    lg = lg_ref[...]
    tm = lg.shape[0]
    lane = lax.broadcasted_iota(jnp.int32, lg.shape, 1)
    big = jnp.int32(ROUTER_COLS)
    is_g = lane < N_GROUPS
    gmax = jnp.max(jnp.where(is_g, lg, -jnp.inf), -1, keepdims=True)
    g_sel = jnp.min(jnp.where(is_g & (lg == gmax), lane, big), -1, keepdims=True)
    p_g = 1.0 / jnp.sum(jnp.where(is_g, jnp.exp(lg - gmax), 0.0), -1, keepdims=True)
    lo = N_GROUPS + E_PER_GROUP * g_sel
    in_e = (lane >= lo) & (lane < lo + E_PER_GROUP)
    el = jnp.where(in_e, lg, -jnp.inf)
    v1 = jnp.max(el, -1, keepdims=True)
    i1 = jnp.min(jnp.where(el == v1, lane, big), -1, keepdims=True)
    el2 = jnp.where(lane == i1, -jnp.inf, el)
    v2 = jnp.max(el2, -1, keepdims=True)
    i2 = jnp.min(jnp.where(el2 == v2, lane, big), -1, keepdims=True)
    t = jnp.exp(v2 - v1)
    w1 = p_g / (1.0 + t)
    w2 = w1 * t
    e1 = i1 - N_GROUPS
    e2 = i2 - N_GROUPS
    hit1 = lane == e1
    hit2 = lane == e2
    onehot = jnp.where(hit1 | hit2, 1.0, 0.0)
    ri = lax.broadcasted_iota(jnp.int32, (tm, tm), 0)
    ci = lax.broadcasted_iota(jnp.int32, (tm, tm), 1)
    below = jnp.where(ci < ri, 1.0, 0.0).astype(BF16)
    before = carry_ref[...] + jnp.dot(below, onehot.astype(BF16), preferred_element_type=F32)
    r1 = jnp.sum(jnp.where(hit1, before, 0.0), -1, keepdims=True).astype(jnp.int32)
    r2 = jnp.sum(jnp.where(hit2, before, 0.0), -1, keepdims=True).astype(jnp.int32)
    ids_ref[...] = jnp.where(lane == 0, e1, jnp.where(lane == 1, e2, jnp.where(lane == 2, r1, jnp.where(lane == 3, r2, 0))))
    wts_ref[...] = jnp.where(lane == 0, w1, jnp.where(lane == 1, w2, 0.0))
    carry_ref[...] = carry_ref[...] + jnp.sum(onehot, 0, keepdims=True)
    cnt_ref[...] = carry_ref[...]


def _route_call(logits):
    t = logits.shape[0]
    tm = min(t, ROUTE_TM)
    row = pl.BlockSpec((tm, ROUTER_COLS), lambda i: (i, 0))
    return pl.pallas_call(
        _route_kernel,
        grid=(t // tm,),
        in_specs=[row],
        out_specs=[row, row, pl.BlockSpec((1, ROUTER_COLS), lambda i: (0, 0))],
        out_shape=[jax.ShapeDtypeStruct((t, ROUTER_COLS), jnp.int32),
                   jax.ShapeDtypeStruct((t, ROUTER_COLS), F32),
                   jax.ShapeDtypeStruct((1, ROUTER_COLS), F32)],
        scratch_shapes=[pltpu.VMEM((1, ROUTER_COLS), F32)],
        compiler_params=_params(("arbitrary",), 32),
        name="route",
    )(logits)


def _resid_kernel(x_ref, y0_ref, y1_ref, w_ref, gate_ref, g_ref, o_ref, *, final_norm):
    w = w_ref[0]
    y = y0_ref[0].astype(F32) * w[:, 0:1] + y1_ref[0].astype(F32) * w[:, 1:2]
    xn = x_ref[0] + gate_ref[0] * y
    if final_norm:
        xn = xn * lax.rsqrt(jnp.mean(xn * xn, -1, keepdims=True) + EPS) * g_ref[...]
    o_ref[0] = xn


def _resid_call(x, y0, y1, w, gate, gain, final_norm):
    bn, seq, d = x.shape
    tm = min(seq, ROW_TM)
    per_batch = gate.shape[0] > 1
    mod_map = (lambda b, i: (b, 0, 0)) if per_batch else (lambda b, i: (0, 0, 0))
    row = pl.BlockSpec((1, tm, d), lambda b, i: (b, i, 0))
    return pl.pallas_call(
        functools.partial(_resid_kernel, final_norm=final_norm),
        grid=(bn, seq // tm),
        in_specs=[row, row, row, pl.BlockSpec((1, tm, ROUTER_COLS), lambda b, i: (b, i, 0)),
                  pl.BlockSpec((1, 1, d), mod_map), pl.BlockSpec((1, d), lambda b, i: (0, 0))],
        out_specs=row,
        out_shape=jax.ShapeDtypeStruct((bn, seq, d), F32),
        compiler_params=_params(("parallel", "parallel"), 48),
        name="moe_residual",
    )(x, y0, y1, w, gate, gain)


def _softplus(x):
    return jnp.maximum(x, 0.0) + jnp.log1p(jnp.exp(-jnp.abs(x)))


def _gdn_prep(x_ref, w_ref, lanes, dst_ref, off, mode, xp_ref):
    n = x_ref.shape[1]
    pad = CONV_W // 2
    edge = jnp.zeros((SUBLANE, DK_A), F32)
    xp_ref[0:SUBLANE, :] = edge
    xp_ref[n + SUBLANE:n + 2 * SUBLANE, :] = edge
    xp_ref[SUBLANE:n + SUBLANE, :] = x_ref[0, :, lanes].astype(F32)
    acc = None
    for o in range(-pad, pad + 1):
        term = xp_ref[SUBLANE + o:SUBLANE + o + n, :] * w_ref[o + pad:o + pad + 1, lanes]
        acc = term if acc is None else acc + term
    y = acc * jax.nn.sigmoid(acc)
    if mode != "v":
        y = y * lax.rsqrt(jnp.sum(y * y, -1, keepdims=True) + EPS)
    if mode == "q":
        y = y * DK_A ** -0.5
    dst_ref[off:off + n, :] = y


def _gdn_kernel(alog_ref, dtb_ref,
                qc_ref, kc_ref, vc_ref, zc_ref, ql_ref, kl_ref, vl_ref, zl_ref,
                wq_ref, wk_ref, wv_ref, grow_ref, bcol_ref, nw_ref,
                oc_ref, ol_ref,
                q_s, k_s, v_s, o_s, u_s, l1_s, l2_s, gl_s, gc_s, s_s, xp_s, *, h_a, n_ctx):
    C = CHUNK
    hp = o_s.shape[0]
    lc = qc_ref.shape[1]
    n_tot = u_s.shape[2]
    n_pairs = n_tot // 2
    group = max(g for g in GDN_GROUPS if n_pairs % g == 0)

    NB = 4
    W = NB * C
    ii = lax.broadcasted_iota(jnp.int32, (C, W), 0)
    jl = lax.broadcasted_iota(jnp.int32, (C, W), 1)
    blk = jl // C
    jm = jl - blk * C
    eye4 = ii == jm
    tri4 = ((blk < 2) & (ii >= jm)) | ((blk >= 2) & (ii <= jm))
    eye4f = eye4.astype(F32)
    lane2 = lax.broadcasted_iota(jnp.int32, (C, 2 * C), 1)
    lo2 = lane2 < C
    i2 = lax.broadcasted_iota(jnp.int32, (2 * C, 2 * C), 0)
    j2 = lax.broadcasted_iota(jnp.int32, (2 * C, 2 * C), 1)
    same = (i2 // C) == (j2 // C)
    lvl_same = [(ii // (INV_BLK << k)) == (jm // (INV_BLK << k)) for k in range((C // INV_BLK).bit_length())]

    def block_diag(x):
        return jnp.concatenate([jnp.where(blk == b, x, 0.0) for b in range(NB)], 0).astype(BF16)

    def head_terms(hh):
        head = pl.program_id(1) * hp + hh
        lanes = slice(hh * DK_A, (hh + 1) * DK_A)
        _gdn_prep(qc_ref, wq_ref, lanes, q_s, 0, "q", xp_s)
        _gdn_prep(kc_ref, wk_ref, lanes, k_s, 0, "k", xp_s)
        _gdn_prep(vc_ref, wv_ref, lanes, v_s, 0, "v", xp_s)
        _gdn_prep(ql_ref, wq_ref, lanes, q_s, lc, "q", xp_s)
        _gdn_prep(kl_ref, wk_ref, lanes, k_s, lc, "k", xp_s)
        _gdn_prep(vl_ref, wv_ref, lanes, v_s, lc, "v", xp_s)

        for d in range(2):
            a = jnp.exp(jnp.full((1, 2 * C), alog_ref[d * h_a + head], F32))
            g = -a * _softplus(grow_ref[0, hh, d] + dtb_ref[d * h_a + head])
            tri = same & ((i2 <= j2) if d == 0 else (i2 >= j2))
            gc_s[d] = jnp.dot(g, tri.astype(F32), preferred_element_type=F32, precision=HIGHEST)

        def pair_setup(p):
            r0 = pl.multiple_of(p * 2 * C, 2 * C)
            q2 = q_s[pl.ds(r0, 2 * C), :]
            k2 = k_s[pl.ds(r0, 2 * C), :]
            v2 = v_s[pl.ds(r0, 2 * C), :]
            k16 = k2.astype(BF16)
            gk = lax.dot_general(k16, k16, _NT, preferred_element_type=F32)
            gq = lax.dot_general(q2.astype(BF16), k16, _NT, preferred_element_type=F32)
            kk2 = jnp.where(lo2, gk[:C], gk[C:])
            qk2 = jnp.where(lo2, gq[:C], gq[C:])
            kk4 = jnp.concatenate([kk2, kk2], 1)
            qk4 = jnp.concatenate([qk2, qk2], 1)
            grow = [gc_s[d, pl.ds(p, 1), :] for d in range(2)]
            rowb = jnp.concatenate([jnp.broadcast_to(g, (C, 2 * C)) for g in grow], 1)
            diag = jnp.where(eye4, rowb, 0.0)
            gcol, bcol, tot = [], [], []
            for b in range(NB):
                d, par = b // 2, b % 2
                half = diag[:, d * 2 * C:(d + 1) * 2 * C]
                gcol.append(jnp.sum(jnp.where(lo2 if par == 0 else ~lo2, half, 0.0), axis=1, keepdims=True))
                bcol.append(jax.nn.sigmoid(bcol_ref[0, hh, pl.ds(r0 + par * C, C), d:d + 1]))
                e = par * C + (C - 1 if d == 0 else 0)
                tot.append(grow[d][:, e:e + 1])
            gb = [jnp.broadcast_to(g, (C, 2 * C)) for g in gcol]
            bb = [jnp.broadcast_to(b_, (C, 2 * C)) for b_ in bcol]
            gcc4 = jnp.concatenate([jnp.where(lo2, gb[0], gb[1]), jnp.where(lo2, gb[2], gb[3])], 1)
            bcc4 = jnp.concatenate([jnp.where(lo2, bb[0], bb[1]), jnp.where(lo2, bb[2], bb[3])], 1)
            dec4 = jnp.where(tri4, jnp.exp(jnp.where(tri4, gcc4 - rowb, 0.0)), 0.0)
            m0 = jnp.where(eye4, 0.0, -(kk4 * bcc4 * dec4))
            return m0, (p, q2, k2, v2, qk4 * dec4, gb, bb, tot)

        def pair_finish(tinv4, aux):
            p, q2, k2, v2, attn4, gb, bb, tot = aux
            eb = [jnp.exp(g) for g in gb]
            ks = (k2[:C], k2[C:])
            vs = (v2[:C], v2[C:])
            qs = (q2[:C], q2[C:])
            rhs = jnp.concatenate([jnp.concatenate([vs[b % 2] * bb[b], ks[b % 2] * (bb[b] * eb[b])], 1)
                                   for b in range(NB)], 0).astype(BF16)
            uw = jnp.dot(block_diag(tinv4), rhs, preferred_element_type=F32)
            for b in range(NB):
                d, par = b // 2, b % 2
                c = 2 * p + par
                u_s[hh, d, c] = uw[b * C:(b + 1) * C, :DK_A]
                l1_s[hh, d, c] = jnp.concatenate([uw[b * C:(b + 1) * C, DK_A:], qs[par] * eb[b]], 0).astype(BF16)
                gl_s[hh, d, c] = jnp.broadcast_to(jnp.exp(tot[b]), (1, DK_A))
            for d in range(2):
                kdec = jnp.concatenate([ks[par] * jnp.exp(tot[2 * d + par] - gb[2 * d + par]) for par in range(2)], 0)
                l2_s[hh, d, p] = jnp.concatenate([attn4[:, d * 2 * C:(d + 1) * 2 * C], kdec.T], 0).astype(BF16)

        def group_terms(i, _):
            st = [pair_setup(i * group + g) for g in range(group)]
            m0s = [s[0] for s in st]
            mds = [jnp.where(lvl_same[0], m0, 0.0) for m0 in m0s]
            rs = [jnp.dot(md.astype(BF16), block_diag(md), preferred_element_type=F32) for md in mds]
            xs = [eye4f + md for md in mds]
            for _ in range(INV_DOUBLINGS - 2):
                qrs = [jnp.dot(jnp.concatenate([x, r], 0).astype(BF16), block_diag(r), preferred_element_type=F32)
                       for x, r in zip(xs, rs)]
                xs = [x + qr[:C] for x, qr in zip(xs, qrs)]
                rs = [qr[C:] for qr in qrs]
            xs = [x + jnp.dot(x.astype(BF16), block_diag(r), preferred_element_type=F32) for x, r in zip(xs, rs)]
            for lv in range(1, len(lvl_same)):
                offs = [jnp.where(lvl_same[lv] & ~lvl_same[lv - 1], m0, 0.0) for m0 in m0s]
                ys = [jnp.dot(x.astype(BF16), block_diag(off), preferred_element_type=F32) for x, off in zip(xs, offs)]
                xs = [x + jnp.dot(y.astype(BF16), block_diag(x), preferred_element_type=F32) for x, y in zip(xs, ys)]
            for tinv4, s in zip(xs, st):
                pair_finish(tinv4, s[1])
            return 0

        lax.fori_loop(0, n_pairs // group, group_terms, 0)

    for hh in range(hp):
        head_terms(hh)

    o_s[...] = jnp.zeros_like(o_s)
    s_s[...] = jnp.zeros_like(s_s)
    row_half = lax.broadcasted_iota(jnp.int32, (2 * C, DK_A), 0) // C

    def scan_step(i, _):
        c_b = jnp.where(i < n_ctx, n_ctx - 1 - i, n_ctx + n_tot - 1 - i)
        chains = [(hh, d, i if d == 0 else c_b) for hh in range(hp) for d in range(2)]
        r1s = [jnp.dot(l1_s[hh, d, c], s_s[hh, d].astype(BF16), preferred_element_type=F32)
               for hh, d, c in chains]
        v_exts = []
        for (hh, d, c), r1 in zip(chains, r1s):
            v_new = u_s[hh, d, c] - r1[:C]
            v_exts.append(jnp.where(row_half == c % 2, jnp.concatenate([v_new, v_new], 0), 0.0).astype(BF16))
        r2s = [jnp.dot(l2_s[hh, d, c // 2], v_ext, preferred_element_type=F32)
               for (hh, d, c), v_ext in zip(chains, v_exts)]
        for (hh, d, c), r1, r2 in zip(chains, r1s, r2s):
            rows = pl.ds(pl.multiple_of(c * C, C), C)
            o_s[hh, rows, :] = o_s[hh, rows, :] + r1[C:] + r2[:C]
            s_s[hh, d] = s_s[hh, d] * gl_s[hh, d, c] + r2[C:]
        return 0

    lax.fori_loop(0, n_tot, scan_step, 0)

    for hh in range(hp):
        lanes = slice(hh * DK_A, (hh + 1) * DK_A)
        for z_ref, o_ref, off in ((zc_ref, oc_ref, 0), (zl_ref, ol_ref, lc)):
            n = z_ref.shape[1]
            o = o_s[hh, off:off + n, :]
            o = o * lax.rsqrt(jnp.mean(o * o, -1, keepdims=True) + EPS) * nw_ref[...]
            z = z_ref[0, :, lanes].astype(F32)
            o_ref[0, :, lanes] = (o * (z * jax.nn.sigmoid(z))).astype(o_ref.dtype)


def _gdn_call(p_c, p_l, ab_c, ab_l, conv_w, a_log, dt_bias, norm_w, h_a):
    B, L, _ = p_l.shape
    lc = p_c.shape[1]
    t = lc + L
    n_tot, n_ctx = t // CHUNK, lc // CHUNK
    assert n_tot % 2 == 0
    hp = GDN_HEADS if h_a % GDN_HEADS == 0 else 1
    n_hg = h_a // hp
    wd = hp * DK_A
    ab = jnp.concatenate([ab_c, ab_l], 1)[..., :4 * h_a].reshape(B, t, 2, 2, h_a)
    g_row = ab[:, :, :, 0].transpose(0, 3, 2, 1).reshape(B, h_a, 2, n_tot // 2, 2 * CHUNK)
    b_col = ab[:, :, :, 1].transpose(0, 3, 1, 2)

    def col(k, n):
        return pl.BlockSpec((1, n, wd), lambda b, j, *_: (b, 0, k * n_hg + j))

    def tap(k):
        return pl.BlockSpec((CONV_W, wd), lambda b, j, *_: (0, k * n_hg + j))

    grid_spec = pltpu.PrefetchScalarGridSpec(
        num_scalar_prefetch=2,
        grid=(B, n_hg),
        in_specs=[col(0, lc), col(1, lc), col(2, lc), col(3, lc),
                  col(0, L), col(1, L), col(2, L), col(3, L),
                  tap(0), tap(1), tap(2),
                  pl.BlockSpec((1, hp, 2, n_tot // 2, 2 * CHUNK), lambda b, j, *_: (b, j, 0, 0, 0)),
                  pl.BlockSpec((1, hp, t, 2), lambda b, j, *_: (b, j, 0, 0)),
                  pl.BlockSpec((1, DK_A), lambda b, j, *_: (0, 0))],
        out_specs=[pl.BlockSpec((1, lc, wd), lambda b, j, *_: (b, 0, j)),
                   pl.BlockSpec((1, L, wd), lambda b, j, *_: (b, 0, j))],
        scratch_shapes=[pltpu.VMEM((t, DK_A), F32), pltpu.VMEM((t, DK_A), F32), pltpu.VMEM((t, DK_A), F32),
                        pltpu.VMEM((hp, t, DK_A), F32),
                        pltpu.VMEM((hp, 2, n_tot, CHUNK, DK_A), F32),
                        pltpu.VMEM((hp, 2, n_tot, 2 * CHUNK, DK_A), BF16),
                        pltpu.VMEM((hp, 2, n_tot // 2, CHUNK + DK_A, 2 * CHUNK), BF16),
                        pltpu.VMEM((hp, 2, n_tot, 1, DK_A), F32),
                        pltpu.VMEM((2, n_tot // 2, 2 * CHUNK), F32),
                        pltpu.VMEM((hp, 2, DK_A, DK_A), F32),
                        pltpu.VMEM((max(L, lc) + 2 * SUBLANE, DK_A), F32)],
    )
    return pl.pallas_call(
        functools.partial(_gdn_kernel, h_a=h_a, n_ctx=n_ctx),
        grid_spec=grid_spec,
        out_shape=[jax.ShapeDtypeStruct((B, lc, h_a * DK_A), BF16), jax.ShapeDtypeStruct((B, L, h_a * DK_A), BF16)],
        compiler_params=_params(("parallel", "parallel"), 56),
        name="gdn",
    )(a_log.reshape(-1), dt_bias.reshape(-1), p_c, p_c, p_c, p_c, p_l, p_l, p_l, p_l,
      conv_w, conv_w, conv_w, g_row, b_col, norm_w.reshape(1, DK_A))


def _softmax_rows(s, sink=None):
    m = jnp.max(s, -1, keepdims=True)
    if sink is not None:
        m = jnp.maximum(m, sink)
    e = jnp.exp(s - m)
    den = jnp.sum(e, -1, keepdims=True)
    if sink is not None:
        den = den + jnp.exp(sink - m)
    return (e * (1.0 / den)).astype(BF16)


def _gqa_kernel(sink_ref, q_ref, kvl_ref, kvc_ref, cos_ref, sa_ref, sb_ref, *rest, n_heads, latent):
    o_ref = _do_casts(rest)
    G = n_heads // KV_B
    P = 2 * DH
    n = pl.program_id(1)
    seq = kvl_ref.shape[1]
    lc = kvc_ref.shape[1]

    def rope(x, r0):
        reps = x.shape[1] // P
        rows = pl.ds(pl.multiple_of(r0, QBLK), QBLK)
        c, a, b = [jnp.concatenate([t[rows, :]] * reps, 1) if reps > 1 else t[rows, :]
                   for t in (cos_ref, sa_ref, sb_ref)]
        w = x.shape[1]
        return x * c + pltpu.roll(x, w - DH // 4, 1) * a + pltpu.roll(x, DH // 4, 1) * b

    q = q_ref[0].astype(F32) * DH ** -0.5
    kc = kvc_ref[0][:, :P].astype(F32)
    vc = kvc_ref[0][:, P:].astype(F32)
    if latent:
        q = rope(q, n * QBLK)
        ks, vs = [], []
        for o in (-1, 0, 1):
            s0 = jnp.clip((n + o) * QBLK, 0, seq - QBLK)
            kv = kvl_ref[0, pl.ds(pl.multiple_of(s0, QBLK), QBLK), :].astype(F32)
            ks.append(rope(kv[:, :P], s0))
            vs.append(kv[:, P:])
        k = jnp.concatenate(ks + [kc], 0)
        v = jnp.concatenate(vs + [vc], 0)
        nk = 3 * QBLK + lc
        qi = lax.broadcasted_iota(jnp.int32, (QBLK, nk), 0)
        kj = lax.broadcasted_iota(jnp.int32, (QBLK, nk), 1)
        kpos = n * QBLK - QBLK + kj
        rel = kj - QBLK - qi
        valid = ((rel >= -WIN) & (rel <= WIN) & (kpos >= 0) & (kpos < seq)) | (kj >= 3 * QBLK)
    else:
        k, v = kc, vc
        valid = None
    q = q.astype(BF16)
    k_sw = pltpu.roll(k, DH, 1).astype(BF16)
    v_sw = pltpu.roll(v, DH, 1).astype(BF16)
    k = k.astype(BF16)
    v = v.astype(BF16)
    lane = lax.broadcasted_iota(jnp.int32, (QBLK, P), 1)
    zero = jnp.zeros((QBLK, P), BF16)
    scores = []
    for h in range(n_heads):
        qp = q[:, (h // 2) * P:(h // 2 + 1) * P]
        qh = jnp.where(lane < DH, qp, zero) if h % 2 == 0 else jnp.where(lane >= DH, qp, zero)
        straight = (h // G == h % 2)
        scores.append(lax.dot_general(qh, k if straight else k_sw, _NT, preferred_element_type=F32))
    probs = []
    for h, s in enumerate(scores):
        if valid is not None:
            s = jnp.where(valid, s, NEG_INF)
        probs.append(_softmax_rows(s, sink_ref[h]))
    outs = [jnp.dot(pr, v if (h // G == h % 2) else v_sw, preferred_element_type=F32) for h, pr in enumerate(probs)]
    for p in range(n_heads // 2):
        o_ref[0, :, p * P:(p + 1) * P] = jnp.where(lane < DH, outs[2 * p], outs[2 * p + 1]).astype(o_ref.dtype)


def _gqa_call(p_q, p_l, p_c, sink, tables, d_model, d_b, latent, casts=None):
    B, nq, _ = p_q.shape
    L, lc = p_l.shape[1], p_c.shape[1]
    kvw = 2 * KV_B * DH
    n_heads = d_b // DH
    assert (n_heads // KV_B) % 2 == 0
    in_specs = [pl.BlockSpec((1, QBLK, d_b), lambda b, n, *_: (b, n, 2 * d_model // d_b)),
                pl.BlockSpec((1, L, kvw), lambda b, n, *_: (b, 0, 3 * d_model // kvw)),
                pl.BlockSpec((1, lc, kvw), lambda b, n, *_: (b, 0, 3 * d_model // kvw)),
                pl.BlockSpec((L, 2 * DH), lambda b, n, *_: (0, 0)),
                pl.BlockSpec((L, 2 * DH), lambda b, n, *_: (0, 0)),
                pl.BlockSpec((L, 2 * DH), lambda b, n, *_: (0, 0))]
    out_specs = [pl.BlockSpec((1, QBLK, d_b), lambda b, n, *_: (b, n, 0))]
    out_shape = [jax.ShapeDtypeStruct((B, nq, d_b), BF16)]
    extra = _add_casts(casts, nq // QBLK, in_specs, out_specs, out_shape)
    grid_spec = pltpu.PrefetchScalarGridSpec(
        num_scalar_prefetch=1, grid=(B, nq // QBLK), in_specs=in_specs, out_specs=out_specs)
    return pl.pallas_call(
        functools.partial(_gqa_kernel, n_heads=n_heads, latent=latent),
        grid_spec=grid_spec,
        out_shape=out_shape,
        compiler_params=_params(("parallel", "parallel"), 48),
        name="window_gqa" if latent else "ctx_gqa",
    )(sink, p_q, p_l, p_c, *tables, *extra)


def _rope_tables(n_tokens):
    t = jnp.arange(n_tokens)
    row = (t // GRID_W).astype(F32)
    col = (t % GRID_W).astype(F32)
    half = DH // 2
    inv = ROPE_BASE ** (-jnp.arange(0, half, 2, dtype=F32) / half)
    ang_r = row[:, None] * inv[None]
    ang_c = col[:, None] * inv[None]
    ang = jnp.concatenate([ang_r, ang_r, ang_c, ang_c], -1)
    cos, sin = jnp.cos(ang), jnp.sin(ang)
    first = (jnp.arange(DH) % half) < half // 2
    sa = jnp.where(first, -sin, 0.0)
    sb = jnp.where(first, 0.0, sin)
    return tuple(jnp.concatenate([u, u], -1) for u in (cos, sa, sb))


def _na_kernel(q_ref, k_ref, v_ref, kc_ref, vc_ref, tbl_ref, *rest, n_heads, rows, latent):
    o_ref = _do_casts(rest)
    P = 2 * DH
    r = pl.program_id(1)
    nq = q_ref.shape[1]
    nb = KH_MAX * GRID_W
    q = (q_ref[0].astype(F32) * DH ** -0.5).astype(BF16)
    lane = lax.broadcasted_iota(jnp.int32, (nq, P), 1)
    zero = jnp.zeros((nq, P), BF16)
    if latent:
        r0 = jnp.clip(r - KH_MAX // 2, 0, rows - KH_MAX)
        band = pl.ds(pl.multiple_of(r0 * GRID_W, GRID_W), nb)
        d0 = r0 - r + (KH_MAX - 1)
    scores, vals = [], []
    for p in range(n_heads // 2):
        cols = slice(p * P, (p + 1) * P)
        qp = q[:, cols]
        kc = kc_ref[0, :, cols]
        vc = vc_ref[0, :, cols]
        if latent:
            kk = jnp.concatenate([k_ref[0, band, cols], kc], 0)
            vals.append(jnp.concatenate([v_ref[0, band, cols], vc], 0))
        else:
            kk = kc
            vals.append(vc)
        for half in range(2):
            qh = jnp.where(lane < DH, qp, zero) if half == 0 else jnp.where(lane >= DH, qp, zero)
            scores.append(lax.dot_general(qh, kk, _NT, preferred_element_type=F32))
    probs = []
    for h, s in enumerate(scores):
        if latent:
            bias = jnp.concatenate([tbl_ref[h, d0 + 2 * t] for t in range(KH_MAX // 2)], 1)
            s = jnp.concatenate([s[:, :nb] + bias, s[:, nb:]], 1)
        probs.append(_softmax_rows(s))
    outs = [jnp.dot(pr, vals[h // 2], preferred_element_type=F32) for h, pr in enumerate(probs)]
    for p in range(n_heads // 2):
        o_ref[0, :, p * P:(p + 1) * P] = jnp.where(lane < DH, outs[2 * p], outs[2 * p + 1]).astype(o_ref.dtype)


def _na_bias_table(rpb):
    qc = jnp.arange(GRID_W)[:, None]
    kc = jnp.arange(GRID_W)[None, :]
    dc = jnp.clip(kc - qc + (KW - 1), 0, 2 * KW - 2)
    q_cs = jnp.clip(qc - KW // 2, 0, GRID_W - KW)
    in_win = (kc >= q_cs) & (kc < q_cs + KW)
    t = jnp.where(in_win[None, None], rpb[:, :, dc], NEG_INF)
    return jnp.concatenate([t[:, :-1], t[:, 1:]], -1)


def _na_call(p_q, p_l, p_c, tbl, d_model, d_c, latent, casts=None):
    B, nq, _ = p_q.shape
    L, lc = p_l.shape[1], p_c.shape[1]
    n_heads = d_c // DH
    rows = L // GRID_W
    assert rows >= KH_MAX and n_heads % 2 == 0
    cb = 2 * d_model // d_c
    qb = GRID_W
    in_specs = [pl.BlockSpec((1, qb, d_c), lambda b, r: (b, r, cb + 1)),
                pl.BlockSpec((1, L, d_c), lambda b, r: (b, 0, cb + 2)),
                pl.BlockSpec((1, L, d_c), lambda b, r: (b, 0, cb + 3)),
                pl.BlockSpec((1, lc, d_c), lambda b, r: (b, 0, cb + 2)),
                pl.BlockSpec((1, lc, d_c), lambda b, r: (b, 0, cb + 3)),
                pl.BlockSpec(tbl.shape, lambda b, r: (0, 0, 0, 0))]
    out_specs = [pl.BlockSpec((1, qb, d_c), lambda b, r: (b, r, 0))]
    out_shape = [jax.ShapeDtypeStruct((B, nq, d_c), BF16)]
    extra = _add_casts(casts, nq // qb, in_specs, out_specs, out_shape)
    return pl.pallas_call(
        functools.partial(_na_kernel, n_heads=n_heads, rows=rows, latent=latent),
        grid=(B, nq // qb),
        in_specs=in_specs,
        out_specs=out_specs,
        out_shape=out_shape,
        compiler_params=_params(("parallel", "parallel"), 48),
        name="nbr_attn" if latent else "ctx_attn",
    )(p_q, p_l, p_l, p_c, p_c, tbl, *extra)


def _moe(f, logits, w_gate, w_up, w_down, split=0):
    T, d = f.shape
    tm = MOE_TM
    ids, wts, cnt = _route_call(logits)
    expert, rank = ids[:, :TOP_K], ids[:, TOP_K:2 * TOP_K]
    counts = cnt[0, :N_EXPERTS].astype(jnp.int32)
    padded = (counts + tm - 1) // tm * tm
    pends = jnp.cumsum(padded)
    pstarts = pends - padded
    start = jnp.sum(jnp.where(expert[..., None] == jnp.arange(N_EXPERTS), pstarts, 0), -1)
    dest = start + rank
    n_tiles = -(-T * TOP_K // tm) + N_EXPERTS
    tile_expert = jnp.minimum(jnp.sum(pends[None, :] <= (jnp.arange(n_tiles) * tm)[:, None], -1),
                              N_EXPERTS - 1).astype(jnp.int32)
    n_used = (pends[-1] // tm).astype(jnp.int32).reshape(1)
    tok = jnp.broadcast_to(jnp.arange(T, dtype=jnp.int32)[:, None], (T, TOP_K))
    src_tok = (jnp.arange(n_tiles * tm, dtype=jnp.int32) % T).at[dest.reshape(-1)].set(tok.reshape(-1))
    out = _moe_call(tile_expert, n_used, f[src_tok], w_gate, w_up, w_down, tm)
    parts = [(dest[:split], wts[:split]), (dest[split:], wts[split:])]
    return [(out[d_[:, 0]], out[d_[:, 1]], w_) for d_, w_ in parts]


def kernel(x, c, ctx, c_ctx, w_mod, b_mod, norm_mix, norm_ffn, w_in, conv_a, a_log, dt_bias, gdn_norm, sink_b, rpb_c, w_out, w_router_group, b_router_group, w_router_expert, b_router_expert, w_gate, w_up, w_down, norm_final):
    B, L, D = x.shape
    Lc = ctx.shape[1]
    depth = w_mod.shape[0]
    d_a = D // 2
    d_b = D // 4
    d_c = D - d_a - d_b
    h_a = d_a // DK_A
    n_ab = 4 * h_a
    kvb = 2 * KV_B * DH
    n_e, _, d_f = w_gate.shape[1:]
    o_ab = 4 * d_a
    o_qb = o_ab + n_ab
    o_kvb = o_qb + d_b

    cc = jnp.zeros((MOD_ROWS, D), F32).at[:B].set(c).at[B].set(c_ctx)
    mod = _mod_call(cc, w_mod, b_mod).reshape(depth, MOD_ROWS, N_MOD, D)
    rope = _rope_tables(L)
    w_main, w_ab = _winprep_call(w_in, (o_ab, o_qb, o_kvb, o_kvb + kvb))

    x_l, x_c = x, ctx
    for l in range(depth):
        last = l == depth - 1
        ctx_out = not last
        mod_l = mod[l, :B]
        mod_c = mod[l, B:B + 1]
        ml = [mod_l[:, i:i + 1] for i in range(N_MOD)]
        mc = [mod_c[:, i:i + 1] for i in range(N_MOD)]
        gain_mix = norm_mix[l].reshape(1, D)
        p_l, ab_l = _inproj_call(x_l, ml[0], ml[1], gain_mix, w_main, w_ab, l)
        p_c, ab_c = _inproj_call(x_c.reshape(1, B * Lc, D), mc[0], mc[1], gain_mix, w_main, w_ab, l)
        p_c, ab_c = p_c.reshape(B, Lc, -1), ab_c.reshape(B, Lc, -1)

        cast_u = _cast_chunks(w_up, l, B * (L // QBLK))
        cast_gd = [_cast_chunks(w, l, B * (L // GRID_W)) for w in (w_gate, w_down)]
        oA_c, oA_l = _gdn_call(p_c, p_l, ab_c, ab_l, conv_a[l], a_log[l], dt_bias[l], gdn_norm[l], h_a)
        oB_l, *wu = _gqa_call(p_l, p_l, p_c, sink_b[l], rope, D, d_b, True, [cast_u] if cast_u else None)
        tbl = _na_bias_table(rpb_c[l])
        oC_l, *wgd = _na_call(p_l, p_l, p_c, tbl, D, d_c, True, cast_gd if all(cast_gd) else None)
        wu = wu[0].reshape(n_e, D, d_f) if wu else w_up[l].astype(BF16)
        wg = wgd[0].reshape(n_e, D, d_f) if wgd else w_gate[l].astype(BF16)
        wd = wgd[1].reshape(n_e, d_f, D) if wgd else w_down[l].astype(BF16)
        if ctx_out:
            oB_c, = _gqa_call(p_c, p_l, p_c, sink_b[l], rope, D, d_b, False)
            oC_c, = _na_call(p_c, p_l, p_c, tbl, D, d_c, False)

        w_o = w_out[l].astype(BF16)
        w_r = jnp.pad(jnp.concatenate([w_router_group[l], w_router_expert[l]], -1),
                      ((0, 0), (0, ROUTER_COLS - N_GROUPS - N_EXPERTS)))
        w_r_hi = w_r.astype(BF16)
        w_r = jnp.concatenate([w_r_hi, (w_r - w_r_hi.astype(F32)).astype(BF16)], -1)
        b_r = jnp.pad(jnp.concatenate([b_router_group[l], b_router_expert[l]], -1),
                      (0, ROUTER_COLS - N_GROUPS - N_EXPERTS)).reshape(1, ROUTER_COLS)
        gain_ffn = norm_ffn[l].reshape(1, D)
        x_l, f_l, lg_l = _outproj_call((oA_l, oB_l, oC_l), x_l, ml[2], ml[3], ml[4], gain_ffn, w_o, w_r, b_r)
        if ctx_out:
            x_c, f_c, lg_c = _outproj_call((oA_c, oB_c, oC_c), x_c, mc[2], mc[3], mc[4], gain_ffn, w_o, w_r, b_r)
            f_all = jnp.concatenate([f_c.reshape(B * Lc, D), f_l.reshape(B * L, D)], 0)
            lg_all = jnp.concatenate([lg_c.reshape(B * Lc, ROUTER_COLS), lg_l.reshape(B * L, ROUTER_COLS)], 0)
            y_c, y_l = _moe(f_all, lg_all, wg, wu, wd, B * Lc)
            y_c = [u.reshape(B, Lc, -1) for u in y_c]
            x_c = _resid_call(x_c, *y_c, mc[5], gain_ffn, False)
        else:
            _, y_l = _moe(f_l.reshape(B * L, D), lg_l.reshape(B * L, ROUTER_COLS), wg, wu, wd)
        y_l = [u.reshape(B, L, -1) for u in y_l]
        x_l = _resid_call(x_l, *y_l, ml[5], norm_final.reshape(1, D), last)
    return x_l
```

```python
import functools

import jax
import jax.numpy as jnp
from jax import lax
from jax.experimental import pallas as pl
from jax.experimental.pallas import tpu as pltpu

F32 = jnp.float32
BF16 = jnp.bfloat16
HIGHEST = lax.Precision.HIGHEST
_NT = (((1,), (1,)), ((), ()))

EPS = 1e-6
NEG_INF = -1e30
N_MOD = 6
GRID_W = 64
DH = 64
DK_A = 128
CONV_W = 5
CHUNK = 64
KV_B = 2
WIN = 128
QBLK = 128
ROPE_BASE = 10000.0
KH_MAX = 8
KW = 16
N_GROUPS = 4
E_PER_GROUP = 8
N_EXPERTS = N_GROUPS * E_PER_GROUP
TOP_K = 2

LANE = 128
SUBLANE = 8
BF16_ROWS = 16

MOD_ROWS = BF16_ROWS
MOD_TN = 1024
ROUTER_COLS = LANE
INPROJ_TM = 1024
INPROJ_SPLIT = 4
INPROJ_TN = (1280, 1024, 768, 640, 512, 384, 256, 128)
WPREP_ROWS = 256
ROW_TM = 256
MOE_TM = 256
ROUTE_TM = 256
GDN_HEADS = 2
GDN_GROUPS = (1, 2, 3, 6, 9)
INV_BLK = 8
INV_DOUBLINGS = INV_BLK.bit_length() - 1


def _params(sem, vmem_mb):
    return pltpu.CompilerParams(dimension_semantics=sem, vmem_limit_bytes=vmem_mb << 20)


def _mod_kernel(c_ref, w_ref, b_ref, o_ref):
    c = c_ref[...]
    s = c * jax.nn.sigmoid(c)
    o_ref[0] = jnp.dot(s, w_ref[0], preferred_element_type=F32, precision=HIGHEST) + b_ref[0]


def _mod_call(cc, w_mod, b_mod):
    depth, d, n = w_mod.shape
    tn = min(n, MOD_TN)
    return pl.pallas_call(
        _mod_kernel,
        grid=(depth, n // tn),
        in_specs=[pl.BlockSpec((MOD_ROWS, d), lambda l, j: (0, 0)),
                  pl.BlockSpec((1, d, tn), lambda l, j: (l, 0, j)),
                  pl.BlockSpec((1, 1, tn), lambda l, j: (l, 0, j))],
        out_specs=pl.BlockSpec((1, MOD_ROWS, tn), lambda l, j: (l, 0, j)),
        out_shape=jax.ShapeDtypeStruct((depth, MOD_ROWS, n), F32),
        compiler_params=_params(("parallel", "parallel"), 48),
        name="mod_proj",
    )(cc, w_mod, b_mod.reshape(depth, 1, n))


def _winprep_kernel(w_ref, m_ref, ab_ref, *, bounds):
    o_ab, o_qb, o_kvb, o_qc = bounds
    x = w_ref[0]
    m_ref[0] = jnp.concatenate([x[:, :o_ab], x[:, o_qb:o_kvb], x[:, o_qc:], x[:, o_kvb:o_qc]], 1).astype(BF16)
    ab = x[:, o_ab:o_qb]
    ab_ref[0] = jnp.concatenate([ab, jnp.zeros((x.shape[0], LANE - ab.shape[1]), F32)], 1).astype(BF16)


def _winprep_call(w_in, bounds):
    depth, d, n = w_in.shape
    n_main = n - (bounds[1] - bounds[0])
    rt = min(d, WPREP_ROWS)
    return pl.pallas_call(
        functools.partial(_winprep_kernel, bounds=bounds),
        grid=(depth, d // rt),
        in_specs=[pl.BlockSpec((1, rt, n), lambda l, i: (l, i, 0))],
        out_specs=[pl.BlockSpec((1, rt, n_main), lambda l, i: (l, i, 0)),
                   pl.BlockSpec((1, rt, LANE), lambda l, i: (l, i, 0))],
        out_shape=[jax.ShapeDtypeStruct((depth, d, n_main), BF16), jax.ShapeDtypeStruct((depth, d, LANE), BF16)],
        compiler_params=_params(("parallel", "parallel"), 48),
        name="w_in_prep",
    )(w_in)


def _cast_chunks(w, layer, n):
    depth, rows = w.shape[0], w.shape[1] * w.shape[2]
    if rows % n or (rows // n) % BF16_ROWS:
        return None
    return w.reshape(depth * n, rows // n, w.shape[3]), layer * n, n


def _cast_specs(cast, n_inner):
    chunks, first, n = cast
    blk = (1,) + chunks.shape[1:]
    if n_inner:
        src = pl.BlockSpec(blk, lambda b, j, *_: (first + b * n_inner + j, 0, 0))
        dst = pl.BlockSpec(blk, lambda b, j, *_: (b * n_inner + j, 0, 0))
    else:
        src = pl.BlockSpec(blk, lambda i, *_: (first + i, 0, 0))
        dst = pl.BlockSpec(blk, lambda i, *_: (i, 0, 0))
    return src, dst, jax.ShapeDtypeStruct((n,) + chunks.shape[1:], BF16)


def _add_casts(casts, n_inner, in_specs, out_specs, out_shape):
    extra = ()
    for cast in casts or ():
        src, dst, shape = _cast_specs(cast, n_inner)
        in_specs.append(src)
        out_specs.append(dst)
        out_shape.append(shape)
        extra += (cast[0],)
    return extra


def _do_casts(rest):
    n_cast = len(rest) // 2
    for cast_src, cast_dst in zip(rest[:n_cast], rest[n_cast + 1:]):
        cast_dst[...] = cast_src[...].astype(BF16)
    return rest[n_cast]


def _inproj_kernel(x_ref, shift_ref, scale_ref, g_ref, w_ref, wab_ref, o_ref, ab_ref, h_ref):
    first = pl.program_id(2) == 0

    @pl.when(first)
    def _():
        tm = x_ref.shape[1]
        nb = INPROJ_SPLIT if tm % (INPROJ_SPLIT * BF16_ROWS) == 0 else 1
        for q in range(nb):
            rows = slice(q * (tm // nb), (q + 1) * (tm // nb))
            x = x_ref[0, rows, :]
            y = x * lax.rsqrt(jnp.mean(x * x, -1, keepdims=True) + EPS) * g_ref[...]
            h = (y * (1.0 + scale_ref[0]) + shift_ref[0]).astype(BF16)
            h_ref[rows, :] = h
            ab_ref[0, rows, :] = jnp.dot(h, wab_ref[0], preferred_element_type=F32)
            o_ref[0, rows, :] = jnp.dot(h, w_ref[0], preferred_element_type=F32).astype(o_ref.dtype)

    @pl.when(jnp.logical_not(first))
    def _():
        o_ref[0] = jnp.dot(h_ref[...], w_ref[0], preferred_element_type=F32).astype(o_ref.dtype)


def _inproj_call(x, shift, scale, gain, w_main, w_ab, layer):
    bn, seq, d = x.shape
    n = w_main.shape[2]
    tm = min(seq, INPROJ_TM)
    tn = next((t for t in INPROJ_TN if n % t == 0), n)
    per_batch = shift.shape[0] > 1
    mod_map = (lambda b, i, j: (b, 0, 0)) if per_batch else (lambda b, i, j: (0, 0, 0))
    return pl.pallas_call(
        _inproj_kernel,
        grid=(bn, seq // tm, n // tn),
        in_specs=[pl.BlockSpec((1, tm, d), lambda b, i, j: (b, i, 0)),
                  pl.BlockSpec((1, 1, d), mod_map),
                  pl.BlockSpec((1, 1, d), mod_map),
                  pl.BlockSpec((1, d), lambda b, i, j: (0, 0)),
                  pl.BlockSpec((1, d, tn), lambda b, i, j: (layer, 0, j)),
                  pl.BlockSpec((1, d, LANE), lambda b, i, j: (layer, 0, 0))],
        out_specs=[pl.BlockSpec((1, tm, tn), lambda b, i, j: (b, i, j)),
                   pl.BlockSpec((1, tm, LANE), lambda b, i, j: (b, i, 0))],
        out_shape=[jax.ShapeDtypeStruct((bn, seq, n), BF16),
                   jax.ShapeDtypeStruct((bn, seq, LANE), F32)],
        scratch_shapes=[pltpu.VMEM((tm, d), BF16)],
        compiler_params=_params(("parallel", "parallel", "arbitrary"), 56),
        name="in_proj",
    )(x, shift, scale, gain, w_main, w_ab)


def _outproj_kernel(oa_ref, ob_ref, oc_ref, x_ref, gate_ref, shift_ref, scale_ref, g_ref, w_ref, wr_ref, br_ref,
                    xn_ref, f_ref, lg_ref):
    acc, r0 = None, 0
    for o_ref in (oa_ref, ob_ref, oc_ref):
        r1 = r0 + o_ref.shape[2]
        part = jnp.dot(o_ref[0], w_ref[r0:r1, :], preferred_element_type=F32)
        acc = part if acc is None else acc + part
        r0 = r1
    xn = x_ref[0] + gate_ref[0] * acc
    xn_ref[0] = xn
    y = xn * lax.rsqrt(jnp.mean(xn * xn, -1, keepdims=True) + EPS) * g_ref[...]
    f = y * (1.0 + scale_ref[0]) + shift_ref[0]
    f_hi = f.astype(BF16)
    f_ref[0] = f_hi
    f_lo = (f - f_hi.astype(F32)).astype(BF16)
    hh_hl = jnp.dot(f_hi, wr_ref[...], preferred_element_type=F32)
    lh = jnp.dot(f_lo, wr_ref[:, :ROUTER_COLS], preferred_element_type=F32)
    lg_ref[0] = hh_hl[:, :ROUTER_COLS] + hh_hl[:, ROUTER_COLS:] + lh + br_ref[...]


def _outproj_call(os, x, gate, shift, scale, gain, w_out, w_r, b_r):
    bn, seq, d = x.shape
    tm = min(seq, ROW_TM)
    per_batch = gate.shape[0] > 1
    mod_map = (lambda b, i: (b, 0, 0)) if per_batch else (lambda b, i: (0, 0, 0))
    row = pl.BlockSpec((1, tm, d), lambda b, i: (b, i, 0))
    o_specs = [pl.BlockSpec((1, tm, o.shape[2]), lambda b, i: (b, i, 0)) for o in os]
    return pl.pallas_call(
        _outproj_kernel,
        grid=(bn, seq // tm),
        in_specs=[*o_specs, row,
                  pl.BlockSpec((1, 1, d), mod_map),
                  pl.BlockSpec((1, 1, d), mod_map),
                  pl.BlockSpec((1, 1, d), mod_map),
                  pl.BlockSpec((1, d), lambda b, i: (0, 0)),
                  pl.BlockSpec((d, d), lambda b, i: (0, 0)),
                  pl.BlockSpec((d, 2 * ROUTER_COLS), lambda b, i: (0, 0)),
                  pl.BlockSpec((1, ROUTER_COLS), lambda b, i: (0, 0))],
        out_specs=[row, row, pl.BlockSpec((1, tm, ROUTER_COLS), lambda b, i: (b, i, 0))],
        out_shape=[jax.ShapeDtypeStruct((bn, seq, d), F32),
                   jax.ShapeDtypeStruct((bn, seq, d), BF16),
                   jax.ShapeDtypeStruct((bn, seq, ROUTER_COLS), F32)],
        compiler_params=_params(("parallel", "parallel"), 56),
        name="out_proj",
    )(*os, x, gate, shift, scale, gain, w_out, w_r, b_r)


def _moe_kernel(te_ref, nu_ref, x_ref, wg_ref, wu_ref, wd_ref, *rest):
    o_ref = _do_casts(rest)
    used = pl.program_id(0) < nu_ref[0]

    @pl.when(used)
    def _():
        x = x_ref[...]
        g = jnp.dot(x, wg_ref[0], preferred_element_type=F32)
        u = jnp.dot(x, wu_ref[0], preferred_element_type=F32)
        a = (g * jax.nn.sigmoid(g) * u).astype(BF16)
        o_ref[...] = jnp.dot(a, wd_ref[0], preferred_element_type=F32).astype(o_ref.dtype)

    @pl.when(jnp.logical_not(used))
    def _():
        o_ref[...] = jnp.zeros_like(o_ref)


def _moe_call(tile_expert, n_used, buf, w_gate, w_up, w_down, tm, cast=None):
    p, d = buf.shape
    f = w_gate.shape[-1]
    in_specs = [pl.BlockSpec((tm, d), lambda i, te, nu: (i, 0)),
                pl.BlockSpec((1, d, f), lambda i, te, nu: (te[i], 0, 0)),
                pl.BlockSpec((1, d, f), lambda i, te, nu: (te[i], 0, 0)),
                pl.BlockSpec((1, f, d), lambda i, te, nu: (te[i], 0, 0))]
    out_specs = [pl.BlockSpec((tm, d), lambda i, te, nu: (i, 0))]
    out_shape = [jax.ShapeDtypeStruct((p, d), BF16)]
    extra = _add_casts([cast] if cast else None, 0, in_specs, out_specs, out_shape)
    grid_spec = pltpu.PrefetchScalarGridSpec(
        num_scalar_prefetch=2, grid=(p // tm,), in_specs=in_specs, out_specs=out_specs)
    return pl.pallas_call(
        _moe_kernel,
        grid_spec=grid_spec,
        out_shape=out_shape,
        compiler_params=_params(("arbitrary",), 60),
        name="moe_experts",
    )(tile_expert, n_used, buf, w_gate, w_up, w_down, *extra)


def _route_kernel(lg_ref, ids_ref, wts_ref, cnt_ref, carry_ref):
    @pl.when(pl.program_id(0) == 0)
    def _():
        carry_ref[...] = jnp.zeros_like(carry_ref)

    lg = lg_ref[...]
    tm = lg.shape[0]
    lane = lax.broadcasted_iota(jnp.int32, lg.shape, 1)
    big = jnp.int32(ROUTER_COLS)
    is_g = lane < N_GROUPS
    gmax = jnp.max(jnp.where(is_g, lg, -jnp.inf), -1, keepdims=True)
    g_sel = jnp.min(jnp.where(is_g & (lg == gmax), lane, big), -1, keepdims=True)
    p_g = 1.0 / jnp.sum(jnp.where(is_g, jnp.exp(lg - gmax), 0.0), -1, keepdims=True)
    lo = N_GROUPS + E_PER_GROUP * g_sel
    in_e = (lane >= lo) & (lane < lo + E_PER_GROUP)
    el = jnp.where(in_e, lg, -jnp.inf)
    v1 = jnp.max(el, -1, keepdims=True)
    i1 = jnp.min(jnp.where(el == v1, lane, big), -1, keepdims=True)
    el2 = jnp.where(lane == i1, -jnp.inf, el)
    v2 = jnp.max(el2, -1, keepdims=True)
    i2 = jnp.min(jnp.where(el2 == v2, lane, big), -1, keepdims=True)
    t = jnp.exp(v2 - v1)
    w1 = p_g / (1.0 + t)
    w2 = w1 * t
    e1 = i1 - N_GROUPS
    e2 = i2 - N_GROUPS
    hit1 = lane == e1
    hit2 = lane == e2
    onehot = jnp.where(hit1 | hit2, 1.0, 0.0)
    ri = lax.broadcasted_iota(jnp.int32, (tm, tm), 0)
    ci = lax.broadcasted_iota(jnp.int32, (tm, tm), 1)
    below = jnp.where(ci < ri, 1.0, 0.0).astype(BF16)
    before = carry_ref[...] + jnp.dot(below, onehot.astype(BF16), preferred_element_type=F32)
    r1 = jnp.sum(jnp.where(hit1, before, 0.0), -1, keepdims=True).astype(jnp.int32)
    r2 = jnp.sum(jnp.where(hit2, before, 0.0), -1, keepdims=True).astype(jnp.int32)
    ids_ref[...] = jnp.where(lane == 0, e1, jnp.where(lane == 1, e2, jnp.where(lane == 2, r1, jnp.where(lane == 3, r2, 0))))
    wts_ref[...] = jnp.where(lane == 0, w1, jnp.where(lane == 1, w2, 0.0))
    carry_ref[...] = carry_ref[...] + jnp.sum(onehot, 0, keepdims=True)
    cnt_ref[...] = carry_ref[...]


def _route_call(logits):
    t = logits.shape[0]
    tm = min(t, ROUTE_TM)
    row = pl.BlockSpec((tm, ROUTER_COLS), lambda i: (i, 0))
    return pl.pallas_call(
        _route_kernel,
        grid=(t // tm,),
        in_specs=[row],
        out_specs=[row, row, pl.BlockSpec((1, ROUTER_COLS), lambda i: (0, 0))],
        out_shape=[jax.ShapeDtypeStruct((t, ROUTER_COLS), jnp.int32),
                   jax.ShapeDtypeStruct((t, ROUTER_COLS), F32),
                   jax.ShapeDtypeStruct((1, ROUTER_COLS), F32)],
        scratch_shapes=[pltpu.VMEM((1, ROUTER_COLS), F32)],
        compiler_params=_params(("arbitrary",), 32),
        name="route",
    )(logits)


def _resid_kernel(x_ref, y0_ref, y1_ref, w_ref, gate_ref, g_ref, o_ref, *, final_norm):
    w = w_ref[0]
    y = y0_ref[0].astype(F32) * w[:, 0:1] + y1_ref[0].astype(F32) * w[:, 1:2]
    xn = x_ref[0] + gate_ref[0] * y
    if final_norm:
        xn = xn * lax.rsqrt(jnp.mean(xn * xn, -1, keepdims=True) + EPS) * g_ref[...]
    o_ref[0] = xn


def _resid_call(x, y0, y1, w, gate, gain, final_norm):
    bn, seq, d = x.shape
    tm = min(seq, ROW_TM)
    per_batch = gate.shape[0] > 1
    mod_map = (lambda b, i: (b, 0, 0)) if per_batch else (lambda b, i: (0, 0, 0))
    row = pl.BlockSpec((1, tm, d), lambda b, i: (b, i, 0))
    return pl.pallas_call(
        functools.partial(_resid_kernel, final_norm=final_norm),
        grid=(bn, seq // tm),
        in_specs=[row, row, row, pl.BlockSpec((1, tm, ROUTER_COLS), lambda b, i: (b, i, 0)),
                  pl.BlockSpec((1, 1, d), mod_map), pl.BlockSpec((1, d), lambda b, i: (0, 0))],
        out_specs=row,
        out_shape=jax.ShapeDtypeStruct((bn, seq, d), F32),
        compiler_params=_params(("parallel", "parallel"), 48),
        name="moe_residual",
    )(x, y0, y1, w, gate, gain)


def _softplus(x):
    return jnp.maximum(x, 0.0) + jnp.log1p(jnp.exp(-jnp.abs(x)))


def _gdn_prep(x_ref, w_ref, lanes, dst_ref, off, mode, xp_ref):
    n = x_ref.shape[1]
    pad = CONV_W // 2
    edge = jnp.zeros((SUBLANE, DK_A), F32)
    xp_ref[0:SUBLANE, :] = edge
    xp_ref[n + SUBLANE:n + 2 * SUBLANE, :] = edge
    xp_ref[SUBLANE:n + SUBLANE, :] = x_ref[0, :, lanes].astype(F32)
    acc = None
    for o in range(-pad, pad + 1):
        term = xp_ref[SUBLANE + o:SUBLANE + o + n, :] * w_ref[o + pad:o + pad + 1, lanes]
        acc = term if acc is None else acc + term
    y = acc * jax.nn.sigmoid(acc)
    if mode != "v":
        y = y * lax.rsqrt(jnp.sum(y * y, -1, keepdims=True) + EPS)
    if mode == "q":
        y = y * DK_A ** -0.5
    dst_ref[off:off + n, :] = y


def _gdn_kernel(alog_ref, dtb_ref,
                qc_ref, kc_ref, vc_ref, zc_ref, ql_ref, kl_ref, vl_ref, zl_ref,
                wq_ref, wk_ref, wv_ref, grow_ref, bcol_ref, nw_ref,
                oc_ref, ol_ref,
                q_s, k_s, v_s, o_s, u_s, l1_s, l2_s, gl_s, gc_s, s_s, xp_s, *, h_a, n_ctx):
    C = CHUNK
    hp = o_s.shape[0]
    lc = qc_ref.shape[1]
    n_tot = u_s.shape[2]
    n_pairs = n_tot // 2
    group = max(g for g in GDN_GROUPS if n_pairs % g == 0)

    NB = 4
    W = NB * C
    ii = lax.broadcasted_iota(jnp.int32, (C, W), 0)
    jl = lax.broadcasted_iota(jnp.int32, (C, W), 1)
    blk = jl // C
    jm = jl - blk * C
    eye4 = ii == jm
    tri4 = ((blk < 2) & (ii >= jm)) | ((blk >= 2) & (ii <= jm))
    eye4f = eye4.astype(F32)
    lane2 = lax.broadcasted_iota(jnp.int32, (C, 2 * C), 1)
    lo2 = lane2 < C
    i2 = lax.broadcasted_iota(jnp.int32, (2 * C, 2 * C), 0)
    j2 = lax.broadcasted_iota(jnp.int32, (2 * C, 2 * C), 1)
    same = (i2 // C) == (j2 // C)
    lvl_same = [(ii // (INV_BLK << k)) == (jm // (INV_BLK << k)) for k in range((C // INV_BLK).bit_length())]

    def block_diag(x):
        return jnp.concatenate([jnp.where(blk == b, x, 0.0) for b in range(NB)], 0).astype(BF16)

    def head_terms(hh):
        head = pl.program_id(1) * hp + hh
        lanes = slice(hh * DK_A, (hh + 1) * DK_A)
        _gdn_prep(qc_ref, wq_ref, lanes, q_s, 0, "q", xp_s)
        _gdn_prep(kc_ref, wk_ref, lanes, k_s, 0, "k", xp_s)
        _gdn_prep(vc_ref, wv_ref, lanes, v_s, 0, "v", xp_s)
        _gdn_prep(ql_ref, wq_ref, lanes, q_s, lc, "q", xp_s)
        _gdn_prep(kl_ref, wk_ref, lanes, k_s, lc, "k", xp_s)
        _gdn_prep(vl_ref, wv_ref, lanes, v_s, lc, "v", xp_s)

        for d in range(2):
            a = jnp.exp(jnp.full((1, 2 * C), alog_ref[d * h_a + head], F32))
            g = -a * _softplus(grow_ref[0, hh, d] + dtb_ref[d * h_a + head])
            tri = same & ((i2 <= j2) if d == 0 else (i2 >= j2))
            gc_s[d] = jnp.dot(g, tri.astype(F32), preferred_element_type=F32, precision=HIGHEST)

        def pair_setup(p):
            r0 = pl.multiple_of(p * 2 * C, 2 * C)
            q2 = q_s[pl.ds(r0, 2 * C), :]
            k2 = k_s[pl.ds(r0, 2 * C), :]
            v2 = v_s[pl.ds(r0, 2 * C), :]
            k16 = k2.astype(BF16)
            gk = lax.dot_general(k16, k16, _NT, preferred_element_type=F32)
            gq = lax.dot_general(q2.astype(BF16), k16, _NT, preferred_element_type=F32)
            kk2 = jnp.where(lo2, gk[:C], gk[C:])
            qk2 = jnp.where(lo2, gq[:C], gq[C:])
            kk4 = jnp.concatenate([kk2, kk2], 1)
            qk4 = jnp.concatenate([qk2, qk2], 1)
            grow = [gc_s[d, pl.ds(p, 1), :] for d in range(2)]
            rowb = jnp.concatenate([jnp.broadcast_to(g, (C, 2 * C)) for g in grow], 1)
            diag = jnp.where(eye4, rowb, 0.0)
            gcol, bcol, tot = [], [], []
            for b in range(NB):
                d, par = b // 2, b % 2
                half = diag[:, d * 2 * C:(d + 1) * 2 * C]
                gcol.append(jnp.sum(jnp.where(lo2 if par == 0 else ~lo2, half, 0.0), axis=1, keepdims=True))
                bcol.append(jax.nn.sigmoid(bcol_ref[0, hh, pl.ds(r0 + par * C, C), d:d + 1]))
                e = par * C + (C - 1 if d == 0 else 0)
                tot.append(grow[d][:, e:e + 1])
            gb = [jnp.broadcast_to(g, (C, 2 * C)) for g in gcol]
            bb = [jnp.broadcast_to(b_, (C, 2 * C)) for b_ in bcol]
            gcc4 = jnp.concatenate([jnp.where(lo2, gb[0], gb[1]), jnp.where(lo2, gb[2], gb[3])], 1)
            bcc4 = jnp.concatenate([jnp.where(lo2, bb[0], bb[1]), jnp.where(lo2, bb[2], bb[3])], 1)
            dec4 = jnp.where(tri4, jnp.exp(jnp.where(tri4, gcc4 - rowb, 0.0)), 0.0)
            m0 = jnp.where(eye4, 0.0, -(kk4 * bcc4 * dec4))
            return m0, (p, q2, k2, v2, qk4 * dec4, gb, bb, tot)

        def pair_finish(tinv4, aux):
            p, q2, k2, v2, attn4, gb, bb, tot = aux
            eb = [jnp.exp(g) for g in gb]
            ks = (k2[:C], k2[C:])
            vs = (v2[:C], v2[C:])
            qs = (q2[:C], q2[C:])
            rhs = jnp.concatenate([jnp.concatenate([vs[b % 2] * bb[b], ks[b % 2] * (bb[b] * eb[b])], 1)
                                   for b in range(NB)], 0).astype(BF16)
            uw = jnp.dot(block_diag(tinv4), rhs, preferred_element_type=F32)
            for b in range(NB):
                d, par = b // 2, b % 2
                c = 2 * p + par
                u_s[hh, d, c] = uw[b * C:(b + 1) * C, :DK_A]
                l1_s[hh, d, c] = jnp.concatenate([uw[b * C:(b + 1) * C, DK_A:], qs[par] * eb[b]], 0).astype(BF16)
                gl_s[hh, d, c] = jnp.broadcast_to(jnp.exp(tot[b]), (1, DK_A))
            for d in range(2):
                kdec = jnp.concatenate([ks[par] * jnp.exp(tot[2 * d + par] - gb[2 * d + par]) for par in range(2)], 0)
                l2_s[hh, d, p] = jnp.concatenate([attn4[:, d * 2 * C:(d + 1) * 2 * C], kdec.T], 0).astype(BF16)

        def group_terms(i, _):
            st = [pair_setup(i * group + g) for g in range(group)]
            m0s = [s[0] for s in st]
            mds = [jnp.where(lvl_same[0], m0, 0.0) for m0 in m0s]
            rs = [jnp.dot(md.astype(BF16), block_diag(md), preferred_element_type=F32) for md in mds]
            xs = [eye4f + md for md in mds]
            for _ in range(INV_DOUBLINGS - 2):
                qrs = [jnp.dot(jnp.concatenate([x, r], 0).astype(BF16), block_diag(r), preferred_element_type=F32)
                       for x, r in zip(xs, rs)]
                xs = [x + qr[:C] for x, qr in zip(xs, qrs)]
                rs = [qr[C:] for qr in qrs]
            xs = [x + jnp.dot(x.astype(BF16), block_diag(r), preferred_element_type=F32) for x, r in zip(xs, rs)]
            for lv in range(1, len(lvl_same)):
                offs = [jnp.where(lvl_same[lv] & ~lvl_same[lv - 1], m0, 0.0) for m0 in m0s]
                ys = [jnp.dot(x.astype(BF16), block_diag(off), preferred_element_type=F32) for x, off in zip(xs, offs)]
                xs = [x + jnp.dot(y.astype(BF16), block_diag(x), preferred_element_type=F32) for x, y in zip(xs, ys)]
            for tinv4, s in zip(xs, st):
                pair_finish(tinv4, s[1])
            return 0

        lax.fori_loop(0, n_pairs // group, group_terms, 0)

    for hh in range(hp):
        head_terms(hh)

    o_s[...] = jnp.zeros_like(o_s)
    s_s[...] = jnp.zeros_like(s_s)
    row_half = lax.broadcasted_iota(jnp.int32, (2 * C, DK_A), 0) // C

    def scan_step(i, _):
        c_b = jnp.where(i < n_ctx, n_ctx - 1 - i, n_ctx + n_tot - 1 - i)
        chains = [(hh, d, i if d == 0 else c_b) for hh in range(hp) for d in range(2)]
        r1s = [jnp.dot(l1_s[hh, d, c], s_s[hh, d].astype(BF16), preferred_element_type=F32)
               for hh, d, c in chains]
        v_exts = []
        for (hh, d, c), r1 in zip(chains, r1s):
            v_new = u_s[hh, d, c] - r1[:C]
            v_exts.append(jnp.where(row_half == c % 2, jnp.concatenate([v_new, v_new], 0), 0.0).astype(BF16))
        r2s = [jnp.dot(l2_s[hh, d, c // 2], v_ext, preferred_element_type=F32)
               for (hh, d, c), v_ext in zip(chains, v_exts)]
        for (hh, d, c), r1, r2 in zip(chains, r1s, r2s):
            rows = pl.ds(pl.multiple_of(c * C, C), C)
            o_s[hh, rows, :] = o_s[hh, rows, :] + r1[C:] + r2[:C]
            s_s[hh, d] = s_s[hh, d] * gl_s[hh, d, c] + r2[C:]
        return 0

    lax.fori_loop(0, n_tot, scan_step, 0)

    for hh in range(hp):
        lanes = slice(hh * DK_A, (hh + 1) * DK_A)
        for z_ref, o_ref, off in ((zc_ref, oc_ref, 0), (zl_ref, ol_ref, lc)):
            n = z_ref.shape[1]
            o = o_s[hh, off:off + n, :]
            o = o * lax.rsqrt(jnp.mean(o * o, -1, keepdims=True) + EPS) * nw_ref[...]
            z = z_ref[0, :, lanes].astype(F32)
            o_ref[0, :, lanes] = (o * (z * jax.nn.sigmoid(z))).astype(o_ref.dtype)


def _gdn_call(p_c, p_l, ab_c, ab_l, conv_w, a_log, dt_bias, norm_w, h_a):
    B, L, _ = p_l.shape
    lc = p_c.shape[1]
    t = lc + L
    n_tot, n_ctx = t // CHUNK, lc // CHUNK
    assert n_tot % 2 == 0
    hp = GDN_HEADS if h_a % GDN_HEADS == 0 else 1
    n_hg = h_a // hp
    wd = hp * DK_A
    ab = jnp.concatenate([ab_c, ab_l], 1)[..., :4 * h_a].reshape(B, t, 2, 2, h_a)
    g_row = ab[:, :, :, 0].transpose(0, 3, 2, 1).reshape(B, h_a, 2, n_tot // 2, 2 * CHUNK)
    b_col = ab[:, :, :, 1].transpose(0, 3, 1, 2)

    def col(k, n):
        return pl.BlockSpec((1, n, wd), lambda b, j, *_: (b, 0, k * n_hg + j))

    def tap(k):
        return pl.BlockSpec((CONV_W, wd), lambda b, j, *_: (0, k * n_hg + j))

    grid_spec = pltpu.PrefetchScalarGridSpec(
        num_scalar_prefetch=2,
        grid=(B, n_hg),
        in_specs=[col(0, lc), col(1, lc), col(2, lc), col(3, lc),
                  col(0, L), col(1, L), col(2, L), col(3, L),
                  tap(0), tap(1), tap(2),
                  pl.BlockSpec((1, hp, 2, n_tot // 2, 2 * CHUNK), lambda b, j, *_: (b, j, 0, 0, 0)),
                  pl.BlockSpec((1, hp, t, 2), lambda b, j, *_: (b, j, 0, 0)),
                  pl.BlockSpec((1, DK_A), lambda b, j, *_: (0, 0))],
        out_specs=[pl.BlockSpec((1, lc, wd), lambda b, j, *_: (b, 0, j)),
                   pl.BlockSpec((1, L, wd), lambda b, j, *_: (b, 0, j))],
        scratch_shapes=[pltpu.VMEM((t, DK_A), F32), pltpu.VMEM((t, DK_A), F32), pltpu.VMEM((t, DK_A), F32),
                        pltpu.VMEM((hp, t, DK_A), F32),
                        pltpu.VMEM((hp, 2, n_tot, CHUNK, DK_A), F32),
                        pltpu.VMEM((hp, 2, n_tot, 2 * CHUNK, DK_A), BF16),
                        pltpu.VMEM((hp, 2, n_tot // 2, CHUNK + DK_A, 2 * CHUNK), BF16),
                        pltpu.VMEM((hp, 2, n_tot, 1, DK_A), F32),
                        pltpu.VMEM((2, n_tot // 2, 2 * CHUNK), F32),
                        pltpu.VMEM((hp, 2, DK_A, DK_A), F32),
                        pltpu.VMEM((max(L, lc) + 2 * SUBLANE, DK_A), F32)],
    )
    return pl.pallas_call(
        functools.partial(_gdn_kernel, h_a=h_a, n_ctx=n_ctx),
        grid_spec=grid_spec,
        out_shape=[jax.ShapeDtypeStruct((B, lc, h_a * DK_A), BF16), jax.ShapeDtypeStruct((B, L, h_a * DK_A), BF16)],
        compiler_params=_params(("parallel", "parallel"), 56),
        name="gdn",
    )(a_log.reshape(-1), dt_bias.reshape(-1), p_c, p_c, p_c, p_c, p_l, p_l, p_l, p_l,
      conv_w, conv_w, conv_w, g_row, b_col, norm_w.reshape(1, DK_A))


def _softmax_rows(s, sink=None):
    m = jnp.max(s, -1, keepdims=True)
    if sink is not None:
        m = jnp.maximum(m, sink)
    e = jnp.exp(s - m)
    den = jnp.sum(e, -1, keepdims=True)
    if sink is not None:
        den = den + jnp.exp(sink - m)
    return (e * (1.0 / den)).astype(BF16)


def _gqa_kernel(sink_ref, q_ref, kvl_ref, kvc_ref, cos_ref, sa_ref, sb_ref, *rest, n_heads, latent):
    o_ref = _do_casts(rest)
    G = n_heads // KV_B
    P = 2 * DH
    n = pl.program_id(1)
    seq = kvl_ref.shape[1]
    lc = kvc_ref.shape[1]

    def rope(x, r0):
        reps = x.shape[1] // P
        rows = pl.ds(pl.multiple_of(r0, QBLK), QBLK)
        c, a, b = [jnp.concatenate([t[rows, :]] * reps, 1) if reps > 1 else t[rows, :]
                   for t in (cos_ref, sa_ref, sb_ref)]
        w = x.shape[1]
        return x * c + pltpu.roll(x, w - DH // 4, 1) * a + pltpu.roll(x, DH // 4, 1) * b

    q = q_ref[0].astype(F32) * DH ** -0.5
    kc = kvc_ref[0][:, :P].astype(F32)
    vc = kvc_ref[0][:, P:].astype(F32)
    if latent:
        q = rope(q, n * QBLK)
        ks, vs = [], []
        for o in (-1, 0, 1):
            s0 = jnp.clip((n + o) * QBLK, 0, seq - QBLK)
            kv = kvl_ref[0, pl.ds(pl.multiple_of(s0, QBLK), QBLK), :].astype(F32)
            ks.append(rope(kv[:, :P], s0))
            vs.append(kv[:, P:])
        k = jnp.concatenate(ks + [kc], 0)
        v = jnp.concatenate(vs + [vc], 0)
        nk = 3 * QBLK + lc
        qi = lax.broadcasted_iota(jnp.int32, (QBLK, nk), 0)
        kj = lax.broadcasted_iota(jnp.int32, (QBLK, nk), 1)
        kpos = n * QBLK - QBLK + kj
        rel = kj - QBLK - qi
        valid = ((rel >= -WIN) & (rel <= WIN) & (kpos >= 0) & (kpos < seq)) | (kj >= 3 * QBLK)
    else:
        k, v = kc, vc
        valid = None
    q = q.astype(BF16)
    k_sw = pltpu.roll(k, DH, 1).astype(BF16)
    v_sw = pltpu.roll(v, DH, 1).astype(BF16)
    k = k.astype(BF16)
    v = v.astype(BF16)
    lane = lax.broadcasted_iota(jnp.int32, (QBLK, P), 1)
    zero = jnp.zeros((QBLK, P), BF16)
    scores = []
    for h in range(n_heads):
        qp = q[:, (h // 2) * P:(h // 2 + 1) * P]
        qh = jnp.where(lane < DH, qp, zero) if h % 2 == 0 else jnp.where(lane >= DH, qp, zero)
        straight = (h // G == h % 2)
        scores.append(lax.dot_general(qh, k if straight else k_sw, _NT, preferred_element_type=F32))
    probs = []
    for h, s in enumerate(scores):
        if valid is not None:
            s = jnp.where(valid, s, NEG_INF)
        probs.append(_softmax_rows(s, sink_ref[h]))
    outs = [jnp.dot(pr, v if (h // G == h % 2) else v_sw, preferred_element_type=F32) for h, pr in enumerate(probs)]
    for p in range(n_heads // 2):
        o_ref[0, :, p * P:(p + 1) * P] = jnp.where(lane < DH, outs[2 * p], outs[2 * p + 1]).astype(o_ref.dtype)


def _gqa_call(p_q, p_l, p_c, sink, tables, d_model, d_b, latent, casts=None):
    B, nq, _ = p_q.shape
    L, lc = p_l.shape[1], p_c.shape[1]
    kvw = 2 * KV_B * DH
    n_heads = d_b // DH
    assert (n_heads // KV_B) % 2 == 0
    in_specs = [pl.BlockSpec((1, QBLK, d_b), lambda b, n, *_: (b, n, 2 * d_model // d_b)),
                pl.BlockSpec((1, L, kvw), lambda b, n, *_: (b, 0, 3 * d_model // kvw)),
                pl.BlockSpec((1, lc, kvw), lambda b, n, *_: (b, 0, 3 * d_model // kvw)),
                pl.BlockSpec((L, 2 * DH), lambda b, n, *_: (0, 0)),
                pl.BlockSpec((L, 2 * DH), lambda b, n, *_: (0, 0)),
                pl.BlockSpec((L, 2 * DH), lambda b, n, *_: (0, 0))]
    out_specs = [pl.BlockSpec((1, QBLK, d_b), lambda b, n, *_: (b, n, 0))]
    out_shape = [jax.ShapeDtypeStruct((B, nq, d_b), BF16)]
    extra = _add_casts(casts, nq // QBLK, in_specs, out_specs, out_shape)
    grid_spec = pltpu.PrefetchScalarGridSpec(
        num_scalar_prefetch=1, grid=(B, nq // QBLK), in_specs=in_specs, out_specs=out_specs)
    return pl.pallas_call(
        functools.partial(_gqa_kernel, n_heads=n_heads, latent=latent),
        grid_spec=grid_spec,
        out_shape=out_shape,
        compiler_params=_params(("parallel", "parallel"), 48),
        name="window_gqa" if latent else "ctx_gqa",
    )(sink, p_q, p_l, p_c, *tables, *extra)


def _rope_tables(n_tokens):
    t = jnp.arange(n_tokens)
    row = (t // GRID_W).astype(F32)
    col = (t % GRID_W).astype(F32)
    half = DH // 2
    inv = ROPE_BASE ** (-jnp.arange(0, half, 2, dtype=F32) / half)
    ang_r = row[:, None] * inv[None]
    ang_c = col[:, None] * inv[None]
    ang = jnp.concatenate([ang_r, ang_r, ang_c, ang_c], -1)
    cos, sin = jnp.cos(ang), jnp.sin(ang)
    first = (jnp.arange(DH) % half) < half // 2
    sa = jnp.where(first, -sin, 0.0)
    sb = jnp.where(first, 0.0, sin)
    return tuple(jnp.concatenate([u, u], -1) for u in (cos, sa, sb))


def _na_kernel(q_ref, k_ref, v_ref, kc_ref, vc_ref, tbl_ref, *rest, n_heads, rows, latent):
    o_ref = _do_casts(rest)
    P = 2 * DH
    r = pl.program_id(1)
    nq = q_ref.shape[1]
    nb = KH_MAX * GRID_W
    q = (q_ref[0].astype(F32) * DH ** -0.5).astype(BF16)
    lane = lax.broadcasted_iota(jnp.int32, (nq, P), 1)
    zero = jnp.zeros((nq, P), BF16)
    if latent:
        r0 = jnp.clip(r - KH_MAX // 2, 0, rows - KH_MAX)
        band = pl.ds(pl.multiple_of(r0 * GRID_W, GRID_W), nb)
        d0 = r0 - r + (KH_MAX - 1)
    scores, vals = [], []
    for p in range(n_heads // 2):
        cols = slice(p * P, (p + 1) * P)
        qp = q[:, cols]
        kc = kc_ref[0, :, cols]
        vc = vc_ref[0, :, cols]
        if latent:
            kk = jnp.concatenate([k_ref[0, band, cols], kc], 0)
            vals.append(jnp.concatenate([v_ref[0, band, cols], vc], 0))
        else:
            kk = kc
            vals.append(vc)
        for half in range(2):
            qh = jnp.where(lane < DH, qp, zero) if half == 0 else jnp.where(lane >= DH, qp, zero)
            scores.append(lax.dot_general(qh, kk, _NT, preferred_element_type=F32))
    probs = []
    for h, s in enumerate(scores):
        if latent:
            bias = jnp.concatenate([tbl_ref[h, d0 + 2 * t] for t in range(KH_MAX // 2)], 1)
            s = jnp.concatenate([s[:, :nb] + bias, s[:, nb:]], 1)
        probs.append(_softmax_rows(s))
    outs = [jnp.dot(pr, vals[h // 2], preferred_element_type=F32) for h, pr in enumerate(probs)]
    for p in range(n_heads // 2):
        o_ref[0, :, p * P:(p + 1) * P] = jnp.where(lane < DH, outs[2 * p], outs[2 * p + 1]).astype(o_ref.dtype)


def _na_bias_table(rpb):
    qc = jnp.arange(GRID_W)[:, None]
    kc = jnp.arange(GRID_W)[None, :]
    dc = jnp.clip(kc - qc + (KW - 1), 0, 2 * KW - 2)
    q_cs = jnp.clip(qc - KW // 2, 0, GRID_W - KW)
    in_win = (kc >= q_cs) & (kc < q_cs + KW)
    t = jnp.where(in_win[None, None], rpb[:, :, dc], NEG_INF)
    return jnp.concatenate([t[:, :-1], t[:, 1:]], -1)


def _na_call(p_q, p_l, p_c, tbl, d_model, d_c, latent, casts=None):
    B, nq, _ = p_q.shape
    L, lc = p_l.shape[1], p_c.shape[1]
    n_heads = d_c // DH
    rows = L // GRID_W
    assert rows >= KH_MAX and n_heads % 2 == 0
    cb = 2 * d_model // d_c
    qb = GRID_W
    in_specs = [pl.BlockSpec((1, qb, d_c), lambda b, r: (b, r, cb + 1)),
                pl.BlockSpec((1, L, d_c), lambda b, r: (b, 0, cb + 2)),
                pl.BlockSpec((1, L, d_c), lambda b, r: (b, 0, cb + 3)),
                pl.BlockSpec((1, lc, d_c), lambda b, r: (b, 0, cb + 2)),
                pl.BlockSpec((1, lc, d_c), lambda b, r: (b, 0, cb + 3)),
                pl.BlockSpec(tbl.shape, lambda b, r: (0, 0, 0, 0))]
    out_specs = [pl.BlockSpec((1, qb, d_c), lambda b, r: (b, r, 0))]
    out_shape = [jax.ShapeDtypeStruct((B, nq, d_c), BF16)]
    extra = _add_casts(casts, nq // qb, in_specs, out_specs, out_shape)
    return pl.pallas_call(
        functools.partial(_na_kernel, n_heads=n_heads, rows=rows, latent=latent),
        grid=(B, nq // qb),
        in_specs=in_specs,
        out_specs=out_specs,
        out_shape=out_shape,
        compiler_params=_params(("parallel", "parallel"), 48),
        name="nbr_attn" if latent else "ctx_attn",
    )(p_q, p_l, p_l, p_c, p_c, tbl, *extra)


def _moe_tiles(n_tokens):
    return -(-n_tokens * TOP_K // MOE_TM) + N_EXPERTS


def _moe(f, logits, w_gate, w_up, w_down, split=0, cast=None):
    T, d = f.shape
    tm = MOE_TM
    ids, wts, cnt = _route_call(logits)
    expert, rank = ids[:, :TOP_K], ids[:, TOP_K:2 * TOP_K]
    counts = cnt[0, :N_EXPERTS].astype(jnp.int32)
    padded = (counts + tm - 1) // tm * tm
    pends = jnp.cumsum(padded)
    pstarts = pends - padded
    start = jnp.sum(jnp.where(expert[..., None] == jnp.arange(N_EXPERTS), pstarts, 0), -1)
    dest = start + rank
    n_tiles = _moe_tiles(T)
    tile_expert = jnp.minimum(jnp.sum(pends[None, :] <= (jnp.arange(n_tiles) * tm)[:, None], -1),
                              N_EXPERTS - 1).astype(jnp.int32)
    n_used = (pends[-1] // tm).astype(jnp.int32).reshape(1)
    tok = jnp.broadcast_to(jnp.arange(T, dtype=jnp.int32)[:, None], (T, TOP_K))
    src_tok = (jnp.arange(n_tiles * tm, dtype=jnp.int32) % T).at[dest.reshape(-1)].set(tok.reshape(-1))
    out, *cast_out = _moe_call(tile_expert, n_used, f[src_tok], w_gate, w_up, w_down, tm, cast)
    parts = [(dest[:split], wts[:split]), (dest[split:], wts[split:])]
    return [(out[d_[:, 0]], out[d_[:, 1]], w_) for d_, w_ in parts] + cast_out


def kernel(x, c, ctx, c_ctx, w_mod, b_mod, norm_mix, norm_ffn, w_in, conv_a, a_log, dt_bias, gdn_norm, sink_b, rpb_c, w_out, w_router_group, b_router_group, w_router_expert, b_router_expert, w_gate, w_up, w_down, norm_final):
    B, L, D = x.shape
    Lc = ctx.shape[1]
    depth = w_mod.shape[0]
    d_a = D // 2
    d_b = D // 4
    d_c = D - d_a - d_b
    h_a = d_a // DK_A
    n_ab = 4 * h_a
    kvb = 2 * KV_B * DH
    n_e, _, d_f = w_gate.shape[1:]
    o_ab = 4 * d_a
    o_qb = o_ab + n_ab
    o_kvb = o_qb + d_b

    cc = jnp.zeros((MOD_ROWS, D), F32).at[:B].set(c).at[B].set(c_ctx)
    mod = _mod_call(cc, w_mod, b_mod).reshape(depth, MOD_ROWS, N_MOD, D)
    rope = _rope_tables(L)
    w_main, w_ab = _winprep_call(w_in, (o_ab, o_qb, o_kvb, o_kvb + kvb))

    x_l, x_c = x, ctx
    wd_early = None
    for l in range(depth):
        last = l == depth - 1
        ctx_out = not last
        mod_l = mod[l, :B]
        mod_c = mod[l, B:B + 1]
        ml = [mod_l[:, i:i + 1] for i in range(N_MOD)]
        mc = [mod_c[:, i:i + 1] for i in range(N_MOD)]
        gain_mix = norm_mix[l].reshape(1, D)
        p_l, ab_l = _inproj_call(x_l, ml[0], ml[1], gain_mix, w_main, w_ab, l)
        p_c, ab_c = _inproj_call(x_c.reshape(1, B * Lc, D), mc[0], mc[1], gain_mix, w_main, w_ab, l)
        p_c, ab_c = p_c.reshape(B, Lc, -1), ab_c.reshape(B, Lc, -1)

        cast_u = _cast_chunks(w_up, l, B * (L // QBLK))
        na_casts = (w_gate,) if wd_early is not None else (w_gate, w_down)
        cast_gd = [_cast_chunks(w, l, B * (L // GRID_W)) for w in na_casts]
        oA_c, oA_l = _gdn_call(p_c, p_l, ab_c, ab_l, conv_a[l], a_log[l], dt_bias[l], gdn_norm[l], h_a)
        oB_l, *wu = _gqa_call(p_l, p_l, p_c, sink_b[l], rope, D, d_b, True, [cast_u] if cast_u else None)
        tbl = _na_bias_table(rpb_c[l])
        oC_l, *wgd = _na_call(p_l, p_l, p_c, tbl, D, d_c, True, cast_gd if all(cast_gd) else None)
        wu = wu[0].reshape(n_e, D, d_f) if wu else w_up[l].astype(BF16)
        wg = wgd[0].reshape(n_e, D, d_f) if wgd else w_gate[l].astype(BF16)
        if wd_early is not None:
            wd = wd_early.reshape(n_e, d_f, D)
        else:
            wd = wgd[1].reshape(n_e, d_f, D) if wgd else w_down[l].astype(BF16)
        if ctx_out:
            oB_c, = _gqa_call(p_c, p_l, p_c, sink_b[l], rope, D, d_b, False)
            oC_c, = _na_call(p_c, p_l, p_c, tbl, D, d_c, False)

        w_o = w_out[l].astype(BF16)
        w_r = jnp.pad(jnp.concatenate([w_router_group[l], w_router_expert[l]], -1),
                      ((0, 0), (0, ROUTER_COLS - N_GROUPS - N_EXPERTS)))
        w_r_hi = w_r.astype(BF16)
        w_r = jnp.concatenate([w_r_hi, (w_r - w_r_hi.astype(F32)).astype(BF16)], -1)
        b_r = jnp.pad(jnp.concatenate([b_router_group[l], b_router_expert[l]], -1),
                      (0, ROUTER_COLS - N_GROUPS - N_EXPERTS)).reshape(1, ROUTER_COLS)
        gain_ffn = norm_ffn[l].reshape(1, D)
        x_l, f_l, lg_l = _outproj_call((oA_l, oB_l, oC_l), x_l, ml[2], ml[3], ml[4], gain_ffn, w_o, w_r, b_r)
        if ctx_out:
            x_c, f_c, lg_c = _outproj_call((oA_c, oB_c, oC_c), x_c, mc[2], mc[3], mc[4], gain_ffn, w_o, w_r, b_r)
            f_all = jnp.concatenate([f_c.reshape(B * Lc, D), f_l.reshape(B * L, D)], 0)
            lg_all = jnp.concatenate([lg_c.reshape(B * Lc, ROUTER_COLS), lg_l.reshape(B * L, ROUTER_COLS)], 0)
            cast_next = _cast_chunks(w_down, l + 1, _moe_tiles(B * (Lc + L)))
            y_c, y_l, *early = _moe(f_all, lg_all, wg, wu, wd, B * Lc, cast_next)
            wd_early = early[0] if early else None
            y_c = [u.reshape(B, Lc, -1) for u in y_c]
            x_c = _resid_call(x_c, *y_c, mc[5], gain_ffn, False)
        else:
            _, y_l = _moe(f_l.reshape(B * L, D), lg_l.reshape(B * L, ROUTER_COLS), wg, wu, wd)
            wd_early = None
        y_l = [u.reshape(B, L, -1) for u in y_l]
        x_l = _resid_call(x_l, *y_l, ml[5], norm_final.reshape(1, D), last)
    return x_l
```
